```python
import math
import jax, jax.numpy as jnp
from jax import lax
import numpy as np

D_MODEL = 2048
BATCH = 16
SEQ = 2048
DEPTH = 1
DEC_BATCH = 32
DEC_SEQ = 1
PAST_LEN = 16384
PAGE_SIZE = 128

HEAD_DIM = 128
HEADS_PER_GROUP = 4
ATT_GROUPS = ((128, 1), (512, 4), (2048, 16))
N_ATT_GROUPS = len(ATT_GROUPS)
N_ATT_HEADS = N_ATT_GROUPS * HEADS_PER_GROUP
ATT_WIDTH = N_ATT_HEADS * HEAD_DIM
ATT_OUT_WIDTH = HEADS_PER_GROUP * HEAD_DIM
ATT_SCALE = HEAD_DIM ** -0.5
ROPE_THETA = 10000.0
QBLOCK = 128
SSM_WIDTH = D_MODEL // 2
SSM_GROUP = 16
SSM_GROUPS = SSM_WIDTH // SSM_GROUP
SSM_STATE = 64
D_FF = ((8 * D_MODEL // 3 + 127) // 128) * 128
IN_WIDTH = 3 * ATT_WIDTH + SSM_WIDTH + 2 * D_MODEL
RMS_EPS = 1e-6

kernel_name = "dilated_attn_s5_gated_macaron_step"


def rmsnorm(x, gain):
    xf = x.astype(jnp.float32)
    y = xf * lax.rsqrt(jnp.mean(xf * xf, axis=-1, keepdims=True) + RMS_EPS)
    return (y * gain.astype(jnp.float32)).astype(x.dtype)


def rope(x, pos):
    half = HEAD_DIM // 2
    inv = jnp.power(ROPE_THETA, -jnp.arange(half, dtype=jnp.float32) * (2.0 / HEAD_DIM))
    ang = pos.astype(jnp.float32)[:, None] * inv[None, :]
    cos, sin = jnp.cos(ang)[:, None, :], jnp.sin(ang)[:, None, :]
    xf = x.astype(jnp.float32)
    x1, x2 = xf[..., :half], xf[..., half:]
    return jnp.concatenate([x1 * cos - x2 * sin, x2 * cos + x1 * sin], axis=-1).astype(x.dtype)


def swiglu(x, w_gate, w_up, w_down):
    return (jax.nn.silu(x @ w_gate) * (x @ w_up)) @ w_down


def masked_softmax_lse(s, mask):
    s = jnp.where(mask, s, -jnp.inf)
    mx = jnp.max(s, axis=-1, keepdims=True)
    p = jnp.exp(s - mx)
    den = jnp.sum(p, axis=-1, keepdims=True)
    return p / den, (mx + jnp.log(den))[..., 0]


def dilated_attn_prompt(q, k, v, dil, n_back):
    B, L, H, E = q.shape
    M = L // dil
    bq = math.gcd(M, QBLOCK)
    nblk = M // bq
    bk = bq + n_back
    qr = q.reshape(B, nblk, bq, dil, H, E)
    pad = ((0, 0), (n_back, 0), (0, 0), (0, 0), (0, 0))
    kp = jnp.pad(k.reshape(B, M, dil, H, E), pad)
    vp = jnp.pad(v.reshape(B, M, dil, H, E), pad)
    idx = jnp.arange(nblk)[:, None] * bq + jnp.arange(bk)[None, :]
    kb, vb = kp[:, idx], vp[:, idx]
    s = jnp.einsum('bnqrhe,bnkrhe->bnrhqk', qr, kb, preferred_element_type=jnp.float32) * ATT_SCALE
    qi = jnp.arange(bq)[:, None]
    ki = jnp.arange(bk)[None, :]
    rel = qi - ki + n_back
    kpos = jnp.arange(nblk)[:, None, None] * bq + ki[None] - n_back
    mask = (rel >= 0) & (rel <= n_back) & (kpos >= 0)
    p, lse = masked_softmax_lse(s, mask[None, :, None, None])
    o = jnp.einsum('bnrhqk,bnkrhe->bnqrhe', p, vb.astype(jnp.float32))
    return o.reshape(B, L, H, E), lse.transpose(0, 1, 4, 2, 3).reshape(B, L, H)


def dilated_attn_decode(q, k_all, v_all, buf_len, dil, n_back):
    S = q.shape[1]
    idx = buf_len + jnp.arange(S)[:, None] - jnp.arange(n_back + 1)[None, :] * dil
    valid = idx >= 0
    idxc = jnp.maximum(idx, 0)
    kg, vg = k_all[:, idxc], v_all[:, idxc]
    s = jnp.einsum('bshe,bsjhe->bhsj', q, kg, preferred_element_type=jnp.float32) * ATT_SCALE
    p, lse = masked_softmax_lse(s, valid[None, None])
    o = jnp.einsum('bhsj,bsjhe->bshe', p, vg.astype(jnp.float32))
    return o, lse.transpose(0, 2, 1)


def _ssm_combine(e1, e2):
    a1r, a1i, b1r, b1i = e1
    a2r, a2i, b2r, b2i = e2
    ar = a2r * a1r - a2i * a1i
    ai = a2r * a1i + a2i * a1r
    a2rb, a2ib = a2r[:, None], a2i[:, None]
    br = a2rb * b1r - a2ib * b1i + b2r
    bi = a2rb * b1i + a2ib * b1r + b2i
    return ar, ai, br, bi


def s5_ssm(u, x0, lam_re, lam_im, log_dt, b_re, b_im, c_re, c_im, d_skip):
    f32 = jnp.float32
    Bn, L, _ = u.shape
    uf = u.astype(f32).reshape(Bn, L, SSM_GROUPS, SSM_GROUP)
    lr, li = lam_re.astype(f32), lam_im.astype(f32)
    dt = jnp.exp(log_dt.astype(f32))[:, None]
    mag = jnp.exp(lr * dt)
    ar, ai = mag * jnp.cos(li * dt), mag * jnp.sin(li * dt)
    den = lr * lr + li * li
    fr = ((ar - 1.0) * lr + ai * li) / den
    fi = (ai * lr - (ar - 1.0) * li) / den
    br, bi = b_re.astype(f32), b_im.astype(f32)
    bbr = fr[..., None] * br - fi[..., None] * bi
    bbi = fr[..., None] * bi + fi[..., None] * br
    bu_r = jnp.einsum('blgc,gpc->lbgp', uf, bbr)
    bu_i = jnp.einsum('blgc,gpc->lbgp', uf, bbi)
    a_r = jnp.broadcast_to(ar, (L,) + ar.shape)
    a_i = jnp.broadcast_to(ai, (L,) + ai.shape)
    acc_r, acc_i, xr, xi = lax.associative_scan(_ssm_combine, (a_r, a_i, bu_r, bu_i), axis=0)
    if x0 is not None:
        x0r, x0i = x0[0].astype(f32)[None], x0[1].astype(f32)[None]
        pr, pi = acc_r[:, None], acc_i[:, None]
        xr, xi = xr + pr * x0r - pi * x0i, xi + pr * x0i + pi * x0r
    y = (jnp.einsum('lbgp,gcp->blgc', xr, c_re.astype(f32))
         - jnp.einsum('lbgp,gcp->blgc', xi, c_im.astype(f32))
         + d_skip.astype(f32).reshape(SSM_GROUPS, SSM_GROUP) * uf)
    return y.reshape(Bn, L, SSM_WIDTH).astype(u.dtype), xr[-1], xi[-1]


def layer(x, pos, kv_cache, ssm_state, p):
    B, L, _ = x.shape
    x = x + 0.5 * swiglu(rmsnorm(x, p['ffn1_norm']), p['ffn1_w_gate'], p['ffn1_w_up'], p['ffn1_w_down'])
    h = rmsnorm(x, p['mix_norm'])
    z = h @ p['w_in']
    q = z[..., :ATT_WIDTH].reshape(B, L, N_ATT_HEADS, HEAD_DIM)
    k = z[..., ATT_WIDTH:2 * ATT_WIDTH].reshape(B, L, N_ATT_HEADS, HEAD_DIM)
    v = z[..., 2 * ATT_WIDTH:3 * ATT_WIDTH].reshape(B, L, N_ATT_HEADS, HEAD_DIM)
    o_u = 3 * ATT_WIDTH
    u = z[..., o_u:o_u + SSM_WIDTH]
    gate_a = z[..., o_u + SSM_WIDTH:o_u + SSM_WIDTH + D_MODEL]
    gate_s = z[..., o_u + SSM_WIDTH + D_MODEL:]
    q = rope(rmsnorm(q, p['q_norm']), pos)
    k = rope(rmsnorm(k, p['k_norm']), pos)

    outs, lses, new_kv = [], [], []
    for g, (window, dil) in enumerate(ATT_GROUPS):
        hs = slice(g * HEADS_PER_GROUP, (g + 1) * HEADS_PER_GROUP)
        qg, kg, vg = q[:, :, hs], k[:, :, hs], v[:, :, hs]
        n_back = window // dil
        if kv_cache is None:
            o, lse = dilated_attn_prompt(qg, kg, vg, dil, n_back)
            keep = min(window, L)
            new_kv.append(jnp.stack([kg[:, L - keep:], vg[:, L - keep:]], axis=1))
        else:
            buf = kv_cache[g]
            buf_len = buf.shape[2]
            k_all = jnp.concatenate([buf[:, 0].astype(kg.dtype), kg], axis=1)
            v_all = jnp.concatenate([buf[:, 1].astype(vg.dtype), vg], axis=1)
            o, lse = dilated_attn_decode(qg, k_all, v_all, buf_len, dil, n_back)
            new_kv.append(jnp.stack([k_all[:, L:], v_all[:, L:]], axis=1))
        outs.append(o)
        lses.append(lse)
    w_grp = jax.nn.softmax(jnp.stack(lses, axis=-1), axis=-1)
    att = jnp.einsum('blhg,blhge->blhe', w_grp, jnp.stack(outs, axis=3))
    att = att.reshape(B, L, ATT_OUT_WIDTH).astype(x.dtype)

    y_ssm, s_re, s_im = s5_ssm(u, ssm_state, p['ssm_lambda_re'], p['ssm_lambda_im'], p['ssm_log_dt'],
                               p['ssm_b_re'], p['ssm_b_im'], p['ssm_c_re'], p['ssm_c_im'], p['ssm_d'])
    zs = jax.nn.gelu(y_ssm)
    s_out = zs * jax.nn.sigmoid(zs @ p['glu_w'] + p['glu_b'])

    merged = (jax.nn.sigmoid(gate_a) * (att @ p['w_attn_branch'])
              + jax.nn.sigmoid(gate_s) * (s_out @ p['w_ssm_branch']))
    x = x + merged @ p['w_out']
    x = x + 0.5 * swiglu(rmsnorm(x, p['ffn2_norm']), p['ffn2_w_gate'], p['ffn2_w_up'], p['ffn2_w_down'])
    return x, new_kv, (s_re.astype(x.dtype), s_im.astype(x.dtype))


def setup_inputs(seed: int = 0) -> dict:
    key = jax.random.key(seed)
    ks = iter(jax.random.split(key, 64))
    f32 = jnp.float32

    def normal(shape, scale):
        return jax.random.normal(next(ks), shape, f32) * scale

    def gain(shape):
        return 1.0 + normal(shape, 0.05)

    buf = [min(w, PAST_LEN) for (w, _) in ATT_GROUPS]
    kv_shape = lambda n: (DEPTH, DEC_BATCH, 2, n, HEADS_PER_GROUP, HEAD_DIM)
    n_idx = jnp.arange(SSM_STATE, dtype=f32)
    gp = (DEPTH, SSM_GROUPS, SSM_STATE)
    return {
        "x_prompt": normal((BATCH, SEQ, D_MODEL), 1.0),
        "x_sample": normal((DEC_BATCH, DEC_SEQ, D_MODEL), 1.0),
        "cache_kv_w128": normal(kv_shape(buf[0]), 1.0),
        "cache_kv_w512": normal(kv_shape(buf[1]), 1.0),
        "cache_kv_w2048": normal(kv_shape(buf[2]), 1.0),
        "state_ssm_re": normal((DEPTH, DEC_BATCH, SSM_GROUPS, SSM_STATE), 0.3),
        "state_ssm_im": normal((DEPTH, DEC_BATCH, SSM_GROUPS, SSM_STATE), 0.3),
        "ffn1_norm": gain((DEPTH, D_MODEL)),
        "ffn1_w_gate": normal((DEPTH, D_MODEL, D_FF), D_MODEL ** -0.5),
        "ffn1_w_up": normal((DEPTH, D_MODEL, D_FF), D_MODEL ** -0.5),
        "ffn1_w_down": normal((DEPTH, D_FF, D_MODEL), D_FF ** -0.5),
        "mix_norm": gain((DEPTH, D_MODEL)),
        "w_in": normal((DEPTH, D_MODEL, IN_WIDTH), D_MODEL ** -0.5),
        "q_norm": gain((DEPTH, N_ATT_HEADS, HEAD_DIM)),
        "k_norm": gain((DEPTH, N_ATT_HEADS, HEAD_DIM)),
        "ssm_lambda_re": -0.5 + normal(gp, 0.01),
        "ssm_lambda_im": math.pi * n_idx + normal(gp, 0.01),
        "ssm_log_dt": jax.random.uniform(next(ks), (DEPTH, SSM_GROUPS), f32, math.log(1e-3), math.log(1e-1)),
        "ssm_b_re": normal((DEPTH, SSM_GROUPS, SSM_STATE, SSM_GROUP), (2 * SSM_GROUP) ** -0.5),
        "ssm_b_im": normal((DEPTH, SSM_GROUPS, SSM_STATE, SSM_GROUP), (2 * SSM_GROUP) ** -0.5),
        "ssm_c_re": normal((DEPTH, SSM_GROUPS, SSM_GROUP, SSM_STATE), SSM_STATE ** -0.5),
        "ssm_c_im": normal((DEPTH, SSM_GROUPS, SSM_GROUP, SSM_STATE), SSM_STATE ** -0.5),
        "ssm_d": normal((DEPTH, SSM_WIDTH), 1.0),
        "glu_w": normal((DEPTH, SSM_WIDTH, SSM_WIDTH), SSM_WIDTH ** -0.5),
        "glu_b": normal((DEPTH, SSM_WIDTH), 0.01),
        "w_attn_branch": normal((DEPTH, ATT_OUT_WIDTH, D_MODEL), ATT_OUT_WIDTH ** -0.5),
        "w_ssm_branch": normal((DEPTH, SSM_WIDTH, D_MODEL), SSM_WIDTH ** -0.5),
        "w_out": normal((DEPTH, D_MODEL, D_MODEL), D_MODEL ** -0.5),
        "ffn2_norm": gain((DEPTH, D_MODEL)),
        "ffn2_w_gate": normal((DEPTH, D_MODEL, D_FF), D_MODEL ** -0.5),
        "ffn2_w_up": normal((DEPTH, D_MODEL, D_FF), D_MODEL ** -0.5),
        "ffn2_w_down": normal((DEPTH, D_FF, D_MODEL), D_FF ** -0.5),
    }


def reference(x_prompt, x_sample, cache_kv_w128, cache_kv_w512, cache_kv_w2048, state_ssm_re, state_ssm_im,
              ffn1_norm, ffn1_w_gate, ffn1_w_up, ffn1_w_down, mix_norm, w_in, q_norm, k_norm,
              ssm_lambda_re, ssm_lambda_im, ssm_log_dt, ssm_b_re, ssm_b_im, ssm_c_re, ssm_c_im, ssm_d,
              glu_w, glu_b, w_attn_branch, w_ssm_branch, w_out,
              ffn2_norm, ffn2_w_gate, ffn2_w_up, ffn2_w_down):
    pos_p = jnp.arange(x_prompt.shape[1], dtype=jnp.int32)
    pos_s = PAST_LEN + jnp.arange(x_sample.shape[1], dtype=jnp.int32)
    yp, ys = x_prompt, x_sample
    new_p = [[] for _ in range(5)]
    new_s = [[] for _ in range(5)]
    for l in range(DEPTH):
        p = dict(ffn1_norm=ffn1_norm[l], ffn1_w_gate=ffn1_w_gate[l], ffn1_w_up=ffn1_w_up[l],
                 ffn1_w_down=ffn1_w_down[l], mix_norm=mix_norm[l], w_in=w_in[l],
                 q_norm=q_norm[l], k_norm=k_norm[l],
                 ssm_lambda_re=ssm_lambda_re[l], ssm_lambda_im=ssm_lambda_im[l], ssm_log_dt=ssm_log_dt[l],
                 ssm_b_re=ssm_b_re[l], ssm_b_im=ssm_b_im[l], ssm_c_re=ssm_c_re[l], ssm_c_im=ssm_c_im[l],
                 ssm_d=ssm_d[l], glu_w=glu_w[l], glu_b=glu_b[l],
                 w_attn_branch=w_attn_branch[l], w_ssm_branch=w_ssm_branch[l], w_out=w_out[l],
                 ffn2_norm=ffn2_norm[l], ffn2_w_gate=ffn2_w_gate[l], ffn2_w_up=ffn2_w_up[l],
                 ffn2_w_down=ffn2_w_down[l])
        yp, kv_p, st_p = layer(yp, pos_p, None, None, p)
        ys, kv_s, st_s = layer(ys, pos_s, (cache_kv_w128[l], cache_kv_w512[l], cache_kv_w2048[l]),
                               (state_ssm_re[l], state_ssm_im[l]), p)
        for i, a in enumerate(kv_p + list(st_p)):
            new_p[i].append(a)
        for i, a in enumerate(kv_s + list(st_s)):
            new_s[i].append(a)
    kv128_p, kv512_p, kv2048_p, ssm_re_p, ssm_im_p = [jnp.stack(a) for a in new_p]
    kv128_s, kv512_s, kv2048_s, ssm_re_s, ssm_im_s = [jnp.stack(a) for a in new_s]
    return (yp, ys, kv128_p, kv512_p, kv2048_p, ssm_re_p, ssm_im_p,
            kv128_s, kv512_s, kv2048_s, ssm_re_s, ssm_im_s)
```

```python
import functools

import jax
import jax.numpy as jnp
from jax import lax
from jax.experimental import pallas as pl
from jax.experimental.pallas import tpu as pltpu

F32 = jnp.float32
BF16 = jnp.bfloat16

HEAD_DIM = 128
HEADS_PER_GROUP = 4
GROUP_WIDTH = HEADS_PER_GROUP * HEAD_DIM
ATT_GROUPS = ((128, 1), (512, 4), (2048, 16))
N_GROUPS = len(ATT_GROUPS)
ATT_WIDTH = N_GROUPS * GROUP_WIDTH
N_BACK = 128
QBLOCK = 128
ATT_SCALE = HEAD_DIM ** -0.5
ROPE_THETA = 10000.0
RMS_EPS = 1e-6
PAST_LEN = 16384
SSM_GROUP = 16
SSM_STATE = 64
SSM_GROUPS_PER_BLOCK = 8
LANES = 128
COL_TILE = 512
VMEM_LIMIT = 56 * 1024 * 1024


def _params(semantics):
    return pltpu.CompilerParams(dimension_semantics=semantics, vmem_limit_bytes=VMEM_LIMIT)


def _rms_rows(x, gain):
    ms = jnp.mean(x * x, axis=-1, keepdims=True)
    return x * lax.rsqrt(ms + RMS_EPS) * gain


def _ffn_kernel(x_ref, g_ref, wg_ref, wu_ref, wd_ref, o_ref, h_ref):
    @pl.when(pl.program_id(1) == 0)
    def _():
        x = x_ref[...]
        h_ref[...] = _rms_rows(x, g_ref[...]).astype(BF16)
        o_ref[...] = x

    h = h_ref[...]
    g = jnp.dot(h, wg_ref[...], preferred_element_type=F32)
    u = jnp.dot(h, wu_ref[...], preferred_element_type=F32)
    a = (g * jax.nn.sigmoid(g) * u).astype(BF16)
    o_ref[...] += 0.5 * jnp.dot(a, wd_ref[...], preferred_element_type=F32)


def _ffn(x, gain, wg, wu, wd, *, tm, tf):
    m, d = x.shape
    fp = wg.shape[1]
    return pl.pallas_call(
        _ffn_kernel,
        grid=(m // tm, fp // tf),
        in_specs=[
            pl.BlockSpec((tm, d), lambda i, f: (i, 0)),
            pl.BlockSpec((1, d), lambda i, f: (0, 0)),
            pl.BlockSpec((d, tf), lambda i, f: (0, f)),
            pl.BlockSpec((d, tf), lambda i, f: (0, f)),
            pl.BlockSpec((tf, d), lambda i, f: (f, 0)),
        ],
        out_specs=pl.BlockSpec((tm, d), lambda i, f: (i, 0)),
        out_shape=jax.ShapeDtypeStruct((m, d), F32),
        scratch_shapes=[pltpu.VMEM((tm, d), BF16)],
        compiler_params=_params(("parallel", "arbitrary")),
        name="ffn",
    )(x, gain.reshape(1, d), wg, wu, wd)


def _ffn_weights(wg, wu, wd, tf):
    f = wg.shape[1]
    pad = (-f) % tf
    wg = jnp.pad(wg.astype(BF16), ((0, 0), (0, pad)))
    wu = jnp.pad(wu.astype(BF16), ((0, 0), (0, pad)))
    wd = jnp.pad(wd.astype(BF16), ((0, pad), (0, 0)))
    return wg, wu, wd


_J_K, _J_V, _J_U, _J_GATE, _J_END = 3, 6, 9, 11, 19


def _inproj_kernel(x_ref, g_ref, w_ref, qkg_ref, cos_ref, sin_ref,
                   q_ref, kv0_ref, kv1_ref, kv2_ref, u_ref, gate_ref, h_ref):
    j = pl.program_id(1)

    @pl.when(j == 0)
    def _():
        h_ref[...] = _rms_rows(x_ref[...], g_ref[...]).astype(BF16)

    z = jnp.dot(h_ref[...], w_ref[...], preferred_element_type=F32)
    kv_refs = (kv0_ref, kv1_ref, kv2_ref)

    def norm_rope(z):
        cos, sin = cos_ref[...], sin_ref[...]
        gain = qkg_ref[...]
        heads = []
        for h in range(HEADS_PER_GROUP):
            sl = slice(h * HEAD_DIM, (h + 1) * HEAD_DIM)
            y = _rms_rows(z[:, sl], gain[:, sl])
            heads.append(y * cos + pltpu.roll(y, HEAD_DIM // 2, axis=1) * sin)
        return jnp.concatenate(heads, axis=1)

    @pl.when(j < _J_K)
    def _():
        q_ref[...] = norm_rope(z).astype(q_ref.dtype)

    for g in range(N_GROUPS):
        @pl.when(j == _J_K + g)
        def _(g=g):
            kv_refs[g][...] = norm_rope(z)

        @pl.when(j == _J_V + g)
        def _(g=g):
            kv_refs[g][...] = z

    @pl.when((j >= _J_U) & (j < _J_GATE))
    def _():
        u_ref[...] = z

    @pl.when(j >= _J_GATE)
    def _():
        gate_ref[...] = z


def _kv_slot(j, g):
    return jnp.where(j > _J_K + g, 1, 0)


def _inproj_common_specs(d, tm, nt):
    return [
        pl.BlockSpec((tm, d), lambda i, j: (i, 0)),
        pl.BlockSpec((1, d), lambda i, j: (0, 0)),
        pl.BlockSpec((d, COL_TILE), lambda i, j: (0, j)),
        pl.BlockSpec((1, COL_TILE), lambda i, j: (0, jnp.minimum(j, _J_V - 1))),
        pl.BlockSpec((tm, HEAD_DIM), lambda i, j: (i % nt, 0)),
        pl.BlockSpec((tm, HEAD_DIM), lambda i, j: (i % nt, 0)),
    ]


def _inproj_prompt(x, gain, w, qk_gain, cos, sin, *, b, l, tm):
    m, d = x.shape
    nt = l // tm
    ssm_w = (_J_GATE - _J_U) * COL_TILE
    gate_w = (_J_END - _J_GATE) * COL_TILE
    u_tiles = _J_GATE - _J_U
    out_specs = [pl.BlockSpec((tm, COL_TILE), lambda i, j: (i, jnp.minimum(j, _J_K - 1)))]
    for g in range(N_GROUPS):
        out_specs.append(pl.BlockSpec((None, None, tm, COL_TILE),
                                      lambda i, j, g=g: (i // nt, _kv_slot(j, g), i % nt, 0)))
    out_specs.append(pl.BlockSpec(
        (tm, COL_TILE), lambda i, j: (i % nt, (i // nt) * u_tiles + jnp.clip(j - _J_U, 0, u_tiles - 1))))
    out_specs.append(pl.BlockSpec(
        (tm, COL_TILE), lambda i, j: (i, jnp.clip(j - _J_GATE, 0, _J_END - _J_GATE - 1))))
    out_shape = [jax.ShapeDtypeStruct((m, ATT_WIDTH), BF16)]
    out_shape += [jax.ShapeDtypeStruct((b, 2, l, GROUP_WIDTH), F32)] * N_GROUPS
    out_shape += [jax.ShapeDtypeStruct((l, b * ssm_w), F32), jax.ShapeDtypeStruct((m, gate_w), F32)]
    return pl.pallas_call(
        _inproj_kernel,
        grid=(m // tm, _J_END),
        in_specs=_inproj_common_specs(d, tm, nt),
        out_specs=out_specs,
        out_shape=out_shape,
        scratch_shapes=[pltpu.VMEM((tm, d), BF16)],
        compiler_params=_params(("parallel", "arbitrary")),
        name="inproj_prompt",
    )(x, gain.reshape(1, d), w, qk_gain, cos, sin)


def _inproj_sample(x, gain, w, qk_gain, cos, sin):
    m, d = x.shape
    u_tiles = _J_GATE - _J_U
    out_specs = [pl.BlockSpec((m, COL_TILE), lambda i, j: (0, jnp.minimum(j, _J_K - 1)))]
    for g in range(N_GROUPS):
        out_specs.append(pl.BlockSpec((None, m, COL_TILE), lambda i, j, g=g: (_kv_slot(j, g), 0, 0)))
    out_specs.append(pl.BlockSpec((m, COL_TILE), lambda i, j: (0, jnp.clip(j - _J_U, 0, u_tiles - 1))))
    out_specs.append(pl.BlockSpec((m, COL_TILE), lambda i, j: (0, jnp.clip(j - _J_GATE, 0, _J_END - _J_GATE - 1))))
    out_shape = [jax.ShapeDtypeStruct((m, ATT_WIDTH), F32)]
    out_shape += [jax.ShapeDtypeStruct((2, m, GROUP_WIDTH), F32)] * N_GROUPS
    out_shape += [jax.ShapeDtypeStruct((m, u_tiles * COL_TILE), F32),
                  jax.ShapeDtypeStruct((m, (_J_END - _J_GATE) * COL_TILE), F32)]
    return pl.pallas_call(
        _inproj_kernel,
        grid=(1, _J_END),
        in_specs=_inproj_common_specs(d, m, 1),
        out_specs=out_specs,
        out_shape=out_shape,
        scratch_shapes=[pltpu.VMEM((m, d), BF16)],
        compiler_params=_params(("arbitrary", "arbitrary")),
        name="inproj_sample",
    )(x, gain.reshape(1, d), w, qk_gain, cos, sin)


def _rope_tables(pos):
    half = HEAD_DIM // 2
    inv = jnp.power(ROPE_THETA, -jnp.arange(half, dtype=F32) * (2.0 / HEAD_DIM))
    ang = pos.astype(F32)[:, None] * inv[None, :]
    cos, sin = jnp.cos(ang), jnp.sin(ang)
    return jnp.concatenate([cos, cos], axis=1), jnp.concatenate([-sin, sin], axis=1)


def _attn_prompt_kernel(q_ref, k_ref, v_ref, o_ref, lse_ref, kp_ref, vp_ref, *, has_prev):
    n = pl.program_id(2)
    q = q_ref[...]
    k = k_ref[...].astype(BF16)
    v = v_ref[...].astype(BF16)
    row = lax.broadcasted_iota(jnp.int32, (QBLOCK, QBLOCK), 0)
    col = lax.broadcasted_iota(jnp.int32, (QBLOCK, QBLOCK), 1)
    cur_mask = col <= row
    prev_mask = col >= row
    nt_dims = (((1,), (1,)), ((), ()))

    if has_prev:
        @pl.when(n == 0)
        def _():
            kp_ref[...] = jnp.zeros_like(kp_ref)
            vp_ref[...] = jnp.zeros_like(vp_ref)
        prev_ok = n > 0

    for h in range(HEADS_PER_GROUP):
        sl = slice(h * HEAD_DIM, (h + 1) * HEAD_DIM)
        qh = q[:, sl]
        s_c = lax.dot_general(qh, k[:, sl], nt_dims, preferred_element_type=F32) * ATT_SCALE
        s_c = jnp.where(cur_mask, s_c, -jnp.inf)
        mx = jnp.max(s_c, axis=1, keepdims=True)
        if has_prev:
            s_p = lax.dot_general(qh, kp_ref[:, sl], nt_dims, preferred_element_type=F32) * ATT_SCALE
            s_p = jnp.where(prev_mask & prev_ok, s_p, -jnp.inf)
            mx = jnp.maximum(mx, jnp.max(s_p, axis=1, keepdims=True))
        p_c = jnp.exp(s_c - mx)
        den = jnp.sum(p_c, axis=1, keepdims=True)
        acc = jnp.dot(p_c.astype(BF16), v[:, sl], preferred_element_type=F32)
        if has_prev:
            p_p = jnp.exp(s_p - mx)
            den = den + jnp.sum(p_p, axis=1, keepdims=True)
            acc = acc + jnp.dot(p_p.astype(BF16), vp_ref[:, sl], preferred_element_type=F32)
        o_ref[:, sl] = acc / den
        lse_ref[:, sl] = jnp.broadcast_to(mx + jnp.log(den), (QBLOCK, HEAD_DIM))

    if has_prev:
        kp_ref[...] = k
        vp_ref[...] = v


def _attn_prompt(q, kv, g, *, b, l):
    dil = ATT_GROUPS[g][1]
    ms = l // dil
    nblk = ms // QBLOCK
    qv = q.reshape(b, ms, dil * ATT_WIDTH)
    kvv = kv.reshape(b, 2, ms, dil * GROUP_WIDTH)
    q_tiles = ATT_WIDTH // GROUP_WIDTH
    out = jax.ShapeDtypeStruct((b, ms, dil * GROUP_WIDTH), F32)
    blk = pl.BlockSpec((None, QBLOCK, GROUP_WIDTH), lambda bi, r, n: (bi, n, r))
    o, lse = pl.pallas_call(
        functools.partial(_attn_prompt_kernel, has_prev=nblk > 1),
        grid=(b, dil, nblk),
        in_specs=[
            pl.BlockSpec((None, QBLOCK, GROUP_WIDTH), lambda bi, r, n: (bi, n, r * q_tiles + g)),
            pl.BlockSpec((None, None, QBLOCK, GROUP_WIDTH), lambda bi, r, n: (bi, 0, n, r)),
            pl.BlockSpec((None, None, QBLOCK, GROUP_WIDTH), lambda bi, r, n: (bi, 1, n, r)),
        ],
        out_specs=[blk, blk],
        out_shape=[out, out],
        scratch_shapes=[pltpu.VMEM((QBLOCK, GROUP_WIDTH), BF16)] * 2,
        compiler_params=_params(("parallel", "parallel", "arbitrary")),
        name=f"attn_prompt_g{g}",
    )(qv, kvv, kvv)
    return o.reshape(b * l, GROUP_WIDTH), lse.reshape(b * l, GROUP_WIDTH)


def _attn_decode_kernel(q_ref, *refs):
    o_ref = refs[-1]
    q = q_ref[...]
    outs, lses = [], []
    for g in range(N_GROUPS):
        kc_ref, vc_ref, kn_ref, vn_ref = refs[4 * g:4 * g + 4]
        qg = q[:, g * GROUP_WIDTH:(g + 1) * GROUP_WIDTH]
        kq = kc_ref[...] * qg
        vc = vc_ref[...]
        knq = kn_ref[...] * qg
        vn = vn_ref[...]
        og, lg = [], []
        for h in range(HEADS_PER_GROUP):
            sl = slice(h * HEAD_DIM, (h + 1) * HEAD_DIM)
            s = jnp.sum(kq[:, sl], axis=1, keepdims=True) * ATT_SCALE
            s_new = jnp.sum(knq[:, sl], axis=1, keepdims=True) * ATT_SCALE
            mx = jnp.maximum(jnp.max(s, axis=0, keepdims=True), s_new)
            p = jnp.exp(s - mx)
            p_new = jnp.exp(s_new - mx)
            den = jnp.sum(p, axis=0, keepdims=True) + p_new
            acc = jnp.sum(p * vc[:, sl], axis=0, keepdims=True) + p_new * vn[:, sl]
            og.append(acc / den)
            lg.append(mx + jnp.log(den))
        outs.append(og)
        lses.append(lg)
    for h in range(HEADS_PER_GROUP):
        mx = functools.reduce(jnp.maximum, [lses[g][h] for g in range(N_GROUPS)])
        ws = [jnp.exp(lses[g][h] - mx) for g in range(N_GROUPS)]
        num = functools.reduce(lambda a, c: a + c, [ws[g] * outs[g][h] for g in range(N_GROUPS)])
        o_ref[:, h * HEAD_DIM:(h + 1) * HEAD_DIM] = num / functools.reduce(lambda a, c: a + c, ws)


def _attn_decode(q, caches, kvnews):
    b = q.shape[0]
    in_specs = [pl.BlockSpec((None, 1, ATT_WIDTH), lambda i: (i, 0, 0))]
    args = [q.reshape(b, 1, ATT_WIDTH)]
    for g, (window, dil) in enumerate(ATT_GROUPS):
        cache = caches[g]
        assert cache.shape[2] == window and window == N_BACK * dil
        cv = cache.reshape(b, 2, N_BACK, dil * GROUP_WIDTH)
        nv = kvnews[g].reshape(2, b, 1, GROUP_WIDTH)
        in_specs += [
            pl.BlockSpec((None, None, N_BACK, GROUP_WIDTH), lambda i: (i, 0, 0, 0)),
            pl.BlockSpec((None, None, N_BACK, GROUP_WIDTH), lambda i: (i, 1, 0, 0)),
            pl.BlockSpec((None, None, 1, GROUP_WIDTH), lambda i: (0, i, 0, 0)),
            pl.BlockSpec((None, None, 1, GROUP_WIDTH), lambda i: (1, i, 0, 0)),
        ]
        args += [cv, cv, nv, nv]
    out = pl.pallas_call(
        _attn_decode_kernel,
        grid=(b,),
        in_specs=in_specs,
        out_specs=pl.BlockSpec((None, 1, GROUP_WIDTH), lambda i: (i, 0, 0)),
        out_shape=jax.ShapeDtypeStruct((b, 1, GROUP_WIDTH), F32),
        compiler_params=_params(("parallel",)),
        name="attn_decode",
    )(*args)
    return out.reshape(b, GROUP_WIDTH)


def _cache_shift_kernel(c_ref, new_ref, o_ref):
    w = c_ref.shape[0]
    o_ref[pl.ds(0, w - 1), :] = c_ref[pl.ds(1, w - 1), :]
    o_ref[pl.ds(w - 1, 1), :] = new_ref[...]


def _cache_shift(cache, kvnew):
    b, _, w, _ = cache.shape
    nv = kvnew.reshape(2, b, 1, GROUP_WIDTH)
    return pl.pallas_call(
        _cache_shift_kernel,
        grid=(b, 2),
        in_specs=[
            pl.BlockSpec((None, None, w, GROUP_WIDTH), lambda i, s: (i, s, 0, 0)),
            pl.BlockSpec((None, None, 1, GROUP_WIDTH), lambda i, s: (s, i, 0, 0)),
        ],
        out_specs=pl.BlockSpec((None, None, w, GROUP_WIDTH), lambda i, s: (i, s, 0, 0)),
        out_shape=jax.ShapeDtypeStruct(cache.shape, F32),
        compiler_params=_params(("parallel", "parallel")),
        name=f"cache_shift_w{w}",
    )(cache, nv)


def _ssm_prep_kernel(lr_ref, li_ref, ldt_ref, br_ref, bi_ref, cim_ref,
                     ar_ref, ai_ref, bbr_ref, bbi_ref, ncim_ref):
    lr, li = lr_ref[...], li_ref[...]
    dt = jnp.exp(ldt_ref[...])
    mag = jnp.exp(lr * dt)
    ar = mag * jnp.cos(li * dt)
    ai = mag * jnp.sin(li * dt)
    den = lr * lr + li * li
    fr = ((ar - 1.0) * lr + ai * li) / den
    fi = (ai * lr - (ar - 1.0) * li) / den
    br, bi = br_ref[...], bi_ref[...]
    ar_ref[...] = ar
    ai_ref[...] = ai
    bbr_ref[...] = fr * br - fi * bi
    bbi_ref[...] = fr * bi + fi * br
    ncim_ref[...] = -cim_ref[...]


def _ssm_prepare(lam_re, lam_im, log_dt, b_re, b_im, c_re, c_im):
    gn, pn = lam_re.shape
    cn = SSM_GROUP
    rows = gn * cn
    rep = lambda a: jnp.broadcast_to(a[:, None, :], (gn, cn, pn)).reshape(rows, pn)
    tr = lambda a: jnp.transpose(a, (0, 2, 1)).reshape(rows, pn)
    ldt = jnp.broadcast_to(log_dt[:, None, None], (gn, cn, pn)).reshape(rows, pn)
    shp = jax.ShapeDtypeStruct((rows, pn), F32)
    ar, ai, bbr, bbi, ncim = pl.pallas_call(
        _ssm_prep_kernel, out_shape=[shp] * 5, name="ssm_prep",
    )(rep(lam_re), rep(lam_im), ldt, tr(b_re), tr(b_im), c_im.reshape(rows, pn))
    nq, gb = gn // SSM_GROUPS_PER_BLOCK, SSM_GROUPS_PER_BLOCK
    diag = (jnp.arange(gb)[:, None, None, None] == jnp.arange(gb)[None, None, :, None])
    spread = lambda a: jnp.where(diag, a.reshape(2, nq, gb, cn, 1, pn), 0.0)
    bb = spread(jnp.stack([bbr, bbi]))
    bblk = jnp.transpose(bb, (1, 2, 3, 0, 4, 5)).reshape(nq, gb * cn, 2 * gb * pn).astype(BF16)
    cc = spread(jnp.stack([c_re.reshape(rows, pn), ncim]))
    cblk = jnp.transpose(cc, (1, 0, 2, 5, 4, 3)).reshape(nq, 2 * gb * pn, gb * cn).astype(BF16)
    a_re = ar.reshape(gn, cn, pn)[:, 0, :].reshape(gn * pn // LANES, LANES)
    a_im = ai.reshape(gn, cn, pn)[:, 0, :].reshape(gn * pn // LANES, LANES)
    return a_re, a_im, bblk, cblk


def _ssm_kernel(u_ref, x0_ref, bblk_ref, cblk_ref, are_ref, aim_ref, d_ref,
                y_ref, sfin_ref, bu_ref, xs_ref, st_ref, *, nb, steps):
    c = pl.program_id(0)
    nq = bblk_ref.shape[0]
    tiles_per_q = 2 * SSM_GROUPS_PER_BLOCK * SSM_STATE // LANES
    half = tiles_per_q // 2

    @pl.when(c == 0)
    def _():
        for j in range(nq * tiles_per_q):
            st_ref[j] = x0_ref[:, j * LANES:(j + 1) * LANES]

    u = u_ref[...]
    ub = u.astype(BF16)
    for q in range(nq):
        res = jnp.dot(ub[:, q * LANES:(q + 1) * LANES], bblk_ref[q], preferred_element_type=F32)
        for j in range(tiles_per_q):
            bu_ref[q * tiles_per_q + j] = res[:, j * LANES:(j + 1) * LANES]

    def scan_block(q, carry):
        for lt in range(half):
            jr = q * tiles_per_q + lt
            ji = jr + half
            ar = are_ref[q * half + lt]
            ai = aim_ref[q * half + lt]
            xr, xi = st_ref[jr], st_ref[ji]
            for t in range(steps):
                rows = pl.ds(t * nb, nb)
                xr, xi = (ar * xr - ai * xi + bu_ref[jr, rows, :],
                          ar * xi + ai * xr + bu_ref[ji, rows, :])
                xs_ref[jr, rows, :] = xr.astype(BF16)
                xs_ref[ji, rows, :] = xi.astype(BF16)
            st_ref[jr] = xr
            st_ref[ji] = xi
        return carry

    lax.fori_loop(0, nq, scan_block, 0)

    for q in range(nq):
        xq = jnp.concatenate([xs_ref[q * tiles_per_q + j] for j in range(tiles_per_q)], axis=1)
        yq = jnp.dot(xq, cblk_ref[q], preferred_element_type=F32)
        sl = slice(q * LANES, (q + 1) * LANES)
        y_ref[:, sl] = yq + d_ref[:, sl] * u[:, sl]

    @pl.when(c == pl.num_programs(0) - 1)
    def _():
        for j in range(nq * tiles_per_q):
            sfin_ref[:, j * LANES:(j + 1) * LANES] = st_ref[j]


def _ssm(u_t, x0, a_re, a_im, bblk, cblk, d_skip, *, nb, steps):
    rows_total, width = u_t.shape
    rows = steps * nb
    ntile = a_re.shape[0]
    nstate = 2 * ntile * LANES
    are = jnp.broadcast_to(a_re[:, None, :], (ntile, nb, LANES))
    aim = jnp.broadcast_to(a_im[:, None, :], (ntile, nb, LANES))
    const = lambda shape: pl.BlockSpec(shape, lambda c: (0,) * len(shape))
    return pl.pallas_call(
        functools.partial(_ssm_kernel, nb=nb, steps=steps),
        grid=(rows_total // rows,),
        in_specs=[
            pl.BlockSpec((rows, width), lambda c: (c, 0)),
            const((nb, nstate)),
            const(bblk.shape),
            const(cblk.shape),
            const(are.shape),
            const(aim.shape),
            const((1, width)),
        ],
        out_specs=[pl.BlockSpec((rows, width), lambda c: (c, 0)), const((nb, nstate))],
        out_shape=[jax.ShapeDtypeStruct((rows_total, width), F32), jax.ShapeDtypeStruct((nb, nstate), F32)],
        scratch_shapes=[
            pltpu.VMEM((2 * ntile, rows, LANES), F32),
            pltpu.VMEM((2 * ntile, rows, LANES), BF16),
            pltpu.VMEM((2 * ntile, nb, LANES), F32),
        ],
        compiler_params=_params(("arbitrary",)),
        name=f"ssm_nb{nb}",
    )(u_t, x0, bblk, cblk, are, aim, d_skip.reshape(1, width))


def _state_to_lanes(s_re, s_im):
    b, gn, pn = s_re.shape
    nq = gn // SSM_GROUPS_PER_BLOCK
    st = jnp.stack([s_re.reshape(b, nq, -1), s_im.reshape(b, nq, -1)], axis=2)
    return st.reshape(b, 2 * gn * pn)


def _lanes_to_state(s, gn, pn):
    b = s.shape[0]
    nq = gn // SSM_GROUPS_PER_BLOCK
    st = s.reshape(b, nq, 2, SSM_GROUPS_PER_BLOCK * pn)
    return st[:, :, 0].reshape(b, gn, pn), st[:, :, 1].reshape(b, gn, pn)


def _mix_kernel(*refs, merge_groups):
    if merge_groups:
        (x_ref, o0, o1, o2, l0, l1, l2, y_ref, ga_ref, gs_ref,
         gw_ref, gb_ref, wa_ref, ws_ref, wo_ref, out_ref) = refs
        la, lb, lc = l0[...], l1[...], l2[...]
        mx = jnp.maximum(jnp.maximum(la, lb), lc)
        wa_, wb_, wc_ = jnp.exp(la - mx), jnp.exp(lb - mx), jnp.exp(lc - mx)
        att = (wa_ * o0[...] + wb_ * o1[...] + wc_ * o2[...]) / (wa_ + wb_ + wc_)
    else:
        (x_ref, att_ref, y_ref, ga_ref, gs_ref,
         gw_ref, gb_ref, wa_ref, ws_ref, wo_ref, out_ref) = refs
        att = att_ref[...]
    a_proj = jnp.dot(att.astype(BF16), wa_ref[...], preferred_element_type=F32)
    zs = jax.nn.gelu(y_ref[...])
    glu = jnp.dot(zs.astype(BF16), gw_ref[...], preferred_element_type=F32) + gb_ref[...]
    s_out = zs * jax.nn.sigmoid(glu)
    s_proj = jnp.dot(s_out.astype(BF16), ws_ref[...], preferred_element_type=F32)
    merged = jax.nn.sigmoid(ga_ref[...]) * a_proj + jax.nn.sigmoid(gs_ref[...]) * s_proj
    out_ref[...] = x_ref[...] + jnp.dot(merged.astype(BF16), wo_ref[...], preferred_element_type=F32)


def _mix(x, att_parts, y, y_map, gates, glu_w, glu_b, wa, ws, wo, *, tm):
    m, d = x.shape
    sw = glu_w.shape[0]
    row = lambda width: pl.BlockSpec((tm, width), lambda i: (i, 0))
    const = lambda a: pl.BlockSpec(a.shape, lambda i: (0, 0), pipeline_mode=pl.Buffered(1))
    glu_b = glu_b.reshape(1, sw)
    in_specs = [row(d)] + [row(GROUP_WIDTH)] * len(att_parts)
    in_specs += [pl.BlockSpec((tm, sw), y_map),
                 pl.BlockSpec((tm, d), lambda i: (i, 0)), pl.BlockSpec((tm, d), lambda i: (i, 1)),
                 const(glu_w), const(glu_b), const(wa), const(ws), const(wo)]
    return pl.pallas_call(
        functools.partial(_mix_kernel, merge_groups=len(att_parts) > 1),
        grid=(m // tm,),
        in_specs=in_specs,
        out_specs=row(d),
        out_shape=jax.ShapeDtypeStruct((m, d), F32),
        compiler_params=_params(("parallel",)),
        name="mix",
    )(x, *att_parts, y, gates, gates, glu_w, glu_b, wa, ws, wo)


FFN_TF = 512
PROMPT_TM_FFN = 512
PROMPT_TM_INPROJ = 1024
PROMPT_TM_MIX = 256
SSM_STEPS = 16


def _layer_weights(p):
    w = dict(p)
    for name in ("ffn1", "ffn2"):
        w[name] = _ffn_weights(p[name + "_w_gate"], p[name + "_w_up"], p[name + "_w_down"], FFN_TF)
    w["w_in_b"] = p["w_in"].astype(BF16)
    w["qk_gain"] = jnp.concatenate([p["q_norm"].reshape(1, -1), p["k_norm"].reshape(1, -1)], axis=1)
    w["ssm"] = _ssm_prepare(p["ssm_lambda_re"], p["ssm_lambda_im"], p["ssm_log_dt"],
                            p["ssm_b_re"], p["ssm_b_im"], p["ssm_c_re"], p["ssm_c_im"])
    for name in ("glu_w", "w_attn_branch", "w_ssm_branch", "w_out"):
        w[name + "_b"] = p[name].astype(BF16)
    return w


def _prompt_layer(x, w):
    b, l, d = x.shape
    m = b * l
    gn, pn = w["ssm_lambda_re"].shape
    sw = w["glu_w"].shape[0]
    x1 = _ffn(x.reshape(m, d), w["ffn1_norm"], *w["ffn1"], tm=PROMPT_TM_FFN, tf=FFN_TF)
    cos, sin = _rope_tables(jnp.arange(l, dtype=jnp.int32))
    tm_in = min(PROMPT_TM_INPROJ, l)
    q, kv0, kv1, kv2, u_t, gates = _inproj_prompt(
        x1, w["mix_norm"], w["w_in_b"], w["qk_gain"], cos, sin, b=b, l=l, tm=tm_in)
    kvs = (kv0, kv1, kv2)
    att_parts = [_attn_prompt(q, kvs[g], g, b=b, l=l) for g in range(N_GROUPS)]
    a_re, a_im, bblk, cblk = w["ssm"]
    y_t, sfin = _ssm(u_t.reshape(l * b, sw), jnp.zeros((b, 2 * gn * pn), F32),
                     a_re, a_im, bblk, cblk, w["ssm_d"], nb=b, steps=SSM_STEPS)
    tm = PROMPT_TM_MIX
    nt = l // tm
    x2 = _mix(x1, [a[0] for a in att_parts] + [a[1] for a in att_parts],
              y_t.reshape(l, b * sw), lambda i: (i % nt, i // nt), gates,
              w["glu_w_b"], w["glu_b"], w["w_attn_branch_b"], w["w_ssm_branch_b"], w["w_out_b"], tm=tm)
    y = _ffn(x2, w["ffn2_norm"], *w["ffn2"], tm=PROMPT_TM_FFN, tf=FFN_TF)
    new_kv = []
    for g, (window, _) in enumerate(ATT_GROUPS):
        keep = min(window, l)
        new_kv.append(kvs[g][:, :, l - keep:].reshape(b, 2, keep, HEADS_PER_GROUP, HEAD_DIM))
    s_re, s_im = _lanes_to_state(sfin, gn, pn)
    return y.reshape(b, l, d), new_kv, (s_re, s_im)


def _sample_layer(x, pos0, caches, state, w):
    b, l, d = x.shape
    assert l == 1
    gn, pn = w["ssm_lambda_re"].shape
    x1 = _ffn(x.reshape(b, d), w["ffn1_norm"], *w["ffn1"], tm=b, tf=FFN_TF)
    cos, sin = _rope_tables(jnp.full((b,), pos0, dtype=jnp.int32))
    q, kn0, kn1, kn2, u, gates = _inproj_sample(x1, w["mix_norm"], w["w_in_b"], w["qk_gain"], cos, sin)
    kvnews = (kn0, kn1, kn2)
    caches = [c.reshape(b, 2, c.shape[2], GROUP_WIDTH) for c in caches]
    att = _attn_decode(q, caches, kvnews)
    new_kv = [_cache_shift(caches[g], kvnews[g]).reshape(b, 2, -1, HEADS_PER_GROUP, HEAD_DIM)
              for g in range(N_GROUPS)]
    a_re, a_im, bblk, cblk = w["ssm"]
    y, sfin = _ssm(u, _state_to_lanes(*state), a_re, a_im, bblk, cblk, w["ssm_d"], nb=b, steps=1)
    x2 = _mix(x1, [att], y, lambda i: (i, 0), gates,
              w["glu_w_b"], w["glu_b"], w["w_attn_branch_b"], w["w_ssm_branch_b"], w["w_out_b"], tm=b)
    out = _ffn(x2, w["ffn2_norm"], *w["ffn2"], tm=b, tf=FFN_TF)
    s_re, s_im = _lanes_to_state(sfin, gn, pn)
    return out.reshape(b, l, d), new_kv, (s_re, s_im)


def kernel(x_prompt, x_sample, cache_kv_w128, cache_kv_w512, cache_kv_w2048, state_ssm_re, state_ssm_im,
           ffn1_norm, ffn1_w_gate, ffn1_w_up, ffn1_w_down, mix_norm, w_in, q_norm, k_norm,
           ssm_lambda_re, ssm_lambda_im, ssm_log_dt, ssm_b_re, ssm_b_im, ssm_c_re, ssm_c_im, ssm_d,
           glu_w, glu_b, w_attn_branch, w_ssm_branch, w_out,
           ffn2_norm, ffn2_w_gate, ffn2_w_up, ffn2_w_down):
    depth = w_in.shape[0]
    params = dict(ffn1_norm=ffn1_norm, ffn1_w_gate=ffn1_w_gate, ffn1_w_up=ffn1_w_up, ffn1_w_down=ffn1_w_down,
                  mix_norm=mix_norm, w_in=w_in, q_norm=q_norm, k_norm=k_norm,
                  ssm_lambda_re=ssm_lambda_re, ssm_lambda_im=ssm_lambda_im, ssm_log_dt=ssm_log_dt,
                  ssm_b_re=ssm_b_re, ssm_b_im=ssm_b_im, ssm_c_re=ssm_c_re, ssm_c_im=ssm_c_im, ssm_d=ssm_d,
                  glu_w=glu_w, glu_b=glu_b, w_attn_branch=w_attn_branch, w_ssm_branch=w_ssm_branch,
                  w_out=w_out, ffn2_norm=ffn2_norm, ffn2_w_gate=ffn2_w_gate, ffn2_w_up=ffn2_w_up,
                  ffn2_w_down=ffn2_w_down)
    yp, ys = x_prompt, x_sample
    new_p = [[] for _ in range(5)]
    new_s = [[] for _ in range(5)]
    for layer in range(depth):
        w = _layer_weights({k: v[layer] for k, v in params.items()})
        yp, kv_p, st_p = _prompt_layer(yp, w)
        ys, kv_s, st_s = _sample_layer(
            ys, PAST_LEN, (cache_kv_w128[layer], cache_kv_w512[layer], cache_kv_w2048[layer]),
            (state_ssm_re[layer], state_ssm_im[layer]), w)
        for i, a in enumerate(list(kv_p) + list(st_p)):
            new_p[i].append(a)
        for i, a in enumerate(list(kv_s) + list(st_s)):
            new_s[i].append(a)
    outs_p = [jnp.stack(a) for a in new_p]
    outs_s = [jnp.stack(a) for a in new_s]
    return (yp, ys, *outs_p, *outs_s)
```

```python
import functools

import jax
import jax.numpy as jnp
from jax import lax
from jax.experimental import pallas as pl
from jax.experimental.pallas import tpu as pltpu

F32 = jnp.float32
BF16 = jnp.bfloat16

HEAD_DIM = 128
HEADS_PER_GROUP = 4
GROUP_WIDTH = HEADS_PER_GROUP * HEAD_DIM
ATT_GROUPS = ((128, 1), (512, 4), (2048, 16))
N_GROUPS = len(ATT_GROUPS)
ATT_WIDTH = N_GROUPS * GROUP_WIDTH
N_BACK = 128
QBLOCK = 128
ATT_SCALE = HEAD_DIM ** -0.5
ROPE_THETA = 10000.0
RMS_EPS = 1e-6
PAST_LEN = 16384
SSM_GROUP = 16
SSM_STATE = 64
SSM_GROUPS_PER_BLOCK = 8
LANES = 128
COL_TILE = 512
VMEM_LIMIT = 56 * 1024 * 1024


def _params(semantics):
    return pltpu.CompilerParams(dimension_semantics=semantics, vmem_limit_bytes=VMEM_LIMIT)


def _rms_rows(x, gain):
    ms = jnp.mean(x * x, axis=-1, keepdims=True)
    return x * lax.rsqrt(ms + RMS_EPS) * gain


def _head(h):
    return slice(h * HEAD_DIM, (h + 1) * HEAD_DIM)


def _ffn_kernel(x_ref, g_ref, wg_ref, wu_ref, wd_ref, o_ref, h_ref):
    @pl.when(pl.program_id(1) == 0)
    def _():
        x = x_ref[...]
        h_ref[...] = _rms_rows(x, g_ref[...]).astype(BF16)
        o_ref[...] = x

    h = h_ref[...]
    g = jnp.dot(h, wg_ref[...], preferred_element_type=F32)
    u = jnp.dot(h, wu_ref[...], preferred_element_type=F32)
    a = (0.5 * (g * jax.nn.sigmoid(g) * u)).astype(BF16)
    o_ref[...] += jnp.dot(a, wd_ref[...], preferred_element_type=F32)


def _ffn(x, gain, wg, wu, wd, *, tm, tf):
    m, d = x.shape
    fp = wg.shape[1]
    return pl.pallas_call(
        _ffn_kernel,
        grid=(m // tm, fp // tf),
        in_specs=[
            pl.BlockSpec((tm, d), lambda i, f: (i, 0)),
            pl.BlockSpec((1, d), lambda i, f: (0, 0)),
            pl.BlockSpec((d, tf), lambda i, f: (0, f)),
            pl.BlockSpec((d, tf), lambda i, f: (0, f)),
            pl.BlockSpec((tf, d), lambda i, f: (f, 0)),
        ],
        out_specs=pl.BlockSpec((tm, d), lambda i, f: (i, 0)),
        out_shape=jax.ShapeDtypeStruct((m, d), F32),
        scratch_shapes=[pltpu.VMEM((tm, d), BF16)],
        compiler_params=_params(("parallel", "arbitrary")),
        name="ffn",
    )(x, gain.reshape(1, d), wg, wu, wd)


def _ffn_weights(wg, wu, wd, tf):
    f = wg.shape[1]
    pad = (-f) % tf
    wg = jnp.pad(wg.astype(BF16), ((0, 0), (0, pad)))
    wu = jnp.pad(wu.astype(BF16), ((0, 0), (0, pad)))
    wd = jnp.pad(wd.astype(BF16), ((0, pad), (0, 0)))
    return wg, wu, wd


_J_K, _J_V, _J_U, _J_GATE, _J_END = 3, 6, 9, 11, 19


def _inproj_kernel(x_ref, g_ref, w_ref, qkg_ref, cos_ref, sin_ref,
                   qkv_ref, kv0_ref, kv1_ref, kv2_ref, u_ref, gate_ref, h_ref, *, head_major):
    j = pl.program_id(1)

    @pl.when(j == 0)
    def _():
        h_ref[...] = _rms_rows(x_ref[...], g_ref[...]).astype(BF16)

    kv_refs = (kv0_ref, kv1_ref, kv2_ref)
    rows = h_ref.shape[0]

    def zdot():
        return jnp.dot(h_ref[...], w_ref[...], preferred_element_type=F32)

    def norm_rope(z):
        cos, sin = cos_ref[...], sin_ref[...]
        gain = qkg_ref[...]
        heads = []
        for h in range(HEADS_PER_GROUP):
            y = _rms_rows(z[:, _head(h)], gain[:, _head(h)])
            heads.append(y * cos + pltpu.roll(y, HEAD_DIM // 2, axis=1) * sin)
        return heads

    def split(z):
        return [z[:, _head(h)] for h in range(HEADS_PER_GROUP)]

    def put_heads(heads):
        if head_major:
            for h in range(HEADS_PER_GROUP):
                qkv_ref[h] = heads[h]
        else:
            qkv_ref[...] = jnp.concatenate(heads, axis=1)

    def put_cache(ref, heads):
        if head_major:
            for h in range(HEADS_PER_GROUP):
                ref[pl.ds(h, rows, stride=HEADS_PER_GROUP), :] = heads[h]
        else:
            ref[...] = jnp.concatenate(heads, axis=1)

    @pl.when(j < _J_K)
    def _():
        put_heads(norm_rope(zdot()))

    for g in range(N_GROUPS):
        @pl.when(j == _J_K + g)
        def _(g=g):
            heads = norm_rope(zdot())
            put_heads(heads)
            put_cache(kv_refs[g], heads)

        @pl.when(j == _J_V + g)
        def _(g=g):
            heads = split(zdot())
            put_heads(heads)
            put_cache(kv_refs[g], heads)

    @pl.when((j >= _J_U) & (j < _J_GATE))
    def _():
        u_ref[...] = zdot()

    @pl.when(j >= _J_GATE)
    def _():
        gate_ref[...] = zdot()


def _kv_slot(j, g):
    return jnp.where(j > _J_K + g, 1, 0)


def _inproj_common_specs(d, tm, nt):
    return [
        pl.BlockSpec((tm, d), lambda i, j: (i, 0)),
        pl.BlockSpec((1, d), lambda i, j: (0, 0)),
        pl.BlockSpec((d, COL_TILE), lambda i, j: (0, j)),
        pl.BlockSpec((1, COL_TILE), lambda i, j: (0, jnp.minimum(j, _J_V - 1))),
        pl.BlockSpec((tm, HEAD_DIM), lambda i, j: (i % nt, 0)),
        pl.BlockSpec((tm, HEAD_DIM), lambda i, j: (i % nt, 0)),
    ]


def _inproj_prompt(x, gain, w, qk_gain, cos, sin, *, b, l, tm):
    m, d = x.shape
    nt = l // tm
    ssm_w = (_J_GATE - _J_U) * COL_TILE
    gate_w = (_J_END - _J_GATE) * COL_TILE
    u_tiles = _J_GATE - _J_U
    nh = HEADS_PER_GROUP
    out_specs = [pl.BlockSpec((None, nh, tm, HEAD_DIM),
                              lambda i, j: (i // nt, jnp.minimum(j, _J_U - 1), i % nt, 0))]
    for g in range(N_GROUPS):
        out_specs.append(pl.BlockSpec((None, None, tm * nh, HEAD_DIM),
                                      lambda i, j, g=g: (i // nt, _kv_slot(j, g), i % nt, 0)))
    out_specs.append(pl.BlockSpec(
        (tm, COL_TILE), lambda i, j: (i % nt, (i // nt) * u_tiles + jnp.clip(j - _J_U, 0, u_tiles - 1))))
    out_specs.append(pl.BlockSpec(
        (tm, COL_TILE), lambda i, j: (i, jnp.clip(j - _J_GATE, 0, _J_END - _J_GATE - 1))))
    out_shape = [jax.ShapeDtypeStruct((b, _J_U * nh, l, HEAD_DIM), F32)]
    out_shape += [jax.ShapeDtypeStruct((b, 2, l * nh, HEAD_DIM), F32)] * N_GROUPS
    out_shape += [jax.ShapeDtypeStruct((l, b * ssm_w), F32), jax.ShapeDtypeStruct((m, gate_w), F32)]
    return pl.pallas_call(
        functools.partial(_inproj_kernel, head_major=True),
        grid=(m // tm, _J_END),
        in_specs=_inproj_common_specs(d, tm, nt),
        out_specs=out_specs,
        out_shape=out_shape,
        scratch_shapes=[pltpu.VMEM((tm, d), BF16)],
        compiler_params=_params(("parallel", "arbitrary")),
        name="inproj_prompt",
    )(x, gain.reshape(1, d), w, qk_gain, cos, sin)


def _inproj_sample(x, gain, w, qk_gain, cos, sin):
    m, d = x.shape
    u_tiles = _J_GATE - _J_U
    out_specs = [pl.BlockSpec((m, COL_TILE), lambda i, j: (0, jnp.minimum(j, _J_U - 1)))]
    for g in range(N_GROUPS):
        out_specs.append(pl.BlockSpec((None, m, COL_TILE), lambda i, j, g=g: (_kv_slot(j, g), 0, 0)))
    out_specs.append(pl.BlockSpec((m, COL_TILE), lambda i, j: (0, jnp.clip(j - _J_U, 0, u_tiles - 1))))
    out_specs.append(pl.BlockSpec((m, COL_TILE), lambda i, j: (0, jnp.clip(j - _J_GATE, 0, _J_END - _J_GATE - 1))))
    out_shape = [jax.ShapeDtypeStruct((m, _J_U * COL_TILE), F32)]
    out_shape += [jax.ShapeDtypeStruct((2, m, GROUP_WIDTH), F32)] * N_GROUPS
    out_shape += [jax.ShapeDtypeStruct((m, u_tiles * COL_TILE), F32),
                  jax.ShapeDtypeStruct((m, (_J_END - _J_GATE) * COL_TILE), F32)]
    return pl.pallas_call(
        functools.partial(_inproj_kernel, head_major=False),
        grid=(1, _J_END),
        in_specs=_inproj_common_specs(d, m, 1),
        out_specs=out_specs,
        out_shape=out_shape,
        scratch_shapes=[pltpu.VMEM((m, d), BF16)],
        compiler_params=_params(("arbitrary", "arbitrary")),
        name="inproj_sample",
    )(x, gain.reshape(1, d), w, qk_gain, cos, sin)


def _rope_tables(pos):
    half = HEAD_DIM // 2
    inv = jnp.power(ROPE_THETA, -jnp.arange(half, dtype=F32) * (2.0 / HEAD_DIM))
    ang = pos.astype(F32)[:, None] * inv[None, :]
    cos, sin = jnp.cos(ang), jnp.sin(ang)
    return jnp.concatenate([cos, cos], axis=1), jnp.concatenate([-sin, sin], axis=1)


ATTN_UNROLL = 4


def _attn_prompt_kernel(*refs):
    qkv = refs[:3 * N_GROUPS]
    o_ref, m_ref, l_ref = refs[3 * N_GROUPS:]
    seq = o_ref.shape[0]
    row = lax.broadcasted_iota(jnp.int32, (QBLOCK, QBLOCK), 0)
    col = lax.broadcasted_iota(jnp.int32, (QBLOCK, QBLOCK), 1)
    cur_mask = col <= row
    prev_mask = col >= row
    nt_dims = (((1,), (1,)), ((), ()))
    full = (QBLOCK, HEAD_DIM)

    for g, (_, dil) in enumerate(ATT_GROUPS):
        q_ref, k_ref, v_ref = qkv[3 * g:3 * g + 3]
        nblk = seq // dil // QBLOCK
        span = QBLOCK * dil

        def rows_at(start, dil=dil):
            return pl.ds(start, QBLOCK) if dil == 1 else pl.ds(start, QBLOCK, stride=dil)

        def block(r, n, q_ref=q_ref, k_ref=k_ref, v_ref=v_ref, nblk=nblk, span=span, rows_at=rows_at):
            start = r + n * span
            rows = rows_at(start)
            q = q_ref[rows, :].astype(BF16)
            s_c = lax.dot_general(q, k_ref[rows, :].astype(BF16), nt_dims, preferred_element_type=F32)
            s_c = jnp.where(cur_mask, s_c * ATT_SCALE, -jnp.inf)
            mx = jnp.max(s_c, axis=1, keepdims=True)
            if nblk > 1:
                prows = rows_at(jnp.maximum(start - span, r))
                s_p = lax.dot_general(q, k_ref[prows, :].astype(BF16), nt_dims, preferred_element_type=F32)
                s_p = jnp.where(prev_mask, s_p * ATT_SCALE, -jnp.inf) + jnp.where(n > 0, 0.0, -jnp.inf)
                mx = jnp.maximum(mx, jnp.max(s_p, axis=1, keepdims=True))
            p_c = jnp.exp(s_c - mx)
            den = jnp.sum(p_c, axis=1, keepdims=True)
            acc = jnp.dot(p_c.astype(BF16), v_ref[rows, :].astype(BF16), preferred_element_type=F32)
            if nblk > 1:
                p_p = jnp.exp(s_p - mx)
                den = den + jnp.sum(p_p, axis=1, keepdims=True)
                acc = acc + jnp.dot(p_p.astype(BF16), v_ref[prows, :].astype(BF16), preferred_element_type=F32)
            return rows, acc, jnp.broadcast_to(mx, full), jnp.broadcast_to(den, full)

        def merge(rows, acc, mb, lb, first=(g == 0)):
            if first:
                o_ref[rows, :] = acc
                m_ref[rows, :] = mb
                l_ref[rows, :] = lb
            else:
                m_old = m_ref[rows, :]
                m_new = jnp.maximum(m_old, mb)
                a_old, a_new = jnp.exp(m_old - m_new), jnp.exp(mb - m_new)
                o_ref[rows, :] = a_old * o_ref[rows, :] + a_new * acc
                l_ref[rows, :] = a_old * l_ref[rows, :] + a_new * lb
                m_ref[rows, :] = m_new

        def body(i, carry, dil=dil, block=block, merge=merge):
            results = []
            for d in range(ATTN_UNROLL):
                idx = i * ATTN_UNROLL + d
                results.append(block(idx % dil, idx // dil))
            for res in results:
                merge(*res)
            return carry

        lax.fori_loop(0, dil * nblk // ATTN_UNROLL, body, 0)

    o_ref[...] = o_ref[...] / l_ref[...]


def _attn_prompt(qkv, *, b, l):
    nh = HEADS_PER_GROUP
    in_specs = []
    for g in range(N_GROUPS):
        for part in range(3):
            in_specs.append(pl.BlockSpec(
                (None, None, l, HEAD_DIM),
                lambda bi, hs, g=g, part=part: (bi, part * N_GROUPS * nh + g * nh + hs, 0, 0)))
    return pl.pallas_call(
        _attn_prompt_kernel,
        grid=(b, nh),
        in_specs=in_specs,
        out_specs=pl.BlockSpec((None, None, l, HEAD_DIM), lambda bi, hs: (bi, hs, 0, 0)),
        out_shape=jax.ShapeDtypeStruct((b, nh, l, HEAD_DIM), F32),
        scratch_shapes=[pltpu.VMEM((l, HEAD_DIM), F32)] * 2,
        compiler_params=_params(("parallel", "parallel")),
        name="attn_prompt",
    )(*([qkv] * (3 * N_GROUPS)))


def _attn_decode_kernel(q_ref, *refs):
    o_ref = refs[-1]
    outs, lses = [], []
    for g in range(N_GROUPS):
        kc_ref, vc_ref, kn_ref, vn_ref = refs[4 * g:4 * g + 4]
        qg = q_ref[g][None]
        s = jnp.sum(kc_ref[...] * qg, axis=-1, keepdims=True) * ATT_SCALE
        s_new = jnp.sum(kn_ref[...] * qg, axis=-1, keepdims=True) * ATT_SCALE
        mx = jnp.maximum(jnp.max(s, axis=0, keepdims=True), s_new)
        p = jnp.exp(s - mx)
        p_new = jnp.exp(s_new - mx)
        den = jnp.sum(p, axis=0, keepdims=True) + p_new
        acc = jnp.sum(p * vc_ref[...], axis=0, keepdims=True) + p_new * vn_ref[...]
        outs.append(acc / den)
        lses.append(mx + jnp.log(den))
    mx = functools.reduce(jnp.maximum, lses)
    ws = [jnp.exp(lse - mx) for lse in lses]
    num = functools.reduce(lambda a, c: a + c, [w * o for w, o in zip(ws, outs)])
    o_ref[...] = (num / functools.reduce(lambda a, c: a + c, ws))[0]


def _attn_decode(q, caches, kvnews):
    b = q.shape[0]
    nh = HEADS_PER_GROUP
    in_specs = [pl.BlockSpec((None, N_GROUPS, nh, HEAD_DIM), lambda i: (i, 0, 0, 0))]
    args = [q]
    for g, (window, dil) in enumerate(ATT_GROUPS):
        cache = caches[g]
        assert cache.shape[2] == window and window == N_BACK * dil
        cv = cache.reshape(b, 2, N_BACK, dil, nh, HEAD_DIM)
        in_specs += [
            pl.BlockSpec((None, None, N_BACK, None, nh, HEAD_DIM), lambda i: (i, 0, 0, 0, 0, 0)),
            pl.BlockSpec((None, None, N_BACK, None, nh, HEAD_DIM), lambda i: (i, 1, 0, 0, 0, 0)),
            pl.BlockSpec((None, None, 1, nh, HEAD_DIM), lambda i: (0, i, 0, 0, 0)),
            pl.BlockSpec((None, None, 1, nh, HEAD_DIM), lambda i: (1, i, 0, 0, 0)),
        ]
        args += [cv, cv, kvnews[g], kvnews[g]]
    return pl.pallas_call(
        _attn_decode_kernel,
        grid=(b,),
        in_specs=in_specs,
        out_specs=pl.BlockSpec((None, nh, HEAD_DIM), lambda i: (i, 0, 0)),
        out_shape=jax.ShapeDtypeStruct((b, nh, HEAD_DIM), F32),
        compiler_params=_params(("parallel",)),
        name="attn_decode",
    )(*args)


def _cache_shift_kernel(c_ref, new_ref, o_ref):
    w = c_ref.shape[0]
    o_ref[pl.ds(0, w - 1)] = c_ref[pl.ds(1, w - 1)]
    o_ref[pl.ds(w - 1, 1)] = new_ref[...]


def _cache_shift(cache, kvnew):
    b, _, w, nh, e = cache.shape
    return pl.pallas_call(
        _cache_shift_kernel,
        grid=(b, 2),
        in_specs=[
            pl.BlockSpec((None, None, w, nh, e), lambda i, s: (i, s, 0, 0, 0)),
            pl.BlockSpec((None, None, 1, nh, e), lambda i, s: (s, i, 0, 0, 0)),
        ],
        out_specs=pl.BlockSpec((None, None, w, nh, e), lambda i, s: (i, s, 0, 0, 0)),
        out_shape=jax.ShapeDtypeStruct(cache.shape, F32),
        compiler_params=_params(("parallel", "parallel")),
        name=f"cache_shift_w{w}",
    )(cache, kvnew)


def _ssm_prep_kernel(lr_ref, li_ref, ldt_ref, br_ref, bi_ref, cim_ref,
                     ar_ref, ai_ref, bbr_ref, bbi_ref, ncim_ref):
    lr, li = lr_ref[...], li_ref[...]
    dt = jnp.exp(ldt_ref[...])
    mag = jnp.exp(lr * dt)
    ar = mag * jnp.cos(li * dt)
    ai = mag * jnp.sin(li * dt)
    den = lr * lr + li * li
    fr = ((ar - 1.0) * lr + ai * li) / den
    fi = (ai * lr - (ar - 1.0) * li) / den
    br, bi = br_ref[...], bi_ref[...]
    ar_ref[...] = ar
    ai_ref[...] = ai
    bbr_ref[...] = fr * br - fi * bi
    bbi_ref[...] = fr * bi + fi * br
    ncim_ref[...] = -cim_ref[...]


def _ssm_prepare(lam_re, lam_im, log_dt, b_re, b_im, c_re, c_im):
    gn, pn = lam_re.shape
    cn = SSM_GROUP
    rows = gn * cn
    rep = lambda a: jnp.broadcast_to(a[:, None, :], (gn, cn, pn)).reshape(rows, pn)
    tr = lambda a: jnp.transpose(a, (0, 2, 1)).reshape(rows, pn)
    ldt = jnp.broadcast_to(log_dt[:, None, None], (gn, cn, pn)).reshape(rows, pn)
    shp = jax.ShapeDtypeStruct((rows, pn), F32)
    ar, ai, bbr, bbi, ncim = pl.pallas_call(
        _ssm_prep_kernel, out_shape=[shp] * 5, name="ssm_prep",
    )(rep(lam_re), rep(lam_im), ldt, tr(b_re), tr(b_im), c_im.reshape(rows, pn))
    nq, gb = gn // SSM_GROUPS_PER_BLOCK, SSM_GROUPS_PER_BLOCK
    diag = (jnp.arange(gb)[:, None, None, None] == jnp.arange(gb)[None, None, :, None])
    spread = lambda a: jnp.where(diag, a.reshape(2, nq, gb, cn, 1, pn), 0.0)
    bb = spread(jnp.stack([bbr, bbi]))
    bblk = jnp.transpose(bb, (1, 2, 3, 0, 4, 5)).reshape(nq, gb * cn, 2 * gb * pn).astype(BF16)
    cc = spread(jnp.stack([c_re.reshape(rows, pn), ncim]))
    cblk = jnp.transpose(cc, (1, 0, 2, 5, 4, 3)).reshape(nq, 2 * gb * pn, gb * cn).astype(BF16)
    a_re = ar.reshape(gn, cn, pn)[:, 0, :].reshape(gn * pn // LANES, LANES)
    a_im = ai.reshape(gn, cn, pn)[:, 0, :].reshape(gn * pn // LANES, LANES)
    return a_re, a_im, bblk, cblk


def _ssm_kernel(u_ref, x0_ref, bblk_ref, cblk_ref, are_ref, aim_ref, d_ref,
                y_ref, sfin_ref, bu_ref, xs_ref, st_ref, *, nb, steps):
    c = pl.program_id(0)
    nq = bblk_ref.shape[0]
    tiles_per_q = 2 * SSM_GROUPS_PER_BLOCK * SSM_STATE // LANES
    half = tiles_per_q // 2

    @pl.when(c == 0)
    def _():
        for j in range(nq * tiles_per_q):
            st_ref[j] = x0_ref[:, j * LANES:(j + 1) * LANES]

    u = u_ref[...]
    ub = u.astype(BF16)
    for q in range(nq):
        res = jnp.dot(ub[:, q * LANES:(q + 1) * LANES], bblk_ref[q], preferred_element_type=F32)
        for j in range(tiles_per_q):
            bu_ref[q * tiles_per_q + j] = res[:, j * LANES:(j + 1) * LANES]

    def scan_block(q, carry):
        for lt in range(half):
            jr = q * tiles_per_q + lt
            ji = jr + half
            ar = are_ref[q * half + lt]
            ai = aim_ref[q * half + lt]
            xr, xi = st_ref[jr], st_ref[ji]
            for t in range(steps):
                rows = pl.ds(t * nb, nb)
                xr, xi = (ar * xr - ai * xi + bu_ref[jr, rows, :],
                          ar * xi + ai * xr + bu_ref[ji, rows, :])
                xs_ref[jr, rows, :] = xr.astype(BF16)
                xs_ref[ji, rows, :] = xi.astype(BF16)
            st_ref[jr] = xr
            st_ref[ji] = xi
        return carry

    lax.fori_loop(0, nq, scan_block, 0)

    for q in range(nq):
        xq = jnp.concatenate([xs_ref[q * tiles_per_q + j] for j in range(tiles_per_q)], axis=1)
        yq = jnp.dot(xq, cblk_ref[q], preferred_element_type=F32)
        sl = slice(q * LANES, (q + 1) * LANES)
        y_ref[:, sl] = yq + d_ref[:, sl] * u[:, sl]

    @pl.when(c == pl.num_programs(0) - 1)
    def _():
        for j in range(nq * tiles_per_q):
            sfin_ref[:, j * LANES:(j + 1) * LANES] = st_ref[j]


def _ssm(u_t, x0, a_re, a_im, bblk, cblk, d_skip, *, nb, steps):
    rows_total, width = u_t.shape
    rows = steps * nb
    ntile = a_re.shape[0]
    nstate = 2 * ntile * LANES
    are = jnp.broadcast_to(a_re[:, None, :], (ntile, nb, LANES))
    aim = jnp.broadcast_to(a_im[:, None, :], (ntile, nb, LANES))
    const = lambda shape: pl.BlockSpec(shape, lambda c: (0,) * len(shape))
    return pl.pallas_call(
        functools.partial(_ssm_kernel, nb=nb, steps=steps),
        grid=(rows_total // rows,),
        in_specs=[
            pl.BlockSpec((rows, width), lambda c: (c, 0)),
            const((nb, nstate)),
            const(bblk.shape),
            const(cblk.shape),
            const(are.shape),
            const(aim.shape),
            const((1, width)),
        ],
        out_specs=[pl.BlockSpec((rows, width), lambda c: (c, 0)), const((nb, nstate))],
        out_shape=[jax.ShapeDtypeStruct((rows_total, width), F32), jax.ShapeDtypeStruct((nb, nstate), F32)],
        scratch_shapes=[
            pltpu.VMEM((2 * ntile, rows, LANES), F32),
            pltpu.VMEM((2 * ntile, rows, LANES), BF16),
            pltpu.VMEM((2 * ntile, nb, LANES), F32),
        ],
        compiler_params=_params(("arbitrary",)),
        name=f"ssm_nb{nb}",
    )(u_t, x0, bblk, cblk, are, aim, d_skip.reshape(1, width))


def _state_to_lanes(s_re, s_im):
    b, gn, pn = s_re.shape
    nq = gn // SSM_GROUPS_PER_BLOCK
    st = jnp.stack([s_re.reshape(b, nq, -1), s_im.reshape(b, nq, -1)], axis=2)
    return st.reshape(b, 2 * gn * pn)


def _lanes_to_state(s, gn, pn):
    b = s.shape[0]
    nq = gn // SSM_GROUPS_PER_BLOCK
    st = s.reshape(b, nq, 2, SSM_GROUPS_PER_BLOCK * pn)
    return st[:, :, 0].reshape(b, gn, pn), st[:, :, 1].reshape(b, gn, pn)


def _mix_kernel(x_ref, att_ref, y_ref, ga_ref, gs_ref, gw_ref, gb_ref, wa_ref, ws_ref, wo_ref, out_ref):
    att = jnp.concatenate([att_ref[h] for h in range(HEADS_PER_GROUP)], axis=1)
    a_proj = jnp.dot(att.astype(BF16), wa_ref[...], preferred_element_type=F32)
    zs = jax.nn.gelu(y_ref[...])
    glu = jnp.dot(zs.astype(BF16), gw_ref[...], preferred_element_type=F32) + gb_ref[...]
    s_out = zs * jax.nn.sigmoid(glu)
    s_proj = jnp.dot(s_out.astype(BF16), ws_ref[...], preferred_element_type=F32)
    merged = jax.nn.sigmoid(ga_ref[...]) * a_proj + jax.nn.sigmoid(gs_ref[...]) * s_proj
    out_ref[...] = x_ref[...] + jnp.dot(merged.astype(BF16), wo_ref[...], preferred_element_type=F32)


def _mix(x, att, y, y_map, gates, glu_w, glu_b, wa, ws, wo, *, tm, nt):
    m, d = x.shape
    sw = glu_w.shape[0]
    row = lambda width: pl.BlockSpec((tm, width), lambda i: (i, 0))
    const = lambda a: pl.BlockSpec(a.shape, lambda i: (0, 0), pipeline_mode=pl.Buffered(1))
    glu_b = glu_b.reshape(1, sw)
    in_specs = [row(d),
                pl.BlockSpec((None, HEADS_PER_GROUP, tm, HEAD_DIM), lambda i: (i // nt, 0, i % nt, 0)),
                pl.BlockSpec((tm, sw), y_map),
                pl.BlockSpec((tm, d), lambda i: (i, 0)), pl.BlockSpec((tm, d), lambda i: (i, 1)),
                const(glu_w), const(glu_b), const(wa), const(ws), const(wo)]
    return pl.pallas_call(
        _mix_kernel,
        grid=(m // tm,),
        in_specs=in_specs,
        out_specs=row(d),
        out_shape=jax.ShapeDtypeStruct((m, d), F32),
        compiler_params=_params(("parallel",)),
        name="mix",
    )(x, att, y, gates, gates, glu_w, glu_b, wa, ws, wo)


FFN_TF = 512
PROMPT_TM_FFN = 512
PROMPT_TM_INPROJ = 1024
PROMPT_TM_MIX = 256
SSM_STEPS = 16


def _layer_weights(p):
    w = dict(p)
    for name in ("ffn1", "ffn2"):
        w[name] = _ffn_weights(p[name + "_w_gate"], p[name + "_w_up"], p[name + "_w_down"], FFN_TF)
    w["w_in_b"] = p["w_in"].astype(BF16)
    w["qk_gain"] = jnp.concatenate([p["q_norm"].reshape(1, -1), p["k_norm"].reshape(1, -1)], axis=1)
    w["ssm"] = _ssm_prepare(p["ssm_lambda_re"], p["ssm_lambda_im"], p["ssm_log_dt"],
                            p["ssm_b_re"], p["ssm_b_im"], p["ssm_c_re"], p["ssm_c_im"])
    for name in ("glu_w", "w_attn_branch", "w_ssm_branch", "w_out"):
        w[name + "_b"] = p[name].astype(BF16)
    return w


def _prompt_layer(x, w):
    b, l, d = x.shape
    m = b * l
    nh = HEADS_PER_GROUP
    gn, pn = w["ssm_lambda_re"].shape
    sw = w["glu_w"].shape[0]
    x1 = _ffn(x.reshape(m, d), w["ffn1_norm"], *w["ffn1"], tm=PROMPT_TM_FFN, tf=FFN_TF)
    cos, sin = _rope_tables(jnp.arange(l, dtype=jnp.int32))
    tm_in = min(PROMPT_TM_INPROJ, l)
    qkv, kv0, kv1, kv2, u_t, gates = _inproj_prompt(
        x1, w["mix_norm"], w["w_in_b"], w["qk_gain"], cos, sin, b=b, l=l, tm=tm_in)
    att = _attn_prompt(qkv, b=b, l=l)
    a_re, a_im, bblk, cblk = w["ssm"]
    y_t, sfin = _ssm(u_t.reshape(l * b, sw), jnp.zeros((b, 2 * gn * pn), F32),
                     a_re, a_im, bblk, cblk, w["ssm_d"], nb=b, steps=SSM_STEPS)
    tm = PROMPT_TM_MIX
    nt = l // tm
    x2 = _mix(x1, att, y_t.reshape(l, b * sw), lambda i: (i % nt, i // nt), gates,
              w["glu_w_b"], w["glu_b"], w["w_attn_branch_b"], w["w_ssm_branch_b"], w["w_out_b"], tm=tm, nt=nt)
    y = _ffn(x2, w["ffn2_norm"], *w["ffn2"], tm=PROMPT_TM_FFN, tf=FFN_TF)
    new_kv = []
    for g, (window, _) in enumerate(ATT_GROUPS):
        keep = min(window, l)
        kv = (kv0, kv1, kv2)[g]
        new_kv.append(kv[:, :, (l - keep) * nh:].reshape(b, 2, keep, nh, HEAD_DIM))
    s_re, s_im = _lanes_to_state(sfin, gn, pn)
    return y.reshape(b, l, d), new_kv, (s_re, s_im)


def _sample_layer(x, pos0, caches, state, w):
    b, l, d = x.shape
    assert l == 1
    nh = HEADS_PER_GROUP
    gn, pn = w["ssm_lambda_re"].shape
    x1 = _ffn(x.reshape(b, d), w["ffn1_norm"], *w["ffn1"], tm=b, tf=FFN_TF)
    cos, sin = _rope_tables(jnp.full((b,), pos0, dtype=jnp.int32))
    qkv, kn0, kn1, kn2, u, gates = _inproj_sample(x1, w["mix_norm"], w["w_in_b"], w["qk_gain"], cos, sin)
    q = qkv[:, :ATT_WIDTH].reshape(b, N_GROUPS, nh, HEAD_DIM)
    kvnews = [kn.reshape(2, b, 1, nh, HEAD_DIM) for kn in (kn0, kn1, kn2)]
    att = _attn_decode(q, caches, kvnews)
    new_kv = [_cache_shift(caches[g], kvnews[g]) for g in range(N_GROUPS)]
    a_re, a_im, bblk, cblk = w["ssm"]
    y, sfin = _ssm(u, _state_to_lanes(*state), a_re, a_im, bblk, cblk, w["ssm_d"], nb=b, steps=1)
    att_h = jnp.transpose(att, (1, 0, 2))[None]
    x2 = _mix(x1, att_h, y, lambda i: (i, 0), gates,
              w["glu_w_b"], w["glu_b"], w["w_attn_branch_b"], w["w_ssm_branch_b"], w["w_out_b"], tm=b, nt=1)
    out = _ffn(x2, w["ffn2_norm"], *w["ffn2"], tm=b, tf=FFN_TF)
    s_re, s_im = _lanes_to_state(sfin, gn, pn)
    return out.reshape(b, l, d), new_kv, (s_re, s_im)


def kernel(x_prompt, x_sample, cache_kv_w128, cache_kv_w512, cache_kv_w2048, state_ssm_re, state_ssm_im,
           ffn1_norm, ffn1_w_gate, ffn1_w_up, ffn1_w_down, mix_norm, w_in, q_norm, k_norm,
           ssm_lambda_re, ssm_lambda_im, ssm_log_dt, ssm_b_re, ssm_b_im, ssm_c_re, ssm_c_im, ssm_d,
           glu_w, glu_b, w_attn_branch, w_ssm_branch, w_out,
           ffn2_norm, ffn2_w_gate, ffn2_w_up, ffn2_w_down):
    depth = w_in.shape[0]
    params = dict(ffn1_norm=ffn1_norm, ffn1_w_gate=ffn1_w_gate, ffn1_w_up=ffn1_w_up, ffn1_w_down=ffn1_w_down,
                  mix_norm=mix_norm, w_in=w_in, q_norm=q_norm, k_norm=k_norm,
                  ssm_lambda_re=ssm_lambda_re, ssm_lambda_im=ssm_lambda_im, ssm_log_dt=ssm_log_dt,
                  ssm_b_re=ssm_b_re, ssm_b_im=ssm_b_im, ssm_c_re=ssm_c_re, ssm_c_im=ssm_c_im, ssm_d=ssm_d,
                  glu_w=glu_w, glu_b=glu_b, w_attn_branch=w_attn_branch, w_ssm_branch=w_ssm_branch,
                  w_out=w_out, ffn2_norm=ffn2_norm, ffn2_w_gate=ffn2_w_gate, ffn2_w_up=ffn2_w_up,
                  ffn2_w_down=ffn2_w_down)
    yp, ys = x_prompt, x_sample
    new_p = [[] for _ in range(5)]
    new_s = [[] for _ in range(5)]
    for layer in range(depth):
        w = _layer_weights({k: v[layer] for k, v in params.items()})
        yp, kv_p, st_p = _prompt_layer(yp, w)
        ys, kv_s, st_s = _sample_layer(
            ys, PAST_LEN, (cache_kv_w128[layer], cache_kv_w512[layer], cache_kv_w2048[layer]),
            (state_ssm_re[layer], state_ssm_im[layer]), w)
        for i, a in enumerate(list(kv_p) + list(st_p)):
            new_p[i].append(a)
        for i, a in enumerate(list(kv_s) + list(st_s)):
            new_s[i].append(a)
    outs_p = [jnp.stack(a) for a in new_p]
    outs_s = [jnp.stack(a) for a in new_s]
    return (yp, ys, *outs_p, *outs_s)
```

```python
import functools

import jax
import jax.numpy as jnp
from jax import lax
from jax.experimental import pallas as pl
from jax.experimental.pallas import tpu as pltpu

F32 = jnp.float32
BF16 = jnp.bfloat16

HEAD_DIM = 128
HEADS_PER_GROUP = 4
GROUP_WIDTH = HEADS_PER_GROUP * HEAD_DIM
ATT_GROUPS = ((128, 1), (512, 4), (2048, 16))
N_GROUPS = len(ATT_GROUPS)
ATT_WIDTH = N_GROUPS * GROUP_WIDTH
N_BACK = 128
QBLOCK = 128
ATT_SCALE = HEAD_DIM ** -0.5
ROPE_THETA = 10000.0
RMS_EPS = 1e-6
PAST_LEN = 16384
SSM_GROUP = 16
SSM_STATE = 64
SSM_GROUPS_PER_BLOCK = 8
LANES = 128
COL_TILE = 512
VMEM_LIMIT = 56 * 1024 * 1024


def _params(semantics):
    return pltpu.CompilerParams(dimension_semantics=semantics, vmem_limit_bytes=VMEM_LIMIT)


def _rms_rows(x, gain):
    ms = jnp.mean(x * x, axis=-1, keepdims=True)
    return x * lax.rsqrt(ms + RMS_EPS) * gain


def _head(h):
    return slice(h * HEAD_DIM, (h + 1) * HEAD_DIM)


def _ffn_kernel(x_ref, g_ref, wg_ref, wu_ref, wd_ref, o_ref, h_ref):
    @pl.when(pl.program_id(1) == 0)
    def _():
        x = x_ref[...]
        h_ref[...] = _rms_rows(x, g_ref[...]).astype(BF16)
        o_ref[...] = x

    h = h_ref[...]
    g = jnp.dot(h, wg_ref[...], preferred_element_type=F32)
    u = jnp.dot(h, wu_ref[...], preferred_element_type=F32)
    a = (0.5 * (g * jax.nn.sigmoid(g) * u)).astype(BF16)
    o_ref[...] += jnp.dot(a, wd_ref[...], preferred_element_type=F32)


def _ffn(x, gain, wg, wu, wd, *, tm, tf):
    m, d = x.shape
    fp = wg.shape[1]
    return pl.pallas_call(
        _ffn_kernel,
        grid=(m // tm, fp // tf),
        in_specs=[
            pl.BlockSpec((tm, d), lambda i, f: (i, 0)),
            pl.BlockSpec((1, d), lambda i, f: (0, 0)),
            pl.BlockSpec((d, tf), lambda i, f: (0, f)),
            pl.BlockSpec((d, tf), lambda i, f: (0, f)),
            pl.BlockSpec((tf, d), lambda i, f: (f, 0)),
        ],
        out_specs=pl.BlockSpec((tm, d), lambda i, f: (i, 0)),
        out_shape=jax.ShapeDtypeStruct((m, d), F32),
        scratch_shapes=[pltpu.VMEM((tm, d), BF16)],
        compiler_params=_params(("parallel", "arbitrary")),
        name="ffn",
    )(x, gain.reshape(1, d), wg, wu, wd)


def _ffn_weights(wg, wu, wd, tf):
    f = wg.shape[1]
    pad = (-f) % tf
    wg = jnp.pad(wg.astype(BF16), ((0, 0), (0, pad)))
    wu = jnp.pad(wu.astype(BF16), ((0, 0), (0, pad)))
    wd = jnp.pad(wd.astype(BF16), ((0, pad), (0, 0)))
    return wg, wu, wd


_J_K, _J_V, _J_U, _J_GATE, _J_END = 3, 6, 9, 11, 19
EPILOGUE_ROWS = 256


def _inproj_kernel(x_ref, g_ref, w_ref, qkg_ref, cos_ref, sin_ref,
                   qk_ref, v_ref, kv0_ref, kv1_ref, kv2_ref, u_ref, gate_ref, h_ref, z_ref, *, head_major):
    j = pl.program_id(1)
    rows = h_ref.shape[0]
    nh = HEADS_PER_GROUP
    kv_refs = (kv0_ref, kv1_ref, kv2_ref)

    @pl.when(j == 0)
    def _():
        h_ref[...] = _rms_rows(x_ref[...], g_ref[...]).astype(BF16)

    def zdot():
        return jnp.dot(h_ref[...], w_ref[...], preferred_element_type=F32)

    def put_heads(ref, heads, r0, n):
        if head_major:
            for h in range(nh):
                ref[h, pl.ds(r0, n), :] = heads[h]
        else:
            ref[pl.ds(r0, n), :] = jnp.concatenate(heads, axis=1)

    def put_cache(ref, heads, r0, n):
        if head_major:
            for h in range(nh):
                ref[pl.ds(r0 * nh + h, n, stride=nh), :] = heads[h]
        else:
            ref[pl.ds(r0, n), :] = jnp.concatenate(heads, axis=1)

    def norm_rope_tile(tile):
        z_src = z_ref.at[tile % 2]
        gain = qkg_ref[...]
        n = min(EPILOGUE_ROWS, rows)
        for r0 in range(0, rows, n):
            cos, sin = cos_ref[pl.ds(r0, n), :], sin_ref[pl.ds(r0, n), :]
            heads = []
            for h in range(nh):
                y = _rms_rows(z_src[pl.ds(r0, n), _head(h)], gain[:, _head(h)])
                heads.append(y * cos + pltpu.roll(y, HEAD_DIM // 2, axis=1) * sin)
            put_heads(qk_ref, heads, r0, n)
            if tile >= _J_K:
                put_cache(kv_refs[tile - _J_K], heads, r0, n)

    def v_tile(g):
        z = zdot()
        heads = [z[:, _head(h)] for h in range(nh)]
        put_heads(v_ref, heads, 0, rows)
        put_cache(kv_refs[g], heads, 0, rows)

    for t in range(_J_V + 1):
        @pl.when(j == t)
        def _(t=t):
            if t < _J_V:
                z_ref[t % 2] = zdot()
            else:
                v_tile(0)
            if t > 0:
                norm_rope_tile(t - 1)

    for g in range(1, N_GROUPS):
        @pl.when(j == _J_V + g)
        def _(g=g):
            v_tile(g)

    @pl.when((j >= _J_U) & (j < _J_GATE))
    def _():
        u_ref[...] = zdot()

    @pl.when(j >= _J_GATE)
    def _():
        gate_ref[...] = zdot()


def _kv_slot(j, g):
    return jnp.where(j > _J_K + g + 1, 1, 0)


def _inproj_common_specs(d, tm, nt):
    return [
        pl.BlockSpec((tm, d), lambda i, j: (i, 0), pipeline_mode=pl.Buffered(1)),
        pl.BlockSpec((1, d), lambda i, j: (0, 0)),
        pl.BlockSpec((d, COL_TILE), lambda i, j: (0, j)),
        pl.BlockSpec((1, COL_TILE), lambda i, j: (0, jnp.clip(j - 1, 0, _J_V - 1))),
        pl.BlockSpec((tm, HEAD_DIM), lambda i, j: (i % nt, 0)),
        pl.BlockSpec((tm, HEAD_DIM), lambda i, j: (i % nt, 0)),
    ]


def _inproj_prompt(x, gain, w, qk_gain, cos, sin, *, b, l, tm):
    m, d = x.shape
    nt = l // tm
    ssm_w = (_J_GATE - _J_U) * COL_TILE
    gate_w = (_J_END - _J_GATE) * COL_TILE
    u_tiles = _J_GATE - _J_U
    nh = HEADS_PER_GROUP
    out_specs = [
        pl.BlockSpec((None, nh, tm, HEAD_DIM), lambda i, j: (i // nt, jnp.clip(j - 1, 0, _J_V - 1), i % nt, 0)),
        pl.BlockSpec((None, nh, tm, HEAD_DIM), lambda i, j: (i // nt, jnp.clip(j - _J_V, 0, N_GROUPS - 1), i % nt, 0)),
    ]
    for g in range(N_GROUPS):
        out_specs.append(pl.BlockSpec((None, None, tm * nh, HEAD_DIM),
                                      lambda i, j, g=g: (i // nt, _kv_slot(j, g), i % nt, 0)))
    out_specs.append(pl.BlockSpec(
        (tm, COL_TILE), lambda i, j: (i % nt, (i // nt) * u_tiles + jnp.clip(j - _J_U, 0, u_tiles - 1))))
    out_specs.append(pl.BlockSpec(
        (tm, COL_TILE), lambda i, j: (i, jnp.clip(j - _J_GATE, 0, _J_END - _J_GATE - 1))))
    out_shape = [jax.ShapeDtypeStruct((b, _J_V * nh, l, HEAD_DIM), F32),
                 jax.ShapeDtypeStruct((b, N_GROUPS * nh, l, HEAD_DIM), F32)]
    out_shape += [jax.ShapeDtypeStruct((b, 2, l * nh, HEAD_DIM), F32)] * N_GROUPS
    out_shape += [jax.ShapeDtypeStruct((l, b * ssm_w), F32), jax.ShapeDtypeStruct((m, gate_w), F32)]
    return pl.pallas_call(
        functools.partial(_inproj_kernel, head_major=True),
        grid=(m // tm, _J_END),
        in_specs=_inproj_common_specs(d, tm, nt),
        out_specs=out_specs,
        out_shape=out_shape,
        scratch_shapes=[pltpu.VMEM((tm, d), BF16), pltpu.VMEM((2, tm, COL_TILE), F32)],
        compiler_params=_params(("parallel", "arbitrary")),
        name="inproj_prompt",
    )(x, gain.reshape(1, d), w, qk_gain, cos, sin)


def _inproj_sample(x, gain, w, qk_gain, cos, sin):
    m, d = x.shape
    u_tiles = _J_GATE - _J_U
    out_specs = [pl.BlockSpec((m, COL_TILE), lambda i, j: (0, jnp.clip(j - 1, 0, _J_V - 1))),
                 pl.BlockSpec((m, COL_TILE), lambda i, j: (0, jnp.clip(j - _J_V, 0, N_GROUPS - 1)))]
    for g in range(N_GROUPS):
        out_specs.append(pl.BlockSpec((None, m, COL_TILE), lambda i, j, g=g: (_kv_slot(j, g), 0, 0)))
    out_specs.append(pl.BlockSpec((m, COL_TILE), lambda i, j: (0, jnp.clip(j - _J_U, 0, u_tiles - 1))))
    out_specs.append(pl.BlockSpec((m, COL_TILE), lambda i, j: (0, jnp.clip(j - _J_GATE, 0, _J_END - _J_GATE - 1))))
    out_shape = [jax.ShapeDtypeStruct((m, _J_V * COL_TILE), F32), jax.ShapeDtypeStruct((m, N_GROUPS * COL_TILE), F32)]
    out_shape += [jax.ShapeDtypeStruct((2, m, GROUP_WIDTH), F32)] * N_GROUPS
    out_shape += [jax.ShapeDtypeStruct((m, u_tiles * COL_TILE), F32),
                  jax.ShapeDtypeStruct((m, (_J_END - _J_GATE) * COL_TILE), F32)]
    return pl.pallas_call(
        functools.partial(_inproj_kernel, head_major=False),
        grid=(1, _J_END),
        in_specs=_inproj_common_specs(d, m, 1),
        out_specs=out_specs,
        out_shape=out_shape,
        scratch_shapes=[pltpu.VMEM((m, d), BF16), pltpu.VMEM((2, m, COL_TILE), F32)],
        compiler_params=_params(("arbitrary", "arbitrary")),
        name="inproj_sample",
    )(x, gain.reshape(1, d), w, qk_gain, cos, sin)


def _rope_tables(pos):
    half = HEAD_DIM // 2
    inv = jnp.power(ROPE_THETA, -jnp.arange(half, dtype=F32) * (2.0 / HEAD_DIM))
    ang = pos.astype(F32)[:, None] * inv[None, :]
    cos, sin = jnp.cos(ang), jnp.sin(ang)
    return jnp.concatenate([cos, cos], axis=1), jnp.concatenate([-sin, sin], axis=1)


ATTN_UNROLL = 4


def _attn_prompt_kernel(*refs):
    qkv = refs[:3 * N_GROUPS]
    o_ref, m_ref, l_ref = refs[3 * N_GROUPS:]
    seq = o_ref.shape[0]
    row = lax.broadcasted_iota(jnp.int32, (QBLOCK, QBLOCK), 0)
    col = lax.broadcasted_iota(jnp.int32, (QBLOCK, QBLOCK), 1)
    cur_mask = col <= row
    prev_mask = col >= row
    nt_dims = (((1,), (1,)), ((), ()))
    full = (QBLOCK, HEAD_DIM)

    for g, (_, dil) in enumerate(ATT_GROUPS):
        q_ref, k_ref, v_ref = qkv[3 * g:3 * g + 3]
        nblk = seq // dil // QBLOCK
        span = QBLOCK * dil

        def rows_at(start, dil=dil):
            return pl.ds(start, QBLOCK) if dil == 1 else pl.ds(start, QBLOCK, stride=dil)

        def block(r, n, q_ref=q_ref, k_ref=k_ref, v_ref=v_ref, nblk=nblk, span=span, rows_at=rows_at):
            start = r + n * span
            rows = rows_at(start)
            q = q_ref[rows, :].astype(BF16)
            s_c = lax.dot_general(q, k_ref[rows, :].astype(BF16), nt_dims, preferred_element_type=F32)
            s_c = jnp.where(cur_mask, s_c * ATT_SCALE, -jnp.inf)
            mx = jnp.max(s_c, axis=1, keepdims=True)
            if nblk > 1:
                prows = rows_at(jnp.maximum(start - span, r))
                s_p = lax.dot_general(q, k_ref[prows, :].astype(BF16), nt_dims, preferred_element_type=F32)
                s_p = jnp.where(prev_mask, s_p * ATT_SCALE, -jnp.inf) + jnp.where(n > 0, 0.0, -jnp.inf)
                mx = jnp.maximum(mx, jnp.max(s_p, axis=1, keepdims=True))
            p_c = jnp.exp(s_c - mx)
            den = jnp.sum(p_c, axis=1, keepdims=True)
            acc = jnp.dot(p_c.astype(BF16), v_ref[rows, :].astype(BF16), preferred_element_type=F32)
            if nblk > 1:
                p_p = jnp.exp(s_p - mx)
                den = den + jnp.sum(p_p, axis=1, keepdims=True)
                acc = acc + jnp.dot(p_p.astype(BF16), v_ref[prows, :].astype(BF16), preferred_element_type=F32)
            return rows, acc, jnp.broadcast_to(mx, full), jnp.broadcast_to(den, full)

        def merge(rows, acc, mb, lb, first=(g == 0)):
            if first:
                o_ref[rows, :] = acc
                m_ref[rows, :] = mb
                l_ref[rows, :] = lb
            else:
                m_old = m_ref[rows, :]
                m_new = jnp.maximum(m_old, mb)
                a_old, a_new = jnp.exp(m_old - m_new), jnp.exp(mb - m_new)
                o_ref[rows, :] = a_old * o_ref[rows, :] + a_new * acc
                l_ref[rows, :] = a_old * l_ref[rows, :] + a_new * lb
                m_ref[rows, :] = m_new

        def body(i, carry, dil=dil, block=block, merge=merge):
            results = []
            for d in range(ATTN_UNROLL):
                idx = i * ATTN_UNROLL + d
                results.append(block(idx % dil, idx // dil))
            for res in results:
                merge(*res)
            return carry

        lax.fori_loop(0, dil * nblk // ATTN_UNROLL, body, 0)

    o_ref[...] = o_ref[...] / l_ref[...]


def _attn_prompt(qk, v, *, b, l):
    nh = HEADS_PER_GROUP
    in_specs, args = [], []
    for g in range(N_GROUPS):
        for part in range(3):
            in_specs.append(pl.BlockSpec(
                (None, None, l, HEAD_DIM),
                lambda bi, hs, g=g, part=part: (bi, (part % 2) * N_GROUPS * nh + g * nh + hs, 0, 0)))
            args.append(v if part == 2 else qk)
    return pl.pallas_call(
        _attn_prompt_kernel,
        grid=(b, nh),
        in_specs=in_specs,
        out_specs=pl.BlockSpec((None, None, l, HEAD_DIM), lambda bi, hs: (bi, hs, 0, 0)),
        out_shape=jax.ShapeDtypeStruct((b, nh, l, HEAD_DIM), F32),
        scratch_shapes=[pltpu.VMEM((l, HEAD_DIM), F32)] * 2,
        compiler_params=_params(("parallel", "parallel")),
        name="attn_prompt",
    )(*args)


def _attn_decode_kernel(q_ref, *refs):
    o_ref = refs[-1]
    outs, lses = [], []
    for g in range(N_GROUPS):
        kc_ref, vc_ref, kn_ref, vn_ref = refs[4 * g:4 * g + 4]
        qg = q_ref[g][None]
        s = jnp.sum(kc_ref[...] * qg, axis=-1, keepdims=True) * ATT_SCALE
        s_new = jnp.sum(kn_ref[...] * qg, axis=-1, keepdims=True) * ATT_SCALE
        mx = jnp.maximum(jnp.max(s, axis=0, keepdims=True), s_new)
        p = jnp.exp(s - mx)
        p_new = jnp.exp(s_new - mx)
        den = jnp.sum(p, axis=0, keepdims=True) + p_new
        acc = jnp.sum(p * vc_ref[...], axis=0, keepdims=True) + p_new * vn_ref[...]
        outs.append(acc / den)
        lses.append(mx + jnp.log(den))
    mx = functools.reduce(jnp.maximum, lses)
    ws = [jnp.exp(lse - mx) for lse in lses]
    num = functools.reduce(lambda a, c: a + c, [w * o for w, o in zip(ws, outs)])
    o_ref[...] = (num / functools.reduce(lambda a, c: a + c, ws))[0]


def _attn_decode(q, caches, kvnews):
    b = q.shape[0]
    nh = HEADS_PER_GROUP
    in_specs = [pl.BlockSpec((None, N_GROUPS, nh, HEAD_DIM), lambda i: (i, 0, 0, 0))]
    args = [q]
    for g, (window, dil) in enumerate(ATT_GROUPS):
        cache = caches[g]
        assert cache.shape[2] == window and window == N_BACK * dil
        cv = cache.reshape(b, 2, N_BACK, dil, nh, HEAD_DIM)
        in_specs += [
            pl.BlockSpec((None, None, N_BACK, None, nh, HEAD_DIM), lambda i: (i, 0, 0, 0, 0, 0)),
            pl.BlockSpec((None, None, N_BACK, None, nh, HEAD_DIM), lambda i: (i, 1, 0, 0, 0, 0)),
            pl.BlockSpec((None, None, 1, nh, HEAD_DIM), lambda i: (0, i, 0, 0, 0)),
            pl.BlockSpec((None, None, 1, nh, HEAD_DIM), lambda i: (1, i, 0, 0, 0)),
        ]
        args += [cv, cv, kvnews[g], kvnews[g]]
    return pl.pallas_call(
        _attn_decode_kernel,
        grid=(b,),
        in_specs=in_specs,
        out_specs=pl.BlockSpec((None, nh, HEAD_DIM), lambda i: (i, 0, 0)),
        out_shape=jax.ShapeDtypeStruct((b, nh, HEAD_DIM), F32),
        compiler_params=_params(("parallel",)),
        name="attn_decode",
    )(*args)


def _cache_shift_kernel(c_ref, new_ref, o_ref):
    w = c_ref.shape[0]
    o_ref[pl.ds(0, w - 1)] = c_ref[pl.ds(1, w - 1)]
    o_ref[pl.ds(w - 1, 1)] = new_ref[...]


def _cache_shift(cache, kvnew):
    b, _, w, nh, e = cache.shape
    return pl.pallas_call(
        _cache_shift_kernel,
        grid=(b, 2),
        in_specs=[
            pl.BlockSpec((None, None, w, nh, e), lambda i, s: (i, s, 0, 0, 0)),
            pl.BlockSpec((None, None, 1, nh, e), lambda i, s: (s, i, 0, 0, 0)),
        ],
        out_specs=pl.BlockSpec((None, None, w, nh, e), lambda i, s: (i, s, 0, 0, 0)),
        out_shape=jax.ShapeDtypeStruct(cache.shape, F32),
        compiler_params=_params(("parallel", "parallel")),
        name=f"cache_shift_w{w}",
    )(cache, kvnew)


def _ssm_prep_kernel(lr_ref, li_ref, ldt_ref, br_ref, bi_ref, cim_ref,
                     ar_ref, ai_ref, bbr_ref, bbi_ref, ncim_ref):
    lr, li = lr_ref[...], li_ref[...]
    dt = jnp.exp(ldt_ref[...])
    mag = jnp.exp(lr * dt)
    ar = mag * jnp.cos(li * dt)
    ai = mag * jnp.sin(li * dt)
    den = lr * lr + li * li
    fr = ((ar - 1.0) * lr + ai * li) / den
    fi = (ai * lr - (ar - 1.0) * li) / den
    br, bi = br_ref[...], bi_ref[...]
    ar_ref[...] = ar
    ai_ref[...] = ai
    bbr_ref[...] = fr * br - fi * bi
    bbi_ref[...] = fr * bi + fi * br
    ncim_ref[...] = -cim_ref[...]


def _ssm_prepare(lam_re, lam_im, log_dt, b_re, b_im, c_re, c_im):
    gn, pn = lam_re.shape
    cn = SSM_GROUP
    rows = gn * cn
    rep = lambda a: jnp.broadcast_to(a[:, None, :], (gn, cn, pn)).reshape(rows, pn)
    tr = lambda a: jnp.transpose(a, (0, 2, 1)).reshape(rows, pn)
    ldt = jnp.broadcast_to(log_dt[:, None, None], (gn, cn, pn)).reshape(rows, pn)
    shp = jax.ShapeDtypeStruct((rows, pn), F32)
    ar, ai, bbr, bbi, ncim = pl.pallas_call(
        _ssm_prep_kernel, out_shape=[shp] * 5, name="ssm_prep",
    )(rep(lam_re), rep(lam_im), ldt, tr(b_re), tr(b_im), c_im.reshape(rows, pn))
    nq, gb = gn // SSM_GROUPS_PER_BLOCK, SSM_GROUPS_PER_BLOCK
    diag = (jnp.arange(gb)[:, None, None, None] == jnp.arange(gb)[None, None, :, None])
    spread = lambda a: jnp.where(diag, a.reshape(2, nq, gb, cn, 1, pn), 0.0)
    bb = spread(jnp.stack([bbr, bbi]))
    bblk = jnp.transpose(bb, (1, 2, 3, 0, 4, 5)).reshape(nq, gb * cn, 2 * gb * pn).astype(BF16)
    cc = spread(jnp.stack([c_re.reshape(rows, pn), ncim]))
    cblk = jnp.transpose(cc, (1, 0, 2, 5, 4, 3)).reshape(nq, 2 * gb * pn, gb * cn).astype(BF16)
    a_re = ar.reshape(gn, cn, pn)[:, 0, :].reshape(gn * pn // LANES, LANES)
    a_im = ai.reshape(gn, cn, pn)[:, 0, :].reshape(gn * pn // LANES, LANES)
    return a_re, a_im, bblk, cblk


def _ssm_kernel(u_ref, x0_ref, bblk_ref, cblk_ref, are_ref, aim_ref, d_ref,
                y_ref, sfin_ref, ut_ref, yt_ref, bu_ref, xs_ref, st_ref, *, nb, steps):
    c = pl.program_id(0)
    nq = bblk_ref.shape[0]
    tiles_per_q = 2 * SSM_GROUPS_PER_BLOCK * SSM_STATE // LANES
    half = tiles_per_q // 2

    @pl.when(c == 0)
    def _():
        for j in range(nq * tiles_per_q):
            st_ref[j] = x0_ref[:, j * LANES:(j + 1) * LANES]

    width = nq * LANES
    for s in range(nb if steps > 1 else 1):
        for q in range(nq):
            if steps > 1:
                ut_ref[q, pl.ds(s, steps, stride=nb), :] = u_ref[:, s * width + q * LANES:s * width + (q + 1) * LANES]
            else:
                ut_ref[q] = u_ref[:, q * LANES:(q + 1) * LANES]
    for q in range(nq):
        res = jnp.dot(ut_ref[q].astype(BF16), bblk_ref[q], preferred_element_type=F32)
        for j in range(tiles_per_q):
            bu_ref[q * tiles_per_q + j] = res[:, j * LANES:(j + 1) * LANES]

    def scan_block(q, carry):
        for lt in range(half):
            jr = q * tiles_per_q + lt
            ji = jr + half
            ar = are_ref[q * half + lt]
            ai = aim_ref[q * half + lt]
            xr, xi = st_ref[jr], st_ref[ji]
            for t in range(steps):
                rows = pl.ds(t * nb, nb)
                xr, xi = (ar * xr - ai * xi + bu_ref[jr, rows, :],
                          ar * xi + ai * xr + bu_ref[ji, rows, :])
                xs_ref[jr, rows, :] = xr.astype(BF16)
                xs_ref[ji, rows, :] = xi.astype(BF16)
            st_ref[jr] = xr
            st_ref[ji] = xi
        return carry

    lax.fori_loop(0, nq, scan_block, 0)

    for q in range(nq):
        xq = jnp.concatenate([xs_ref[q * tiles_per_q + j] for j in range(tiles_per_q)], axis=1)
        yq = jnp.dot(xq, cblk_ref[q], preferred_element_type=F32)
        sl = slice(q * LANES, (q + 1) * LANES)
        yq = yq + d_ref[:, sl] * ut_ref[q]
        if steps > 1:
            yt_ref[q] = yq
        else:
            y_ref[:, sl] = yq
    if steps > 1:
        for s in range(nb):
            for q in range(nq):
                y_ref[:, s * width + q * LANES:s * width + (q + 1) * LANES] = yt_ref[q, pl.ds(s, steps, stride=nb), :]

    @pl.when(c == pl.num_programs(0) - 1)
    def _():
        for j in range(nq * tiles_per_q):
            sfin_ref[:, j * LANES:(j + 1) * LANES] = st_ref[j]


def _ssm(u, x0, a_re, a_im, bblk, cblk, d_skip, *, nb, steps):
    width = d_skip.shape[0]
    nsteps = u.shape[0] // steps if steps > 1 else 1
    blk = (steps, nb * width) if steps > 1 else (nb, width)
    rows = steps * nb
    ntile = a_re.shape[0]
    nstate = 2 * ntile * LANES
    are = jnp.broadcast_to(a_re[:, None, :], (ntile, nb, LANES))
    aim = jnp.broadcast_to(a_im[:, None, :], (ntile, nb, LANES))
    const = lambda shape: pl.BlockSpec(shape, lambda c: (0,) * len(shape))
    return pl.pallas_call(
        functools.partial(_ssm_kernel, nb=nb, steps=steps),
        grid=(nsteps,),
        in_specs=[
            pl.BlockSpec(blk, lambda c: (c, 0)),
            const((nb, nstate)),
            const(bblk.shape),
            const(cblk.shape),
            const(are.shape),
            const(aim.shape),
            const((1, width)),
        ],
        out_specs=[pl.BlockSpec(blk, lambda c: (c, 0)), const((nb, nstate))],
        out_shape=[jax.ShapeDtypeStruct(u.shape, F32), jax.ShapeDtypeStruct((nb, nstate), F32)],
        scratch_shapes=[
            pltpu.VMEM((width // LANES, rows, LANES), F32),
            pltpu.VMEM((width // LANES, rows, LANES), F32),
            pltpu.VMEM((2 * ntile, rows, LANES), F32),
            pltpu.VMEM((2 * ntile, rows, LANES), BF16),
            pltpu.VMEM((2 * ntile, nb, LANES), F32),
        ],
        compiler_params=_params(("arbitrary",)),
        name=f"ssm_nb{nb}",
    )(u, x0, bblk, cblk, are, aim, d_skip.reshape(1, width))


def _state_to_lanes(s_re, s_im):
    b, gn, pn = s_re.shape
    nq = gn // SSM_GROUPS_PER_BLOCK
    st = jnp.stack([s_re.reshape(b, nq, -1), s_im.reshape(b, nq, -1)], axis=2)
    return st.reshape(b, 2 * gn * pn)


def _lanes_to_state(s, gn, pn):
    b = s.shape[0]
    nq = gn // SSM_GROUPS_PER_BLOCK
    st = s.reshape(b, nq, 2, SSM_GROUPS_PER_BLOCK * pn)
    return st[:, :, 0].reshape(b, gn, pn), st[:, :, 1].reshape(b, gn, pn)


def _mix_kernel(x_ref, att_ref, y_ref, ga_ref, gs_ref, gw_ref, gb_ref, wa_ref, ws_ref, wo_ref, out_ref):
    att = jnp.concatenate([att_ref[h] for h in range(HEADS_PER_GROUP)], axis=1)
    a_proj = jnp.dot(att.astype(BF16), wa_ref[...], preferred_element_type=F32)
    zs = jax.nn.gelu(y_ref[...])
    glu = jnp.dot(zs.astype(BF16), gw_ref[...], preferred_element_type=F32) + gb_ref[...]
    s_out = zs * jax.nn.sigmoid(glu)
    s_proj = jnp.dot(s_out.astype(BF16), ws_ref[...], preferred_element_type=F32)
    merged = jax.nn.sigmoid(ga_ref[...]) * a_proj + jax.nn.sigmoid(gs_ref[...]) * s_proj
    out_ref[...] = x_ref[...] + jnp.dot(merged.astype(BF16), wo_ref[...], preferred_element_type=F32)


def _mix(x, att, y, y_map, gates, glu_w, glu_b, wa, ws, wo, *, tm, nt):
    m, d = x.shape
    sw = glu_w.shape[0]
    row = lambda width: pl.BlockSpec((tm, width), lambda i: (i, 0))
    const = lambda a: pl.BlockSpec(a.shape, lambda i: (0, 0), pipeline_mode=pl.Buffered(1))
    glu_b = glu_b.reshape(1, sw)
    in_specs = [row(d),
                pl.BlockSpec((None, HEADS_PER_GROUP, tm, HEAD_DIM), lambda i: (i // nt, 0, i % nt, 0)),
                pl.BlockSpec((tm, sw), y_map),
                pl.BlockSpec((tm, d), lambda i: (i, 0)), pl.BlockSpec((tm, d), lambda i: (i, 1)),
                const(glu_w), const(glu_b), const(wa), const(ws), const(wo)]
    return pl.pallas_call(
        _mix_kernel,
        grid=(m // tm,),
        in_specs=in_specs,
        out_specs=row(d),
        out_shape=jax.ShapeDtypeStruct((m, d), F32),
        compiler_params=_params(("parallel",)),
        name="mix",
    )(x, att, y, gates, gates, glu_w, glu_b, wa, ws, wo)


FFN_TF = 512
PROMPT_TM_FFN = 1024
PROMPT_TM_INPROJ = 1024
PROMPT_TM_MIX = 256
SSM_STEPS = 16


def _layer_weights(p):
    w = dict(p)
    for name in ("ffn1", "ffn2"):
        w[name] = _ffn_weights(p[name + "_w_gate"], p[name + "_w_up"], p[name + "_w_down"], FFN_TF)
    w["w_in_b"] = p["w_in"].astype(BF16)
    w["qk_gain"] = jnp.concatenate([p["q_norm"].reshape(1, -1), p["k_norm"].reshape(1, -1)], axis=1)
    w["ssm"] = _ssm_prepare(p["ssm_lambda_re"], p["ssm_lambda_im"], p["ssm_log_dt"],
                            p["ssm_b_re"], p["ssm_b_im"], p["ssm_c_re"], p["ssm_c_im"])
    for name in ("glu_w", "w_attn_branch", "w_ssm_branch", "w_out"):
        w[name + "_b"] = p[name].astype(BF16)
    return w


def _prompt_layer(x, w):
    b, l, d = x.shape
    m = b * l
    nh = HEADS_PER_GROUP
    gn, pn = w["ssm_lambda_re"].shape
    x1 = _ffn(x.reshape(m, d), w["ffn1_norm"], *w["ffn1"], tm=PROMPT_TM_FFN, tf=FFN_TF)
    cos, sin = _rope_tables(jnp.arange(l, dtype=jnp.int32))
    tm_in = min(PROMPT_TM_INPROJ, l)
    qk, v, kv0, kv1, kv2, u_t, gates = _inproj_prompt(
        x1, w["mix_norm"], w["w_in_b"], w["qk_gain"], cos, sin, b=b, l=l, tm=tm_in)
    att = _attn_prompt(qk, v, b=b, l=l)
    a_re, a_im, bblk, cblk = w["ssm"]
    y_t, sfin = _ssm(u_t, jnp.zeros((b, 2 * gn * pn), F32),
                     a_re, a_im, bblk, cblk, w["ssm_d"], nb=b, steps=SSM_STEPS)
    tm = PROMPT_TM_MIX
    nt = l // tm
    x2 = _mix(x1, att, y_t, lambda i: (i % nt, i // nt), gates,
              w["glu_w_b"], w["glu_b"], w["w_attn_branch_b"], w["w_ssm_branch_b"], w["w_out_b"], tm=tm, nt=nt)
    y = _ffn(x2, w["ffn2_norm"], *w["ffn2"], tm=PROMPT_TM_FFN, tf=FFN_TF)
    new_kv = []
    for g, (window, _) in enumerate(ATT_GROUPS):
        keep = min(window, l)
        kv = (kv0, kv1, kv2)[g]
        new_kv.append(kv[:, :, (l - keep) * nh:].reshape(b, 2, keep, nh, HEAD_DIM))
    s_re, s_im = _lanes_to_state(sfin, gn, pn)
    return y.reshape(b, l, d), new_kv, (s_re, s_im)


def _sample_layer(x, pos0, caches, state, w):
    b, l, d = x.shape
    assert l == 1
    nh = HEADS_PER_GROUP
    gn, pn = w["ssm_lambda_re"].shape
    x1 = _ffn(x.reshape(b, d), w["ffn1_norm"], *w["ffn1"], tm=b, tf=FFN_TF)
    cos, sin = _rope_tables(jnp.full((b,), pos0, dtype=jnp.int32))
    qk, _, kn0, kn1, kn2, u, gates = _inproj_sample(x1, w["mix_norm"], w["w_in_b"], w["qk_gain"], cos, sin)
    q = qk[:, :ATT_WIDTH].reshape(b, N_GROUPS, nh, HEAD_DIM)
    kvnews = [kn.reshape(2, b, 1, nh, HEAD_DIM) for kn in (kn0, kn1, kn2)]
    att = _attn_decode(q, caches, kvnews)
    new_kv = [_cache_shift(caches[g], kvnews[g]) for g in range(N_GROUPS)]
    a_re, a_im, bblk, cblk = w["ssm"]
    y, sfin = _ssm(u, _state_to_lanes(*state), a_re, a_im, bblk, cblk, w["ssm_d"], nb=b, steps=1)
    att_h = jnp.transpose(att, (1, 0, 2))[None]
    x2 = _mix(x1, att_h, y, lambda i: (i, 0), gates,
              w["glu_w_b"], w["glu_b"], w["w_attn_branch_b"], w["w_ssm_branch_b"], w["w_out_b"], tm=b, nt=1)
    out = _ffn(x2, w["ffn2_norm"], *w["ffn2"], tm=b, tf=FFN_TF)
    s_re, s_im = _lanes_to_state(sfin, gn, pn)
    return out.reshape(b, l, d), new_kv, (s_re, s_im)


def kernel(x_prompt, x_sample, cache_kv_w128, cache_kv_w512, cache_kv_w2048, state_ssm_re, state_ssm_im,
           ffn1_norm, ffn1_w_gate, ffn1_w_up, ffn1_w_down, mix_norm, w_in, q_norm, k_norm,
           ssm_lambda_re, ssm_lambda_im, ssm_log_dt, ssm_b_re, ssm_b_im, ssm_c_re, ssm_c_im, ssm_d,
           glu_w, glu_b, w_attn_branch, w_ssm_branch, w_out,
           ffn2_norm, ffn2_w_gate, ffn2_w_up, ffn2_w_down):
    depth = w_in.shape[0]
    params = dict(ffn1_norm=ffn1_norm, ffn1_w_gate=ffn1_w_gate, ffn1_w_up=ffn1_w_up, ffn1_w_down=ffn1_w_down,
                  mix_norm=mix_norm, w_in=w_in, q_norm=q_norm, k_norm=k_norm,
                  ssm_lambda_re=ssm_lambda_re, ssm_lambda_im=ssm_lambda_im, ssm_log_dt=ssm_log_dt,
                  ssm_b_re=ssm_b_re, ssm_b_im=ssm_b_im, ssm_c_re=ssm_c_re, ssm_c_im=ssm_c_im, ssm_d=ssm_d,
                  glu_w=glu_w, glu_b=glu_b, w_attn_branch=w_attn_branch, w_ssm_branch=w_ssm_branch,
                  w_out=w_out, ffn2_norm=ffn2_norm, ffn2_w_gate=ffn2_w_gate, ffn2_w_up=ffn2_w_up,
                  ffn2_w_down=ffn2_w_down)
    yp, ys = x_prompt, x_sample
    new_p = [[] for _ in range(5)]
    new_s = [[] for _ in range(5)]
    for layer in range(depth):
        w = _layer_weights({k: v[layer] for k, v in params.items()})
        yp, kv_p, st_p = _prompt_layer(yp, w)
        ys, kv_s, st_s = _sample_layer(
            ys, PAST_LEN, (cache_kv_w128[layer], cache_kv_w512[layer], cache_kv_w2048[layer]),
            (state_ssm_re[layer], state_ssm_im[layer]), w)
        for i, a in enumerate(list(kv_p) + list(st_p)):
            new_p[i].append(a)
        for i, a in enumerate(list(kv_s) + list(st_s)):
            new_s[i].append(a)
    outs_p = [jnp.stack(a) for a in new_p]
    outs_s = [jnp.stack(a) for a in new_s]
    return (yp, ys, *outs_p, *outs_s)
```

```python
import functools

import jax
import jax.numpy as jnp
from jax import lax
from jax.experimental import pallas as pl
from jax.experimental.pallas import tpu as pltpu

F32 = jnp.float32
BF16 = jnp.bfloat16

HEAD_DIM = 128
HEADS_PER_GROUP = 4
GROUP_WIDTH = HEADS_PER_GROUP * HEAD_DIM
ATT_GROUPS = ((128, 1), (512, 4), (2048, 16))
N_GROUPS = len(ATT_GROUPS)
ATT_WIDTH = N_GROUPS * GROUP_WIDTH
N_BACK = 128
QBLOCK = 128
ATT_SCALE = HEAD_DIM ** -0.5
ROPE_THETA = 10000.0
RMS_EPS = 1e-6
PAST_LEN = 16384
SSM_GROUP = 16
SSM_STATE = 64
SSM_GROUPS_PER_BLOCK = 8
LANES = 128
COL_TILE = 512
VMEM_LIMIT = 56 * 1024 * 1024


def _params(semantics):
    return pltpu.CompilerParams(dimension_semantics=semantics, vmem_limit_bytes=VMEM_LIMIT)


def _rms_rows(x, gain):
    ms = jnp.mean(x * x, axis=-1, keepdims=True)
    return x * lax.rsqrt(ms + RMS_EPS) * gain


def _head(h):
    return slice(h * HEAD_DIM, (h + 1) * HEAD_DIM)


def _ffn_kernel(x_ref, g_ref, wg_ref, wu_ref, wd_ref, o_ref, h_ref, *, d_ff):
    @pl.when(pl.program_id(1) == 0)
    def _():
        x = x_ref[...]
        h_ref[...] = _rms_rows(x, g_ref[...]).astype(BF16)
        o_ref[...] = x

    h = h_ref[...]
    g = jnp.dot(h, wg_ref[...], preferred_element_type=F32)
    u = jnp.dot(h, wu_ref[...], preferred_element_type=F32)
    col = pl.program_id(1) * g.shape[1] + lax.broadcasted_iota(jnp.int32, g.shape, 1)
    a = jnp.where(col < d_ff, 0.5 * (g * jax.nn.sigmoid(g) * u), 0.0).astype(BF16)
    o_ref[...] += jnp.dot(a, wd_ref[...], preferred_element_type=F32)


def _ffn(x, gain, wg, wu, wd, *, tm, tf):
    m, d = x.shape
    d_ff = wg.shape[1]
    return pl.pallas_call(
        functools.partial(_ffn_kernel, d_ff=d_ff),
        grid=(m // tm, wd.shape[0] // tf),
        in_specs=[
            pl.BlockSpec((tm, d), lambda i, f: (i, 0)),
            pl.BlockSpec((1, d), lambda i, f: (0, 0)),
            pl.BlockSpec((d, tf), lambda i, f: (0, f)),
            pl.BlockSpec((d, tf), lambda i, f: (0, f)),
            pl.BlockSpec((tf, d), lambda i, f: (f, 0)),
        ],
        out_specs=pl.BlockSpec((tm, d), lambda i, f: (i, 0)),
        out_shape=jax.ShapeDtypeStruct((m, d), F32),
        scratch_shapes=[pltpu.VMEM((tm, d), BF16)],
        compiler_params=_params(("parallel", "arbitrary")),
        name="ffn",
    )(x, gain.reshape(1, d), wg, wu, wd)


def _ffn_weights(wg, wu, wd, tf):
    pad = (-wg.shape[1]) % tf
    return wg.astype(BF16), wu.astype(BF16), jnp.pad(wd.astype(BF16), ((0, pad), (0, 0)))


_J_K, _J_V, _J_U, _J_GATE, _J_END = 3, 6, 9, 11, 19
EPILOGUE_ROWS = 256


def _inproj_kernel(x_ref, g_ref, w_ref, qkg_ref, cos_ref, sin_ref,
                   qk_ref, v_ref, kv0_ref, kv1_ref, kv2_ref, u_ref, gate_ref, h_ref, z_ref, *, head_major):
    j = pl.program_id(1)
    rows = h_ref.shape[0]
    nh = HEADS_PER_GROUP
    kv_refs = (kv0_ref, kv1_ref, kv2_ref)

    @pl.when(j == 0)
    def _():
        h_ref[...] = _rms_rows(x_ref[...], g_ref[...]).astype(BF16)

    def zdot():
        return jnp.dot(h_ref[...], w_ref[...], preferred_element_type=F32)

    def put_heads(ref, heads, r0, n):
        if head_major:
            for h in range(nh):
                ref[h, pl.ds(r0, n), :] = heads[h]
        else:
            ref[pl.ds(r0, n), :] = jnp.concatenate(heads, axis=1)

    def put_cache(ref, heads, r0, n):
        if head_major:
            for h in range(nh):
                ref[pl.ds(r0 * nh + h, n, stride=nh), :] = heads[h]
        else:
            ref[pl.ds(r0, n), :] = jnp.concatenate(heads, axis=1)

    def norm_rope_tile(tile):
        z_src = z_ref.at[tile % 2]
        gain = qkg_ref[...]
        n = min(EPILOGUE_ROWS, rows)
        for r0 in range(0, rows, n):
            cos, sin = cos_ref[pl.ds(r0, n), :], sin_ref[pl.ds(r0, n), :]
            heads = []
            for h in range(nh):
                y = _rms_rows(z_src[pl.ds(r0, n), _head(h)], gain[:, _head(h)])
                heads.append(y * cos + pltpu.roll(y, HEAD_DIM // 2, axis=1) * sin)
            put_heads(qk_ref, heads, r0, n)
            if tile >= _J_K:
                put_cache(kv_refs[tile - _J_K], heads, r0, n)

    def v_tile(g):
        z = zdot()
        heads = [z[:, _head(h)] for h in range(nh)]
        put_heads(v_ref, heads, 0, rows)
        put_cache(kv_refs[g], heads, 0, rows)

    for t in range(_J_V + 1):
        @pl.when(j == t)
        def _(t=t):
            if t < _J_V:
                z_ref[t % 2] = zdot()
            else:
                v_tile(0)
            if t > 0:
                norm_rope_tile(t - 1)

    for g in range(1, N_GROUPS):
        @pl.when(j == _J_V + g)
        def _(g=g):
            v_tile(g)

    @pl.when((j >= _J_U) & (j < _J_GATE))
    def _():
        u_ref[...] = zdot()

    @pl.when(j >= _J_GATE)
    def _():
        gate_ref[...] = zdot()


def _kv_slot(j, g):
    return jnp.where(j > _J_K + g + 1, 1, 0)


def _inproj_common_specs(d, tm, nt):
    return [
        pl.BlockSpec((tm, d), lambda i, j: (i, 0), pipeline_mode=pl.Buffered(1)),
        pl.BlockSpec((1, d), lambda i, j: (0, 0)),
        pl.BlockSpec((d, COL_TILE), lambda i, j: (0, j)),
        pl.BlockSpec((1, COL_TILE), lambda i, j: (0, jnp.clip(j - 1, 0, _J_V - 1))),
        pl.BlockSpec((tm, HEAD_DIM), lambda i, j: (i % nt, 0)),
        pl.BlockSpec((tm, HEAD_DIM), lambda i, j: (i % nt, 0)),
    ]


def _inproj_prompt(x, gain, w, qk_gain, cos, sin, *, b, l, tm):
    m, d = x.shape
    nt = l // tm
    ssm_w = (_J_GATE - _J_U) * COL_TILE
    gate_w = (_J_END - _J_GATE) * COL_TILE
    u_tiles = _J_GATE - _J_U
    nh = HEADS_PER_GROUP
    out_specs = [
        pl.BlockSpec((None, nh, tm, HEAD_DIM), lambda i, j: (i // nt, jnp.clip(j - 1, 0, _J_V - 1), i % nt, 0)),
        pl.BlockSpec((None, nh, tm, HEAD_DIM), lambda i, j: (i // nt, jnp.clip(j - _J_V, 0, N_GROUPS - 1), i % nt, 0)),
    ]
    for g in range(N_GROUPS):
        out_specs.append(pl.BlockSpec((None, None, tm * nh, HEAD_DIM),
                                      lambda i, j, g=g: (i // nt, _kv_slot(j, g), i % nt, 0)))
    out_specs.append(pl.BlockSpec(
        (tm, COL_TILE), lambda i, j: (i % nt, (i // nt) * u_tiles + jnp.clip(j - _J_U, 0, u_tiles - 1))))
    out_specs.append(pl.BlockSpec(
        (tm, COL_TILE), lambda i, j: (i, jnp.clip(j - _J_GATE, 0, _J_END - _J_GATE - 1))))
    out_shape = [jax.ShapeDtypeStruct((b, _J_V * nh, l, HEAD_DIM), F32),
                 jax.ShapeDtypeStruct((b, N_GROUPS * nh, l, HEAD_DIM), F32)]
    out_shape += [jax.ShapeDtypeStruct((b, 2, l * nh, HEAD_DIM), F32)] * N_GROUPS
    out_shape += [jax.ShapeDtypeStruct((l, b * ssm_w), F32), jax.ShapeDtypeStruct((m, gate_w), F32)]
    return pl.pallas_call(
        functools.partial(_inproj_kernel, head_major=True),
        grid=(m // tm, _J_END),
        in_specs=_inproj_common_specs(d, tm, nt),
        out_specs=out_specs,
        out_shape=out_shape,
        scratch_shapes=[pltpu.VMEM((tm, d), BF16), pltpu.VMEM((2, tm, COL_TILE), F32)],
        compiler_params=_params(("parallel", "arbitrary")),
        name="inproj_prompt",
    )(x, gain.reshape(1, d), w, qk_gain, cos, sin)


def _inproj_sample(x, gain, w, qk_gain, cos, sin):
    m, d = x.shape
    u_tiles = _J_GATE - _J_U
    out_specs = [pl.BlockSpec((m, COL_TILE), lambda i, j: (0, jnp.clip(j - 1, 0, _J_V - 1))),
                 pl.BlockSpec((m, COL_TILE), lambda i, j: (0, jnp.clip(j - _J_V, 0, N_GROUPS - 1)))]
    for g in range(N_GROUPS):
        out_specs.append(pl.BlockSpec((None, m, COL_TILE), lambda i, j, g=g: (_kv_slot(j, g), 0, 0)))
    out_specs.append(pl.BlockSpec((m, COL_TILE), lambda i, j: (0, jnp.clip(j - _J_U, 0, u_tiles - 1))))
    out_specs.append(pl.BlockSpec((m, COL_TILE), lambda i, j: (0, jnp.clip(j - _J_GATE, 0, _J_END - _J_GATE - 1))))
    out_shape = [jax.ShapeDtypeStruct((m, _J_V * COL_TILE), F32), jax.ShapeDtypeStruct((m, N_GROUPS * COL_TILE), F32)]
    out_shape += [jax.ShapeDtypeStruct((2, m, GROUP_WIDTH), F32)] * N_GROUPS
    out_shape += [jax.ShapeDtypeStruct((m, u_tiles * COL_TILE), F32),
                  jax.ShapeDtypeStruct((m, (_J_END - _J_GATE) * COL_TILE), F32)]
    return pl.pallas_call(
        functools.partial(_inproj_kernel, head_major=False),
        grid=(1, _J_END),
        in_specs=_inproj_common_specs(d, m, 1),
        out_specs=out_specs,
        out_shape=out_shape,
        scratch_shapes=[pltpu.VMEM((m, d), BF16), pltpu.VMEM((2, m, COL_TILE), F32)],
        compiler_params=_params(("arbitrary", "arbitrary")),
        name="inproj_sample",
    )(x, gain.reshape(1, d), w, qk_gain, cos, sin)


def _rope_tables(pos):
    half = HEAD_DIM // 2
    inv = jnp.power(ROPE_THETA, -jnp.arange(half, dtype=F32) * (2.0 / HEAD_DIM))
    ang = pos.astype(F32)[:, None] * inv[None, :]
    cos, sin = jnp.cos(ang), jnp.sin(ang)
    return jnp.concatenate([cos, cos], axis=1), jnp.concatenate([-sin, sin], axis=1)


ATTN_UNROLL = 4


def _attn_prompt_kernel(*refs):
    qkv = refs[:3 * N_GROUPS]
    o_ref = refs[3 * N_GROUPS]
    stats = (refs[3 * N_GROUPS + 1:3 * N_GROUPS + 4], refs[3 * N_GROUPS + 4:3 * N_GROUPS + 7])
    seq = o_ref.shape[0]
    row = lax.broadcasted_iota(jnp.int32, (QBLOCK, QBLOCK), 0)
    col = lax.broadcasted_iota(jnp.int32, (QBLOCK, QBLOCK), 1)
    cur_mask = col <= row
    prev_mask = col >= row
    nt_dims = (((1,), (1,)), ((), ()))
    full = (QBLOCK, HEAD_DIM)
    ones = jnp.ones(full, BF16)

    for g, (_, dil) in enumerate(ATT_GROUPS):
        q_ref, k_ref, v_ref = qkv[3 * g:3 * g + 3]
        nblk = seq // dil // QBLOCK
        span = QBLOCK * dil

        def rows_at(start, dil=dil):
            return pl.ds(start, QBLOCK) if dil == 1 else pl.ds(start, QBLOCK, stride=dil)

        def block(r, n, q_ref=q_ref, k_ref=k_ref, v_ref=v_ref, nblk=nblk, span=span, rows_at=rows_at):
            start = r + n * span
            rows = rows_at(start)
            q = q_ref[rows, :].astype(BF16)
            k = k_ref[rows, :].astype(BF16)
            v = jnp.concatenate([v_ref[rows, :].astype(BF16), ones], axis=1)
            if nblk > 1:
                prows = rows_at(jnp.maximum(start - span, r))
                k = jnp.concatenate([k, k_ref[prows, :].astype(BF16)], axis=0)
                vp = jnp.concatenate([v_ref[prows, :].astype(BF16), ones], axis=1)
                v = jnp.concatenate([v, vp], axis=0)
            s = lax.dot_general(q, k, nt_dims, preferred_element_type=F32) * ATT_SCALE
            if nblk > 1:
                prev_pen = jnp.where(n > 0, 0.0, -jnp.inf)
                s = jnp.concatenate([jnp.where(cur_mask, s[:, :QBLOCK], -jnp.inf),
                                     jnp.where(prev_mask, s[:, QBLOCK:], -jnp.inf) + prev_pen], axis=1)
                mx = jnp.max(jnp.maximum(s[:, :QBLOCK], s[:, QBLOCK:]), axis=1, keepdims=True)
            else:
                s = jnp.where(cur_mask, s, -jnp.inf)
                mx = jnp.max(s, axis=1, keepdims=True)
            p = jnp.exp(s - mx).astype(BF16)
            acc = jnp.dot(p, v, preferred_element_type=F32)
            return rows, acc[:, :HEAD_DIM], jnp.broadcast_to(mx, full), acc[:, HEAD_DIM:]

        src = stats[(g - 1) % 2] if g > 0 else None
        dst = stats[g % 2] if g < N_GROUPS - 1 else None

        def merge(rows, acc, mb, lb, src=src, dst=dst):
            if src is not None:
                m_old = src[1][rows, :]
                m_new = jnp.maximum(m_old, mb)
                a_old, a_new = jnp.exp(m_old - m_new), jnp.exp(mb - m_new)
                acc = a_old * src[0][rows, :] + a_new * acc
                lb = a_old * src[2][rows, :] + a_new * lb
                mb = m_new
            if dst is not None:
                dst[0][rows, :] = acc
                dst[1][rows, :] = mb
                dst[2][rows, :] = lb
            else:
                o_ref[rows, :] = acc / lb

        def body(i, carry, dil=dil, block=block, merge=merge):
            results = []
            for d in range(ATTN_UNROLL):
                idx = i * ATTN_UNROLL + d
                results.append(block(idx % dil, idx // dil))
            for res in results:
                merge(*res)
            return carry

        lax.fori_loop(0, dil * nblk // ATTN_UNROLL, body, 0)


def _attn_prompt(qk, v, *, b, l):
    nh = HEADS_PER_GROUP
    in_specs, args = [], []
    for g in range(N_GROUPS):
        for part in range(3):
            in_specs.append(pl.BlockSpec(
                (None, None, l, HEAD_DIM),
                lambda bi, hs, g=g, part=part: (bi, (part % 2) * N_GROUPS * nh + g * nh + hs, 0, 0)))
            args.append(v if part == 2 else qk)
    return pl.pallas_call(
        _attn_prompt_kernel,
        grid=(b, nh),
        in_specs=in_specs,
        out_specs=pl.BlockSpec((None, None, l, HEAD_DIM), lambda bi, hs: (bi, hs, 0, 0)),
        out_shape=jax.ShapeDtypeStruct((b, nh, l, HEAD_DIM), F32),
        scratch_shapes=[pltpu.VMEM((l, HEAD_DIM), F32)] * 6,
        compiler_params=_params(("parallel", "parallel")),
        name="attn_prompt",
    )(*args)


def _attn_decode_kernel(q_ref, *refs):
    o_ref = refs[-1]
    outs, lses = [], []
    for g in range(N_GROUPS):
        kc_ref, vc_ref, kn_ref, vn_ref = refs[4 * g:4 * g + 4]
        qg = q_ref[g][None]
        s = jnp.sum(kc_ref[...] * qg, axis=-1, keepdims=True) * ATT_SCALE
        s_new = jnp.sum(kn_ref[...] * qg, axis=-1, keepdims=True) * ATT_SCALE
        mx = jnp.maximum(jnp.max(s, axis=0, keepdims=True), s_new)
        p = jnp.exp(s - mx)
        p_new = jnp.exp(s_new - mx)
        den = jnp.sum(p, axis=0, keepdims=True) + p_new
        acc = jnp.sum(p * vc_ref[...], axis=0, keepdims=True) + p_new * vn_ref[...]
        outs.append(acc / den)
        lses.append(mx + jnp.log(den))
    mx = functools.reduce(jnp.maximum, lses)
    ws = [jnp.exp(lse - mx) for lse in lses]
    num = functools.reduce(lambda a, c: a + c, [w * o for w, o in zip(ws, outs)])
    o_ref[...] = (num / functools.reduce(lambda a, c: a + c, ws))[0]


def _attn_decode(q, caches, kvnews):
    b = q.shape[0]
    nh = HEADS_PER_GROUP
    in_specs = [pl.BlockSpec((None, N_GROUPS, nh, HEAD_DIM), lambda i: (i, 0, 0, 0))]
    args = [q]
    for g, (window, dil) in enumerate(ATT_GROUPS):
        cache = caches[g]
        assert cache.shape[2] == window and window == N_BACK * dil
        cv = cache.reshape(b, 2, N_BACK, dil, nh, HEAD_DIM)
        in_specs += [
            pl.BlockSpec((None, None, N_BACK, None, nh, HEAD_DIM), lambda i: (i, 0, 0, 0, 0, 0)),
            pl.BlockSpec((None, None, N_BACK, None, nh, HEAD_DIM), lambda i: (i, 1, 0, 0, 0, 0)),
            pl.BlockSpec((None, None, 1, nh, HEAD_DIM), lambda i: (0, i, 0, 0, 0)),
            pl.BlockSpec((None, None, 1, nh, HEAD_DIM), lambda i: (1, i, 0, 0, 0)),
        ]
        args += [cv, cv, kvnews[g], kvnews[g]]
    return pl.pallas_call(
        _attn_decode_kernel,
        grid=(b,),
        in_specs=in_specs,
        out_specs=pl.BlockSpec((None, nh, HEAD_DIM), lambda i: (i, 0, 0)),
        out_shape=jax.ShapeDtypeStruct((b, nh, HEAD_DIM), F32),
        compiler_params=_params(("parallel",)),
        name="attn_decode",
    )(*args)


def _cache_shift_kernel(c_ref, new_ref, o_ref):
    w = c_ref.shape[0]
    o_ref[pl.ds(0, w - 1)] = c_ref[pl.ds(1, w - 1)]
    o_ref[pl.ds(w - 1, 1)] = new_ref[...]


def _cache_shift(cache, kvnew):
    b, _, w, nh, e = cache.shape
    return pl.pallas_call(
        _cache_shift_kernel,
        grid=(b, 2),
        in_specs=[
            pl.BlockSpec((None, None, w, nh, e), lambda i, s: (i, s, 0, 0, 0)),
            pl.BlockSpec((None, None, 1, nh, e), lambda i, s: (s, i, 0, 0, 0)),
        ],
        out_specs=pl.BlockSpec((None, None, w, nh, e), lambda i, s: (i, s, 0, 0, 0)),
        out_shape=jax.ShapeDtypeStruct(cache.shape, F32),
        compiler_params=_params(("parallel", "parallel")),
        name=f"cache_shift_w{w}",
    )(cache, kvnew)


def _ssm_prep_kernel(lr_ref, li_ref, ldt_ref, br_ref, bi_ref, cim_ref,
                     ar_ref, ai_ref, bbr_ref, bbi_ref, ncim_ref):
    lr, li = lr_ref[...], li_ref[...]
    dt = jnp.exp(ldt_ref[...])
    mag = jnp.exp(lr * dt)
    ar = mag * jnp.cos(li * dt)
    ai = mag * jnp.sin(li * dt)
    den = lr * lr + li * li
    fr = ((ar - 1.0) * lr + ai * li) / den
    fi = (ai * lr - (ar - 1.0) * li) / den
    br, bi = br_ref[...], bi_ref[...]
    ar_ref[...] = ar
    ai_ref[...] = ai
    bbr_ref[...] = fr * br - fi * bi
    bbi_ref[...] = fr * bi + fi * br
    ncim_ref[...] = -cim_ref[...]


def _ssm_prepare(lam_re, lam_im, log_dt, b_re, b_im, c_re, c_im):
    gn, pn = lam_re.shape
    cn = SSM_GROUP
    rows = gn * cn
    rep = lambda a: jnp.broadcast_to(a[:, None, :], (gn, cn, pn)).reshape(rows, pn)
    tr = lambda a: jnp.transpose(a, (0, 2, 1)).reshape(rows, pn)
    ldt = jnp.broadcast_to(log_dt[:, None, None], (gn, cn, pn)).reshape(rows, pn)
    shp = jax.ShapeDtypeStruct((rows, pn), F32)
    ar, ai, bbr, bbi, ncim = pl.pallas_call(
        _ssm_prep_kernel, out_shape=[shp] * 5, name="ssm_prep",
    )(rep(lam_re), rep(lam_im), ldt, tr(b_re), tr(b_im), c_im.reshape(rows, pn))
    nq, gb = gn // SSM_GROUPS_PER_BLOCK, SSM_GROUPS_PER_BLOCK
    diag = (jnp.arange(gb)[:, None, None, None] == jnp.arange(gb)[None, None, :, None])
    spread = lambda a: jnp.where(diag, a.reshape(2, nq, gb, cn, 1, pn), 0.0)
    bb = spread(jnp.stack([bbr, bbi]))
    bblk = jnp.transpose(bb, (1, 2, 3, 0, 4, 5)).reshape(nq, gb * cn, 2 * gb * pn).astype(BF16)
    cc = spread(jnp.stack([c_re.reshape(rows, pn), ncim]))
    cblk = jnp.transpose(cc, (1, 0, 2, 5, 4, 3)).reshape(nq, 2 * gb * pn, gb * cn).astype(BF16)
    a_re = ar.reshape(gn, cn, pn)[:, 0, :].reshape(gn * pn // LANES, LANES)
    a_im = ai.reshape(gn, cn, pn)[:, 0, :].reshape(gn * pn // LANES, LANES)
    return a_re, a_im, bblk, cblk


def _ssm_kernel(u_ref, x0_ref, bblk_ref, cblk_ref, are_ref, aim_ref, d_ref,
                y_ref, sfin_ref, ut_ref, yt_ref, bu_ref, xs_ref, st_ref, *, nb, steps):
    c = pl.program_id(0)
    nq = bblk_ref.shape[0]
    tiles_per_q = 2 * SSM_GROUPS_PER_BLOCK * SSM_STATE // LANES
    half = tiles_per_q // 2

    @pl.when(c == 0)
    def _():
        for j in range(nq * tiles_per_q):
            st_ref[j] = x0_ref[:, j * LANES:(j + 1) * LANES]

    width = nq * LANES
    for s in range(nb if steps > 1 else 1):
        for q in range(nq):
            if steps > 1:
                ut_ref[q, pl.ds(s, steps, stride=nb), :] = u_ref[:, s * width + q * LANES:s * width + (q + 1) * LANES]
            else:
                ut_ref[q] = u_ref[:, q * LANES:(q + 1) * LANES]
    for q in range(nq):
        res = jnp.dot(ut_ref[q].astype(BF16), bblk_ref[q], preferred_element_type=F32)
        for j in range(tiles_per_q):
            bu_ref[q * tiles_per_q + j] = res[:, j * LANES:(j + 1) * LANES]

    def scan_block(q, carry):
        for lt in range(half):
            jr = q * tiles_per_q + lt
            ji = jr + half
            ar = are_ref[q * half + lt]
            ai = aim_ref[q * half + lt]
            xr, xi = st_ref[jr], st_ref[ji]
            for t in range(steps):
                rows = pl.ds(t * nb, nb)
                xr, xi = (ar * xr - ai * xi + bu_ref[jr, rows, :],
                          ar * xi + ai * xr + bu_ref[ji, rows, :])
                xs_ref[jr, rows, :] = xr.astype(BF16)
                xs_ref[ji, rows, :] = xi.astype(BF16)
            st_ref[jr] = xr
            st_ref[ji] = xi
        return carry

    for q in range(nq):
        scan_block(q, 0)

    for q in range(nq):
        xq = jnp.concatenate([xs_ref[q * tiles_per_q + j] for j in range(tiles_per_q)], axis=1)
        yq = jnp.dot(xq, cblk_ref[q], preferred_element_type=F32)
        sl = slice(q * LANES, (q + 1) * LANES)
        yq = yq + d_ref[:, sl] * ut_ref[q]
        if steps > 1:
            yt_ref[q] = yq
        else:
            y_ref[:, sl] = yq
    if steps > 1:
        for s in range(nb):
            for q in range(nq):
                y_ref[:, s * width + q * LANES:s * width + (q + 1) * LANES] = yt_ref[q, pl.ds(s, steps, stride=nb), :]

    @pl.when(c == pl.num_programs(0) - 1)
    def _():
        for j in range(nq * tiles_per_q):
            sfin_ref[:, j * LANES:(j + 1) * LANES] = st_ref[j]


def _ssm(u, x0, a_re, a_im, bblk, cblk, d_skip, *, nb, steps):
    width = d_skip.shape[0]
    nsteps = u.shape[0] // steps if steps > 1 else 1
    blk = (steps, nb * width) if steps > 1 else (nb, width)
    rows = steps * nb
    ntile = a_re.shape[0]
    nstate = 2 * ntile * LANES
    are = jnp.broadcast_to(a_re[:, None, :], (ntile, nb, LANES))
    aim = jnp.broadcast_to(a_im[:, None, :], (ntile, nb, LANES))
    const = lambda shape: pl.BlockSpec(shape, lambda c: (0,) * len(shape))
    return pl.pallas_call(
        functools.partial(_ssm_kernel, nb=nb, steps=steps),
        grid=(nsteps,),
        in_specs=[
            pl.BlockSpec(blk, lambda c: (c, 0)),
            const((nb, nstate)),
            const(bblk.shape),
            const(cblk.shape),
            const(are.shape),
            const(aim.shape),
            const((1, width)),
        ],
        out_specs=[pl.BlockSpec(blk, lambda c: (c, 0)), const((nb, nstate))],
        out_shape=[jax.ShapeDtypeStruct(u.shape, F32), jax.ShapeDtypeStruct((nb, nstate), F32)],
        scratch_shapes=[
            pltpu.VMEM((width // LANES, rows, LANES), F32),
            pltpu.VMEM((width // LANES, rows, LANES), F32),
            pltpu.VMEM((2 * ntile, rows, LANES), F32),
            pltpu.VMEM((2 * ntile, rows, LANES), BF16),
            pltpu.VMEM((2 * ntile, nb, LANES), F32),
        ],
        compiler_params=_params(("arbitrary",)),
        name=f"ssm_nb{nb}",
    )(u, x0, bblk, cblk, are, aim, d_skip.reshape(1, width))


def _state_to_lanes(s_re, s_im):
    b, gn, pn = s_re.shape
    nq = gn // SSM_GROUPS_PER_BLOCK
    st = jnp.stack([s_re.reshape(b, nq, -1), s_im.reshape(b, nq, -1)], axis=2)
    return st.reshape(b, 2 * gn * pn)


def _lanes_to_state(s, gn, pn):
    b = s.shape[0]
    nq = gn // SSM_GROUPS_PER_BLOCK
    st = s.reshape(b, nq, 2, SSM_GROUPS_PER_BLOCK * pn)
    return st[:, :, 0].reshape(b, gn, pn), st[:, :, 1].reshape(b, gn, pn)


def _mix_kernel(x_ref, att_ref, y_ref, ga_ref, gs_ref, gw_ref, gb_ref, wa_ref, ws_ref, wo_ref, out_ref):
    att = jnp.concatenate([att_ref[h] for h in range(HEADS_PER_GROUP)], axis=1)
    a_proj = jnp.dot(att.astype(BF16), wa_ref[...], preferred_element_type=F32)
    zs = jax.nn.gelu(y_ref[...])
    glu = jnp.dot(zs.astype(BF16), gw_ref[...], preferred_element_type=F32) + gb_ref[...]
    s_out = zs * jax.nn.sigmoid(glu)
    s_proj = jnp.dot(s_out.astype(BF16), ws_ref[...], preferred_element_type=F32)
    merged = jax.nn.sigmoid(ga_ref[...]) * a_proj + jax.nn.sigmoid(gs_ref[...]) * s_proj
    out_ref[...] = x_ref[...] + jnp.dot(merged.astype(BF16), wo_ref[...], preferred_element_type=F32)


def _mix(x, att, y, y_map, gates, glu_w, glu_b, wa, ws, wo, *, tm, nt):
    m, d = x.shape
    sw = glu_w.shape[0]
    row = lambda width: pl.BlockSpec((tm, width), lambda i: (i, 0))
    const = lambda a: pl.BlockSpec(a.shape, lambda i: (0, 0), pipeline_mode=pl.Buffered(1))
    glu_b = glu_b.reshape(1, sw)
    in_specs = [row(d),
                pl.BlockSpec((None, HEADS_PER_GROUP, tm, HEAD_DIM), lambda i: (i // nt, 0, i % nt, 0)),
                pl.BlockSpec((tm, sw), y_map),
                pl.BlockSpec((tm, d), lambda i: (i, 0)), pl.BlockSpec((tm, d), lambda i: (i, 1)),
                const(glu_w), const(glu_b), const(wa), const(ws), const(wo)]
    return pl.pallas_call(
        _mix_kernel,
        grid=(m // tm,),
        in_specs=in_specs,
        out_specs=row(d),
        out_shape=jax.ShapeDtypeStruct((m, d), F32),
        compiler_params=_params(("parallel",)),
        name="mix",
    )(x, att, y, gates, gates, glu_w, glu_b, wa, ws, wo)


FFN_TF = 512
PROMPT_TM_FFN = 1024
PROMPT_TM_INPROJ = 1024
PROMPT_TM_MIX = 256
SSM_STEPS = 16


def _layer_weights(p):
    w = dict(p)
    for name in ("ffn1", "ffn2"):
        w[name] = _ffn_weights(p[name + "_w_gate"], p[name + "_w_up"], p[name + "_w_down"], FFN_TF)
    w["w_in_b"] = p["w_in"].astype(BF16)
    w["qk_gain"] = jnp.concatenate([p["q_norm"].reshape(1, -1), p["k_norm"].reshape(1, -1)], axis=1)
    w["ssm"] = _ssm_prepare(p["ssm_lambda_re"], p["ssm_lambda_im"], p["ssm_log_dt"],
                            p["ssm_b_re"], p["ssm_b_im"], p["ssm_c_re"], p["ssm_c_im"])
    for name in ("glu_w", "w_attn_branch", "w_ssm_branch", "w_out"):
        w[name + "_b"] = p[name].astype(BF16)
    return w


def _prompt_layer(x, w):
    b, l, d = x.shape
    m = b * l
    nh = HEADS_PER_GROUP
    gn, pn = w["ssm_lambda_re"].shape
    x1 = _ffn(x.reshape(m, d), w["ffn1_norm"], *w["ffn1"], tm=PROMPT_TM_FFN, tf=FFN_TF)
    cos, sin = _rope_tables(jnp.arange(l, dtype=jnp.int32))
    tm_in = min(PROMPT_TM_INPROJ, l)
    qk, v, kv0, kv1, kv2, u_t, gates = _inproj_prompt(
        x1, w["mix_norm"], w["w_in_b"], w["qk_gain"], cos, sin, b=b, l=l, tm=tm_in)
    att = _attn_prompt(qk, v, b=b, l=l)
    a_re, a_im, bblk, cblk = w["ssm"]
    y_t, sfin = _ssm(u_t, jnp.zeros((b, 2 * gn * pn), F32),
                     a_re, a_im, bblk, cblk, w["ssm_d"], nb=b, steps=SSM_STEPS)
    tm = PROMPT_TM_MIX
    nt = l // tm
    x2 = _mix(x1, att, y_t, lambda i: (i % nt, i // nt), gates,
              w["glu_w_b"], w["glu_b"], w["w_attn_branch_b"], w["w_ssm_branch_b"], w["w_out_b"], tm=tm, nt=nt)
    y = _ffn(x2, w["ffn2_norm"], *w["ffn2"], tm=PROMPT_TM_FFN, tf=FFN_TF)
    new_kv = []
    for g, (window, _) in enumerate(ATT_GROUPS):
        keep = min(window, l)
        kv = (kv0, kv1, kv2)[g]
        new_kv.append(kv[:, :, (l - keep) * nh:].reshape(b, 2, keep, nh, HEAD_DIM))
    s_re, s_im = _lanes_to_state(sfin, gn, pn)
    return y.reshape(b, l, d), new_kv, (s_re, s_im)


def _sample_layer(x, pos0, caches, state, w):
    b, l, d = x.shape
    assert l == 1
    nh = HEADS_PER_GROUP
    gn, pn = w["ssm_lambda_re"].shape
    x1 = _ffn(x.reshape(b, d), w["ffn1_norm"], *w["ffn1"], tm=b, tf=FFN_TF)
    cos, sin = _rope_tables(jnp.full((b,), pos0, dtype=jnp.int32))
    qk, _, kn0, kn1, kn2, u, gates = _inproj_sample(x1, w["mix_norm"], w["w_in_b"], w["qk_gain"], cos, sin)
    q = qk[:, :ATT_WIDTH].reshape(b, N_GROUPS, nh, HEAD_DIM)
    kvnews = [kn.reshape(2, b, 1, nh, HEAD_DIM) for kn in (kn0, kn1, kn2)]
    att = _attn_decode(q, caches, kvnews)
    new_kv = [_cache_shift(caches[g], kvnews[g]) for g in range(N_GROUPS)]
    a_re, a_im, bblk, cblk = w["ssm"]
    y, sfin = _ssm(u, _state_to_lanes(*state), a_re, a_im, bblk, cblk, w["ssm_d"], nb=b, steps=1)
    att_h = jnp.transpose(att, (1, 0, 2))[None]
    x2 = _mix(x1, att_h, y, lambda i: (i, 0), gates,
              w["glu_w_b"], w["glu_b"], w["w_attn_branch_b"], w["w_ssm_branch_b"], w["w_out_b"], tm=b, nt=1)
    out = _ffn(x2, w["ffn2_norm"], *w["ffn2"], tm=b, tf=FFN_TF)
    s_re, s_im = _lanes_to_state(sfin, gn, pn)
    return out.reshape(b, l, d), new_kv, (s_re, s_im)


def kernel(x_prompt, x_sample, cache_kv_w128, cache_kv_w512, cache_kv_w2048, state_ssm_re, state_ssm_im,
           ffn1_norm, ffn1_w_gate, ffn1_w_up, ffn1_w_down, mix_norm, w_in, q_norm, k_norm,
           ssm_lambda_re, ssm_lambda_im, ssm_log_dt, ssm_b_re, ssm_b_im, ssm_c_re, ssm_c_im, ssm_d,
           glu_w, glu_b, w_attn_branch, w_ssm_branch, w_out,
           ffn2_norm, ffn2_w_gate, ffn2_w_up, ffn2_w_down):
    depth = w_in.shape[0]
    params = dict(ffn1_norm=ffn1_norm, ffn1_w_gate=ffn1_w_gate, ffn1_w_up=ffn1_w_up, ffn1_w_down=ffn1_w_down,
                  mix_norm=mix_norm, w_in=w_in, q_norm=q_norm, k_norm=k_norm,
                  ssm_lambda_re=ssm_lambda_re, ssm_lambda_im=ssm_lambda_im, ssm_log_dt=ssm_log_dt,
                  ssm_b_re=ssm_b_re, ssm_b_im=ssm_b_im, ssm_c_re=ssm_c_re, ssm_c_im=ssm_c_im, ssm_d=ssm_d,
                  glu_w=glu_w, glu_b=glu_b, w_attn_branch=w_attn_branch, w_ssm_branch=w_ssm_branch,
                  w_out=w_out, ffn2_norm=ffn2_norm, ffn2_w_gate=ffn2_w_gate, ffn2_w_up=ffn2_w_up,
                  ffn2_w_down=ffn2_w_down)
    yp, ys = x_prompt, x_sample
    new_p = [[] for _ in range(5)]
    new_s = [[] for _ in range(5)]
    for layer in range(depth):
        w = _layer_weights({k: v[layer] for k, v in params.items()})
        yp, kv_p, st_p = _prompt_layer(yp, w)
        ys, kv_s, st_s = _sample_layer(
            ys, PAST_LEN, (cache_kv_w128[layer], cache_kv_w512[layer], cache_kv_w2048[layer]),
            (state_ssm_re[layer], state_ssm_im[layer]), w)
        for i, a in enumerate(list(kv_p) + list(st_p)):
            new_p[i].append(a)
        for i, a in enumerate(list(kv_s) + list(st_s)):
            new_s[i].append(a)
    outs_p = [jnp.stack(a) for a in new_p]
    outs_s = [jnp.stack(a) for a in new_s]
    return (yp, ys, *outs_p, *outs_s)
```

```python
import functools

import jax
import jax.numpy as jnp
from jax import lax
from jax.experimental import pallas as pl
from jax.experimental.pallas import tpu as pltpu

F32 = jnp.float32
BF16 = jnp.bfloat16

HEAD_DIM = 128
HEADS_PER_GROUP = 4
GROUP_WIDTH = HEADS_PER_GROUP * HEAD_DIM
ATT_GROUPS = ((128, 1), (512, 4), (2048, 16))
N_GROUPS = len(ATT_GROUPS)
ATT_WIDTH = N_GROUPS * GROUP_WIDTH
N_BACK = 128
QBLOCK = 128
ATT_SCALE = HEAD_DIM ** -0.5
ROPE_THETA = 10000.0
RMS_EPS = 1e-6
PAST_LEN = 16384
SSM_GROUP = 16
SSM_STATE = 64
SSM_GROUPS_PER_BLOCK = 8
LANES = 128
COL_TILE = 512
VMEM_LIMIT = 56 * 1024 * 1024


def _params(semantics):
    return pltpu.CompilerParams(dimension_semantics=semantics, vmem_limit_bytes=VMEM_LIMIT)


def _rms_rows(x, gain):
    ms = jnp.mean(x * x, axis=-1, keepdims=True)
    return x * lax.rsqrt(ms + RMS_EPS) * gain


def _head(h):
    return slice(h * HEAD_DIM, (h + 1) * HEAD_DIM)


def _ffn_kernel(x_ref, g_ref, wg_ref, wu_ref, wd_ref, o_ref, h_ref, *, d_ff):
    @pl.when(pl.program_id(1) == 0)
    def _():
        x = x_ref[...]
        h_ref[...] = _rms_rows(x, g_ref[...]).astype(BF16)
        o_ref[...] = x

    h = h_ref[...]
    g = jnp.dot(h, wg_ref[...], preferred_element_type=F32)
    u = jnp.dot(h, wu_ref[...], preferred_element_type=F32)
    col = pl.program_id(1) * g.shape[1] + lax.broadcasted_iota(jnp.int32, g.shape, 1)
    a = jnp.where(col < d_ff, 0.5 * (g * jax.nn.sigmoid(g) * u), 0.0).astype(BF16)
    o_ref[...] += jnp.dot(a, wd_ref[...], preferred_element_type=F32)


def _ffn(x, gain, wg, wu, wd, *, tm, tf):
    m, d = x.shape
    d_ff = wg.shape[1]
    return pl.pallas_call(
        functools.partial(_ffn_kernel, d_ff=d_ff),
        grid=(m // tm, wd.shape[0] // tf),
        in_specs=[
            pl.BlockSpec((tm, d), lambda i, f: (i, 0)),
            pl.BlockSpec((1, d), lambda i, f: (0, 0)),
            pl.BlockSpec((d, tf), lambda i, f: (0, f)),
            pl.BlockSpec((d, tf), lambda i, f: (0, f)),
            pl.BlockSpec((tf, d), lambda i, f: (f, 0)),
        ],
        out_specs=pl.BlockSpec((tm, d), lambda i, f: (i, 0)),
        out_shape=jax.ShapeDtypeStruct((m, d), F32),
        scratch_shapes=[pltpu.VMEM((tm, d), BF16)],
        compiler_params=_params(("parallel", "arbitrary")),
        name="ffn",
    )(x, gain.reshape(1, d), wg, wu, wd)


def _ffn_weights(wg, wu, wd, tf):
    pad = (-wg.shape[1]) % tf
    return wg.astype(BF16), wu.astype(BF16), jnp.pad(wd.astype(BF16), ((0, pad), (0, 0)))


_J_K, _J_V, _J_U, _J_GATE, _J_END = 3, 6, 9, 11, 19
EPILOGUE_ROWS = 256


def _inproj_kernel(x_ref, g_ref, w_ref, qkg_ref, cos_ref, sin_ref,
                   qk_ref, v_ref, kv0_ref, kv1_ref, kv2_ref, u_ref, gate_ref, h_ref, z_ref,
                   *, head_major, cache_keep):
    j = pl.program_id(1)
    rows = h_ref.shape[0]
    nh = HEADS_PER_GROUP
    kv_refs = (kv0_ref, kv1_ref, kv2_ref)

    @pl.when(j == 0)
    def _():
        h_ref[...] = _rms_rows(x_ref[...], g_ref[...]).astype(BF16)

    def zdot():
        return jnp.dot(h_ref[...], w_ref[...], preferred_element_type=F32)

    def put_heads(ref, heads, r0, n):
        if head_major:
            for h in range(nh):
                ref[h, pl.ds(r0, n), :] = heads[h]
        else:
            ref[pl.ds(r0, n), :] = jnp.concatenate(heads, axis=1)

    def put_cache(g, heads, r0, n):
        ref, keep = kv_refs[g], cache_keep[g]
        first = 0 if keep is None else rows - keep
        lo = max(r0, first)
        if lo >= r0 + n:
            return
        part = [hd[lo - r0:] for hd in heads]
        if head_major:
            for h in range(nh):
                ref[pl.ds((lo - first) * nh + h, r0 + n - lo, stride=nh), :] = part[h]
        else:
            ref[pl.ds(lo - first, r0 + n - lo), :] = jnp.concatenate(part, axis=1)

    def norm_rope_tile(tile):
        z_src = z_ref.at[tile % 2]
        gain = qkg_ref[...]
        n = min(EPILOGUE_ROWS, rows)
        for r0 in range(0, rows, n):
            cos, sin = cos_ref[pl.ds(r0, n), :], sin_ref[pl.ds(r0, n), :]
            heads = []
            for h in range(nh):
                y = _rms_rows(z_src[pl.ds(r0, n), _head(h)], gain[:, _head(h)])
                heads.append(y * cos + pltpu.roll(y, HEAD_DIM // 2, axis=1) * sin)
            put_heads(qk_ref, heads, r0, n)
            if tile >= _J_K:
                put_cache(tile - _J_K, heads, r0, n)

    def v_tile(g):
        z = zdot()
        heads = [z[:, _head(h)] for h in range(nh)]
        put_heads(v_ref, heads, 0, rows)
        put_cache(g, heads, 0, rows)

    for t in range(_J_V + 1):
        @pl.when(j == t)
        def _(t=t):
            if t < _J_V:
                z_ref[t % 2] = zdot()
            else:
                v_tile(0)
            if t > 0:
                norm_rope_tile(t - 1)

    for g in range(1, N_GROUPS):
        @pl.when(j == _J_V + g)
        def _(g=g):
            v_tile(g)

    @pl.when((j >= _J_U) & (j < _J_GATE))
    def _():
        u_ref[...] = zdot()

    @pl.when(j >= _J_GATE)
    def _():
        gate_ref[...] = zdot()


def _kv_slot(j, g, writes=True):
    return jnp.where(writes & (j > _J_K + g + 1), 1, 0)


def _inproj_common_specs(d, tm, nt):
    return [
        pl.BlockSpec((tm, d), lambda i, j: (i, 0)),
        pl.BlockSpec((1, d), lambda i, j: (0, 0)),
        pl.BlockSpec((d, COL_TILE), lambda i, j: (0, j)),
        pl.BlockSpec((1, COL_TILE), lambda i, j: (0, jnp.clip(j - 1, 0, _J_V - 1))),
        pl.BlockSpec((tm, HEAD_DIM), lambda i, j: (i % nt, 0)),
        pl.BlockSpec((tm, HEAD_DIM), lambda i, j: (i % nt, 0)),
    ]


def _inproj_prompt(x, gain, w, qk_gain, cos, sin, *, b, l, tm):
    m, d = x.shape
    nt = l // tm
    ssm_w = (_J_GATE - _J_U) * COL_TILE
    gate_w = (_J_END - _J_GATE) * COL_TILE
    u_tiles = _J_GATE - _J_U
    nh = HEADS_PER_GROUP
    out_specs = [
        pl.BlockSpec((None, nh, tm, HEAD_DIM), lambda i, j: (i // nt, jnp.clip(j - 1, 0, _J_V - 1), i % nt, 0)),
        pl.BlockSpec((None, nh, tm, HEAD_DIM), lambda i, j: (i // nt, jnp.clip(j - _J_V, 0, N_GROUPS - 1), i % nt, 0)),
    ]
    cache_keep = tuple(None if window >= l else window for window, _ in ATT_GROUPS)
    for g, keep in enumerate(cache_keep):
        if keep is None:
            out_specs.append(pl.BlockSpec((None, None, tm * nh, HEAD_DIM),
                                          lambda i, j, g=g: (i // nt, _kv_slot(j, g), i % nt, 0)))
        else:
            assert keep <= tm
            out_specs.append(pl.BlockSpec((None, None, keep * nh, HEAD_DIM),
                                          lambda i, j, g=g: (i // nt, _kv_slot(j, g, i % nt == nt - 1), 0, 0)))
    out_specs.append(pl.BlockSpec(
        (tm, COL_TILE), lambda i, j: (i % nt, (i // nt) * u_tiles + jnp.clip(j - _J_U, 0, u_tiles - 1))))
    out_specs.append(pl.BlockSpec(
        (tm, COL_TILE), lambda i, j: (i, jnp.clip(j - _J_GATE, 0, _J_END - _J_GATE - 1))))
    out_shape = [jax.ShapeDtypeStruct((b, _J_V * nh, l, HEAD_DIM), F32),
                 jax.ShapeDtypeStruct((b, N_GROUPS * nh, l, HEAD_DIM), F32)]
    out_shape += [jax.ShapeDtypeStruct((b, 2, (l if keep is None else keep) * nh, HEAD_DIM), F32)
                  for keep in cache_keep]
    out_shape += [jax.ShapeDtypeStruct((l, b * ssm_w), F32), jax.ShapeDtypeStruct((m, gate_w), F32)]
    return pl.pallas_call(
        functools.partial(_inproj_kernel, head_major=True, cache_keep=cache_keep),
        grid=(m // tm, _J_END),
        in_specs=_inproj_common_specs(d, tm, nt),
        out_specs=out_specs,
        out_shape=out_shape,
        scratch_shapes=[pltpu.VMEM((tm, d), BF16), pltpu.VMEM((2, tm, COL_TILE), F32)],
        compiler_params=_params(("arbitrary", "arbitrary")),
        name="inproj_prompt",
    )(x, gain.reshape(1, d), w, qk_gain, cos, sin)


def _inproj_sample(x, gain, w, qk_gain, cos, sin):
    m, d = x.shape
    u_tiles = _J_GATE - _J_U
    out_specs = [pl.BlockSpec((m, COL_TILE), lambda i, j: (0, jnp.clip(j - 1, 0, _J_V - 1))),
                 pl.BlockSpec((m, COL_TILE), lambda i, j: (0, jnp.clip(j - _J_V, 0, N_GROUPS - 1)))]
    for g in range(N_GROUPS):
        out_specs.append(pl.BlockSpec((None, m, COL_TILE), lambda i, j, g=g: (_kv_slot(j, g), 0, 0)))
    out_specs.append(pl.BlockSpec((m, COL_TILE), lambda i, j: (0, jnp.clip(j - _J_U, 0, u_tiles - 1))))
    out_specs.append(pl.BlockSpec((m, COL_TILE), lambda i, j: (0, jnp.clip(j - _J_GATE, 0, _J_END - _J_GATE - 1))))
    out_shape = [jax.ShapeDtypeStruct((m, _J_V * COL_TILE), F32), jax.ShapeDtypeStruct((m, N_GROUPS * COL_TILE), F32)]
    out_shape += [jax.ShapeDtypeStruct((2, m, GROUP_WIDTH), F32)] * N_GROUPS
    out_shape += [jax.ShapeDtypeStruct((m, u_tiles * COL_TILE), F32),
                  jax.ShapeDtypeStruct((m, (_J_END - _J_GATE) * COL_TILE), F32)]
    return pl.pallas_call(
        functools.partial(_inproj_kernel, head_major=False, cache_keep=(None,) * N_GROUPS),
        grid=(1, _J_END),
        in_specs=_inproj_common_specs(d, m, 1),
        out_specs=out_specs,
        out_shape=out_shape,
        scratch_shapes=[pltpu.VMEM((m, d), BF16), pltpu.VMEM((2, m, COL_TILE), F32)],
        compiler_params=_params(("arbitrary", "arbitrary")),
        name="inproj_sample",
    )(x, gain.reshape(1, d), w, qk_gain, cos, sin)


def _rope_tables(pos):
    half = HEAD_DIM // 2
    inv = jnp.power(ROPE_THETA, -jnp.arange(half, dtype=F32) * (2.0 / HEAD_DIM))
    ang = pos.astype(F32)[:, None] * inv[None, :]
    cos, sin = jnp.cos(ang), jnp.sin(ang)
    return jnp.concatenate([cos, cos], axis=1), jnp.concatenate([-sin, sin], axis=1)


ATTN_UNROLL = 16


def _attn_prompt_kernel(*refs):
    qkv = refs[:3 * N_GROUPS]
    o_ref = refs[3 * N_GROUPS]
    stats = (refs[3 * N_GROUPS + 1:3 * N_GROUPS + 4], refs[3 * N_GROUPS + 4:3 * N_GROUPS + 7])
    seq = o_ref.shape[0]
    row = lax.broadcasted_iota(jnp.int32, (QBLOCK, QBLOCK), 0)
    col = lax.broadcasted_iota(jnp.int32, (QBLOCK, QBLOCK), 1)
    cur_mask = col <= row
    prev_mask = col >= row
    nt_dims = (((1,), (1,)), ((), ()))
    full = (QBLOCK, HEAD_DIM)
    ones = jnp.ones(full, BF16)

    for g, (_, dil) in enumerate(ATT_GROUPS):
        q_ref, k_ref, v_ref = qkv[3 * g:3 * g + 3]
        nblk = seq // dil // QBLOCK
        span = QBLOCK * dil

        def rows_at(start, dil=dil):
            return pl.ds(start, QBLOCK) if dil == 1 else pl.ds(start, QBLOCK, stride=dil)

        def block(r, n, q_ref=q_ref, k_ref=k_ref, v_ref=v_ref, nblk=nblk, span=span, rows_at=rows_at):
            start = r + n * span
            rows = rows_at(start)
            q = q_ref[rows, :].astype(BF16)
            k = k_ref[rows, :].astype(BF16)
            v = jnp.concatenate([v_ref[rows, :].astype(BF16), ones], axis=1)
            if nblk > 1:
                prows = rows_at(jnp.maximum(start - span, r))
                k = jnp.concatenate([k, k_ref[prows, :].astype(BF16)], axis=0)
                vp = jnp.concatenate([v_ref[prows, :].astype(BF16), ones], axis=1)
                v = jnp.concatenate([v, vp], axis=0)
            s = lax.dot_general(q, k, nt_dims, preferred_element_type=F32) * ATT_SCALE
            if nblk > 1:
                prev_pen = jnp.where(n > 0, 0.0, -jnp.inf)
                s = jnp.concatenate([jnp.where(cur_mask, s[:, :QBLOCK], -jnp.inf),
                                     jnp.where(prev_mask, s[:, QBLOCK:], -jnp.inf) + prev_pen], axis=1)
                mx = jnp.max(jnp.maximum(s[:, :QBLOCK], s[:, QBLOCK:]), axis=1, keepdims=True)
            else:
                s = jnp.where(cur_mask, s, -jnp.inf)
                mx = jnp.max(s, axis=1, keepdims=True)
            p = jnp.exp(s - mx).astype(BF16)
            acc = jnp.dot(p, v, preferred_element_type=F32)
            return rows, acc[:, :HEAD_DIM], jnp.broadcast_to(mx, full), acc[:, HEAD_DIM:]

        src = stats[(g - 1) % 2] if g > 0 else None
        dst = stats[g % 2] if g < N_GROUPS - 1 else None

        def merge(rows, acc, mb, lb, src=src, dst=dst):
            if src is not None:
                m_old = src[1][rows, :]
                m_new = jnp.maximum(m_old, mb)
                a_old, a_new = jnp.exp(m_old - m_new), jnp.exp(mb - m_new)
                acc = a_old * src[0][rows, :] + a_new * acc
                lb = a_old * src[2][rows, :] + a_new * lb
                mb = m_new
            if dst is not None:
                dst[0][rows, :] = acc
                dst[1][rows, :] = mb
                dst[2][rows, :] = lb
            else:
                o_ref[rows, :] = acc / lb

        def body(i, carry, dil=dil, block=block, merge=merge):
            results = []
            for d in range(ATTN_UNROLL):
                idx = i * ATTN_UNROLL + d
                results.append(block(idx % dil, idx // dil))
            for res in results:
                merge(*res)
            return carry

        lax.fori_loop(0, dil * nblk // ATTN_UNROLL, body, 0)


def _attn_prompt(qk, v, *, b, l):
    nh = HEADS_PER_GROUP
    in_specs, args = [], []
    for g in range(N_GROUPS):
        for part in range(3):
            in_specs.append(pl.BlockSpec(
                (None, None, l, HEAD_DIM),
                lambda bi, hs, g=g, part=part: (bi, (part % 2) * N_GROUPS * nh + g * nh + hs, 0, 0)))
            args.append(v if part == 2 else qk)
    return pl.pallas_call(
        _attn_prompt_kernel,
        grid=(b, nh),
        in_specs=in_specs,
        out_specs=pl.BlockSpec((None, None, l, HEAD_DIM), lambda bi, hs: (bi, hs, 0, 0)),
        out_shape=jax.ShapeDtypeStruct((b, nh, l, HEAD_DIM), F32),
        scratch_shapes=[pltpu.VMEM((l, HEAD_DIM), F32)] * 6,
        compiler_params=_params(("parallel", "parallel")),
        name="attn_prompt",
    )(*args)


def _attn_decode_kernel(q_ref, *refs):
    o_ref = refs[-1]
    outs, lses = [], []
    for g in range(N_GROUPS):
        kc_ref, vc_ref, kn_ref, vn_ref = refs[4 * g:4 * g + 4]
        qg = q_ref[g][None]
        s = jnp.sum(kc_ref[...] * qg, axis=-1, keepdims=True) * ATT_SCALE
        s_new = jnp.sum(kn_ref[...] * qg, axis=-1, keepdims=True) * ATT_SCALE
        mx = jnp.maximum(jnp.max(s, axis=0, keepdims=True), s_new)
        p = jnp.exp(s - mx)
        p_new = jnp.exp(s_new - mx)
        den = jnp.sum(p, axis=0, keepdims=True) + p_new
        acc = jnp.sum(p * vc_ref[...], axis=0, keepdims=True) + p_new * vn_ref[...]
        outs.append(acc / den)
        lses.append(mx + jnp.log(den))
    mx = functools.reduce(jnp.maximum, lses)
    ws = [jnp.exp(lse - mx) for lse in lses]
    num = functools.reduce(lambda a, c: a + c, [w * o for w, o in zip(ws, outs)])
    o_ref[...] = (num / functools.reduce(lambda a, c: a + c, ws))[0]


def _attn_decode(q, caches, kvnews):
    b = q.shape[0]
    nh = HEADS_PER_GROUP
    in_specs = [pl.BlockSpec((None, N_GROUPS, nh, HEAD_DIM), lambda i: (i, 0, 0, 0))]
    args = [q]
    for g, (window, dil) in enumerate(ATT_GROUPS):
        cache = caches[g]
        assert cache.shape[2] == window and window == N_BACK * dil
        cv = cache.reshape(b, 2, N_BACK, dil, nh, HEAD_DIM)
        in_specs += [
            pl.BlockSpec((None, None, N_BACK, None, nh, HEAD_DIM), lambda i: (i, 0, 0, 0, 0, 0)),
            pl.BlockSpec((None, None, N_BACK, None, nh, HEAD_DIM), lambda i: (i, 1, 0, 0, 0, 0)),
            pl.BlockSpec((None, None, 1, nh, HEAD_DIM), lambda i: (0, i, 0, 0, 0)),
            pl.BlockSpec((None, None, 1, nh, HEAD_DIM), lambda i: (1, i, 0, 0, 0)),
        ]
        args += [cv, cv, kvnews[g], kvnews[g]]
    return pl.pallas_call(
        _attn_decode_kernel,
        grid=(b,),
        in_specs=in_specs,
        out_specs=pl.BlockSpec((None, nh, HEAD_DIM), lambda i: (i, 0, 0)),
        out_shape=jax.ShapeDtypeStruct((b, nh, HEAD_DIM), F32),
        compiler_params=_params(("parallel",)),
        name="attn_decode",
    )(*args)


def _cache_shift_kernel(c_ref, new_ref, o_ref):
    w = c_ref.shape[0]
    o_ref[pl.ds(0, w - 1)] = c_ref[pl.ds(1, w - 1)]
    o_ref[pl.ds(w - 1, 1)] = new_ref[...]


def _cache_shift(cache, kvnew):
    b, _, w, nh, e = cache.shape
    return pl.pallas_call(
        _cache_shift_kernel,
        grid=(b, 2),
        in_specs=[
            pl.BlockSpec((None, None, w, nh, e), lambda i, s: (i, s, 0, 0, 0)),
            pl.BlockSpec((None, None, 1, nh, e), lambda i, s: (s, i, 0, 0, 0)),
        ],
        out_specs=pl.BlockSpec((None, None, w, nh, e), lambda i, s: (i, s, 0, 0, 0)),
        out_shape=jax.ShapeDtypeStruct(cache.shape, F32),
        compiler_params=_params(("parallel", "parallel")),
        name=f"cache_shift_w{w}",
    )(cache, kvnew)


def _ssm_prep_kernel(lr_ref, li_ref, ldt_ref, br_ref, bi_ref, cim_ref,
                     ar_ref, ai_ref, bbr_ref, bbi_ref, ncim_ref):
    lr, li = lr_ref[...], li_ref[...]
    dt = jnp.exp(ldt_ref[...])
    mag = jnp.exp(lr * dt)
    ar = mag * jnp.cos(li * dt)
    ai = mag * jnp.sin(li * dt)
    den = lr * lr + li * li
    fr = ((ar - 1.0) * lr + ai * li) / den
    fi = (ai * lr - (ar - 1.0) * li) / den
    br, bi = br_ref[...], bi_ref[...]
    ar_ref[...] = ar
    ai_ref[...] = ai
    bbr_ref[...] = fr * br - fi * bi
    bbi_ref[...] = fr * bi + fi * br
    ncim_ref[...] = -cim_ref[...]


def _ssm_prepare(lam_re, lam_im, log_dt, b_re, b_im, c_re, c_im):
    gn, pn = lam_re.shape
    cn = SSM_GROUP
    rows = gn * cn
    rep = lambda a: jnp.broadcast_to(a[:, None, :], (gn, cn, pn)).reshape(rows, pn)
    tr = lambda a: jnp.transpose(a, (0, 2, 1)).reshape(rows, pn)
    ldt = jnp.broadcast_to(log_dt[:, None, None], (gn, cn, pn)).reshape(rows, pn)
    shp = jax.ShapeDtypeStruct((rows, pn), F32)
    ar, ai, bbr, bbi, ncim = pl.pallas_call(
        _ssm_prep_kernel, out_shape=[shp] * 5, name="ssm_prep",
    )(rep(lam_re), rep(lam_im), ldt, tr(b_re), tr(b_im), c_im.reshape(rows, pn))
    nq, gb = gn // SSM_GROUPS_PER_BLOCK, SSM_GROUPS_PER_BLOCK
    diag = (jnp.arange(gb)[:, None, None, None] == jnp.arange(gb)[None, None, :, None])
    spread = lambda a: jnp.where(diag, a.reshape(2, nq, gb, cn, 1, pn), 0.0)
    bb = spread(jnp.stack([bbr, bbi]))
    bblk = jnp.transpose(bb, (1, 2, 3, 0, 4, 5)).reshape(nq, gb * cn, 2 * gb * pn).astype(BF16)
    cc = spread(jnp.stack([c_re.reshape(rows, pn), ncim]))
    cblk = jnp.transpose(cc, (1, 0, 2, 5, 4, 3)).reshape(nq, 2 * gb * pn, gb * cn).astype(BF16)
    a_re = ar.reshape(gn, cn, pn)[:, 0, :].reshape(gn * pn // LANES, LANES)
    a_im = ai.reshape(gn, cn, pn)[:, 0, :].reshape(gn * pn // LANES, LANES)
    return a_re, a_im, bblk, cblk


def _ssm_kernel(u_ref, x0_ref, bblk_ref, cblk_ref, are_ref, aim_ref, d_ref,
                y_ref, sfin_ref, ut_ref, yt_ref, bu_ref, xs_ref, st_ref, *, nb, steps):
    c = pl.program_id(0)
    nq = bblk_ref.shape[0]
    tiles_per_q = 2 * SSM_GROUPS_PER_BLOCK * SSM_STATE // LANES
    half = tiles_per_q // 2

    @pl.when(c == 0)
    def _():
        for j in range(nq * tiles_per_q):
            st_ref[j] = x0_ref[:, j * LANES:(j + 1) * LANES]

    width = nq * LANES
    for s in range(nb if steps > 1 else 1):
        for q in range(nq):
            if steps > 1:
                ut_ref[q, pl.ds(s, steps, stride=nb), :] = u_ref[:, s * width + q * LANES:s * width + (q + 1) * LANES]
            else:
                ut_ref[q] = u_ref[:, q * LANES:(q + 1) * LANES]
    for q in range(nq):
        res = jnp.dot(ut_ref[q].astype(BF16), bblk_ref[q], preferred_element_type=F32)
        for j in range(tiles_per_q):
            bu_ref[q * tiles_per_q + j] = res[:, j * LANES:(j + 1) * LANES]

    def scan_block(q, carry):
        for lt in range(half):
            jr = q * tiles_per_q + lt
            ji = jr + half
            ar = are_ref[q * half + lt]
            ai = aim_ref[q * half + lt]
            xr, xi = st_ref[jr], st_ref[ji]
            for t in range(steps):
                rows = pl.ds(t * nb, nb)
                xr, xi = (ar * xr - ai * xi + bu_ref[jr, rows, :],
                          ar * xi + ai * xr + bu_ref[ji, rows, :])
                xs_ref[jr, rows, :] = xr.astype(BF16)
                xs_ref[ji, rows, :] = xi.astype(BF16)
            st_ref[jr] = xr
            st_ref[ji] = xi
        return carry

    for q in range(nq):
        scan_block(q, 0)

    for q in range(nq):
        xq = jnp.concatenate([xs_ref[q * tiles_per_q + j] for j in range(tiles_per_q)], axis=1)
        yq = jnp.dot(xq, cblk_ref[q], preferred_element_type=F32)
        sl = slice(q * LANES, (q + 1) * LANES)
        yq = yq + d_ref[:, sl] * ut_ref[q]
        if steps > 1:
            yt_ref[q] = yq
        else:
            y_ref[:, sl] = yq
    if steps > 1:
        for s in range(nb):
            for q in range(nq):
                y_ref[:, s * width + q * LANES:s * width + (q + 1) * LANES] = yt_ref[q, pl.ds(s, steps, stride=nb), :]

    @pl.when(c == pl.num_programs(0) - 1)
    def _():
        for j in range(nq * tiles_per_q):
            sfin_ref[:, j * LANES:(j + 1) * LANES] = st_ref[j]


def _ssm(u, x0, a_re, a_im, bblk, cblk, d_skip, *, nb, steps):
    width = d_skip.shape[0]
    nsteps = u.shape[0] // steps if steps > 1 else 1
    blk = (steps, nb * width) if steps > 1 else (nb, width)
    rows = steps * nb
    ntile = a_re.shape[0]
    nstate = 2 * ntile * LANES
    are = jnp.broadcast_to(a_re[:, None, :], (ntile, nb, LANES))
    aim = jnp.broadcast_to(a_im[:, None, :], (ntile, nb, LANES))
    const = lambda shape: pl.BlockSpec(shape, lambda c: (0,) * len(shape))
    return pl.pallas_call(
        functools.partial(_ssm_kernel, nb=nb, steps=steps),
        grid=(nsteps,),
        in_specs=[
            pl.BlockSpec(blk, lambda c: (c, 0)),
            const((nb, nstate)),
            const(bblk.shape),
            const(cblk.shape),
            const(are.shape),
            const(aim.shape),
            const((1, width)),
        ],
        out_specs=[pl.BlockSpec(blk, lambda c: (c, 0)), const((nb, nstate))],
        out_shape=[jax.ShapeDtypeStruct(u.shape, F32), jax.ShapeDtypeStruct((nb, nstate), F32)],
        scratch_shapes=[
            pltpu.VMEM((width // LANES, rows, LANES), F32),
            pltpu.VMEM((width // LANES, rows, LANES), F32),
            pltpu.VMEM((2 * ntile, rows, LANES), F32),
            pltpu.VMEM((2 * ntile, rows, LANES), BF16),
            pltpu.VMEM((2 * ntile, nb, LANES), F32),
        ],
        compiler_params=_params(("arbitrary",)),
        name=f"ssm_nb{nb}",
    )(u, x0, bblk, cblk, are, aim, d_skip.reshape(1, width))


def _state_to_lanes(s_re, s_im):
    b, gn, pn = s_re.shape
    nq = gn // SSM_GROUPS_PER_BLOCK
    st = jnp.stack([s_re.reshape(b, nq, -1), s_im.reshape(b, nq, -1)], axis=2)
    return st.reshape(b, 2 * gn * pn)


def _lanes_to_state(s, gn, pn):
    b = s.shape[0]
    nq = gn // SSM_GROUPS_PER_BLOCK
    st = s.reshape(b, nq, 2, SSM_GROUPS_PER_BLOCK * pn)
    return st[:, :, 0].reshape(b, gn, pn), st[:, :, 1].reshape(b, gn, pn)


def _mix_kernel(x_ref, att_ref, y_ref, ga_ref, gs_ref, gw_ref, gb_ref, wa_ref, ws_ref, wo_ref, out_ref):
    att = jnp.concatenate([att_ref[h] for h in range(HEADS_PER_GROUP)], axis=1)
    a_proj = jnp.dot(att.astype(BF16), wa_ref[...], preferred_element_type=F32)
    zs = jax.nn.gelu(y_ref[...])
    glu = jnp.dot(zs.astype(BF16), gw_ref[...], preferred_element_type=F32) + gb_ref[...]
    s_out = zs * jax.nn.sigmoid(glu)
    s_proj = jnp.dot(s_out.astype(BF16), ws_ref[...], preferred_element_type=F32)
    merged = jax.nn.sigmoid(ga_ref[...]) * a_proj + jax.nn.sigmoid(gs_ref[...]) * s_proj
    out_ref[...] = x_ref[...] + jnp.dot(merged.astype(BF16), wo_ref[...], preferred_element_type=F32)


def _mix(x, att, y, y_map, gates, glu_w, glu_b, wa, ws, wo, *, tm, nt):
    m, d = x.shape
    sw = glu_w.shape[0]
    row = lambda width: pl.BlockSpec((tm, width), lambda i: (i, 0))
    const = lambda a: pl.BlockSpec(a.shape, lambda i: (0, 0), pipeline_mode=pl.Buffered(1))
    glu_b = glu_b.reshape(1, sw)
    in_specs = [row(d),
                pl.BlockSpec((None, HEADS_PER_GROUP, tm, HEAD_DIM), lambda i: (i // nt, 0, i % nt, 0)),
                pl.BlockSpec((tm, sw), y_map),
                pl.BlockSpec((tm, d), lambda i: (i, 0)), pl.BlockSpec((tm, d), lambda i: (i, 1)),
                const(glu_w), const(glu_b), const(wa), const(ws), const(wo)]
    return pl.pallas_call(
        _mix_kernel,
        grid=(m // tm,),
        in_specs=in_specs,
        out_specs=row(d),
        out_shape=jax.ShapeDtypeStruct((m, d), F32),
        compiler_params=_params(("parallel",)),
        name="mix",
    )(x, att, y, gates, gates, glu_w, glu_b, wa, ws, wo)


FFN_TF = 512
PROMPT_TM_FFN = 1024
PROMPT_TM_INPROJ = 1024
PROMPT_TM_MIX = 256
SSM_STEPS = 16


def _layer_weights(p):
    w = dict(p)
    for name in ("ffn1", "ffn2"):
        w[name] = _ffn_weights(p[name + "_w_gate"], p[name + "_w_up"], p[name + "_w_down"], FFN_TF)
    w["w_in_b"] = p["w_in"].astype(BF16)
    w["qk_gain"] = jnp.concatenate([p["q_norm"].reshape(1, -1), p["k_norm"].reshape(1, -1)], axis=1)
    w["ssm"] = _ssm_prepare(p["ssm_lambda_re"], p["ssm_lambda_im"], p["ssm_log_dt"],
                            p["ssm_b_re"], p["ssm_b_im"], p["ssm_c_re"], p["ssm_c_im"])
    for name in ("glu_w", "w_attn_branch", "w_ssm_branch", "w_out"):
        w[name + "_b"] = p[name].astype(BF16)
    return w


def _prompt_layer(x, w):
    b, l, d = x.shape
    m = b * l
    nh = HEADS_PER_GROUP
    gn, pn = w["ssm_lambda_re"].shape
    x1 = _ffn(x.reshape(m, d), w["ffn1_norm"], *w["ffn1"], tm=PROMPT_TM_FFN, tf=FFN_TF)
    cos, sin = _rope_tables(jnp.arange(l, dtype=jnp.int32))
    tm_in = min(PROMPT_TM_INPROJ, l)
    qk, v, kv0, kv1, kv2, u_t, gates = _inproj_prompt(
        x1, w["mix_norm"], w["w_in_b"], w["qk_gain"], cos, sin, b=b, l=l, tm=tm_in)
    att = _attn_prompt(qk, v, b=b, l=l)
    a_re, a_im, bblk, cblk = w["ssm"]
    y_t, sfin = _ssm(u_t, jnp.zeros((b, 2 * gn * pn), F32),
                     a_re, a_im, bblk, cblk, w["ssm_d"], nb=b, steps=SSM_STEPS)
    tm = PROMPT_TM_MIX
    nt = l // tm
    x2 = _mix(x1, att, y_t, lambda i: (i % nt, i // nt), gates,
              w["glu_w_b"], w["glu_b"], w["w_attn_branch_b"], w["w_ssm_branch_b"], w["w_out_b"], tm=tm, nt=nt)
    y = _ffn(x2, w["ffn2_norm"], *w["ffn2"], tm=PROMPT_TM_FFN, tf=FFN_TF)
    new_kv = []
    for g, (window, _) in enumerate(ATT_GROUPS):
        new_kv.append((kv0, kv1, kv2)[g].reshape(b, 2, min(window, l), nh, HEAD_DIM))
    s_re, s_im = _lanes_to_state(sfin, gn, pn)
    return y.reshape(b, l, d), new_kv, (s_re, s_im)


def _sample_layer(x, pos0, caches, state, w):
    b, l, d = x.shape
    assert l == 1
    nh = HEADS_PER_GROUP
    gn, pn = w["ssm_lambda_re"].shape
    x1 = _ffn(x.reshape(b, d), w["ffn1_norm"], *w["ffn1"], tm=b, tf=FFN_TF)
    cos, sin = _rope_tables(jnp.full((b,), pos0, dtype=jnp.int32))
    qk, _, kn0, kn1, kn2, u, gates = _inproj_sample(x1, w["mix_norm"], w["w_in_b"], w["qk_gain"], cos, sin)
    q = qk[:, :ATT_WIDTH].reshape(b, N_GROUPS, nh, HEAD_DIM)
    kvnews = [kn.reshape(2, b, 1, nh, HEAD_DIM) for kn in (kn0, kn1, kn2)]
    att = _attn_decode(q, caches, kvnews)
    new_kv = [_cache_shift(caches[g], kvnews[g]) for g in range(N_GROUPS)]
    a_re, a_im, bblk, cblk = w["ssm"]
    y, sfin = _ssm(u, _state_to_lanes(*state), a_re, a_im, bblk, cblk, w["ssm_d"], nb=b, steps=1)
    att_h = jnp.transpose(att, (1, 0, 2))[None]
    x2 = _mix(x1, att_h, y, lambda i: (i, 0), gates,
              w["glu_w_b"], w["glu_b"], w["w_attn_branch_b"], w["w_ssm_branch_b"], w["w_out_b"], tm=b, nt=1)
    out = _ffn(x2, w["ffn2_norm"], *w["ffn2"], tm=b, tf=FFN_TF)
    s_re, s_im = _lanes_to_state(sfin, gn, pn)
    return out.reshape(b, l, d), new_kv, (s_re, s_im)


def kernel(x_prompt, x_sample, cache_kv_w128, cache_kv_w512, cache_kv_w2048, state_ssm_re, state_ssm_im,
           ffn1_norm, ffn1_w_gate, ffn1_w_up, ffn1_w_down, mix_norm, w_in, q_norm, k_norm,
           ssm_lambda_re, ssm_lambda_im, ssm_log_dt, ssm_b_re, ssm_b_im, ssm_c_re, ssm_c_im, ssm_d,
           glu_w, glu_b, w_attn_branch, w_ssm_branch, w_out,
           ffn2_norm, ffn2_w_gate, ffn2_w_up, ffn2_w_down):
    depth = w_in.shape[0]
    params = dict(ffn1_norm=ffn1_norm, ffn1_w_gate=ffn1_w_gate, ffn1_w_up=ffn1_w_up, ffn1_w_down=ffn1_w_down,
                  mix_norm=mix_norm, w_in=w_in, q_norm=q_norm, k_norm=k_norm,
                  ssm_lambda_re=ssm_lambda_re, ssm_lambda_im=ssm_lambda_im, ssm_log_dt=ssm_log_dt,
                  ssm_b_re=ssm_b_re, ssm_b_im=ssm_b_im, ssm_c_re=ssm_c_re, ssm_c_im=ssm_c_im, ssm_d=ssm_d,
                  glu_w=glu_w, glu_b=glu_b, w_attn_branch=w_attn_branch, w_ssm_branch=w_ssm_branch,
                  w_out=w_out, ffn2_norm=ffn2_norm, ffn2_w_gate=ffn2_w_gate, ffn2_w_up=ffn2_w_up,
                  ffn2_w_down=ffn2_w_down)
    yp, ys = x_prompt, x_sample
    new_p = [[] for _ in range(5)]
    new_s = [[] for _ in range(5)]
    for layer in range(depth):
        w = _layer_weights({k: v[layer] for k, v in params.items()})
        yp, kv_p, st_p = _prompt_layer(yp, w)
        ys, kv_s, st_s = _sample_layer(
            ys, PAST_LEN, (cache_kv_w128[layer], cache_kv_w512[layer], cache_kv_w2048[layer]),
            (state_ssm_re[layer], state_ssm_im[layer]), w)
        for i, a in enumerate(list(kv_p) + list(st_p)):
            new_p[i].append(a)
        for i, a in enumerate(list(kv_s) + list(st_s)):
            new_s[i].append(a)
    outs_p = [jnp.stack(a) for a in new_p]
    outs_s = [jnp.stack(a) for a in new_s]
    return (yp, ys, *outs_p, *outs_s)
```

```python
import functools

import jax
import jax.numpy as jnp
from jax import lax
from jax.experimental import pallas as pl
from jax.experimental.pallas import tpu as pltpu

F32 = jnp.float32
BF16 = jnp.bfloat16

HEAD_DIM = 128
HEADS_PER_GROUP = 4
GROUP_WIDTH = HEADS_PER_GROUP * HEAD_DIM
ATT_GROUPS = ((128, 1), (512, 4), (2048, 16))
N_GROUPS = len(ATT_GROUPS)
ATT_WIDTH = N_GROUPS * GROUP_WIDTH
N_BACK = 128
QBLOCK = 128
ATT_SCALE = HEAD_DIM ** -0.5
ROPE_THETA = 10000.0
RMS_EPS = 1e-6
PAST_LEN = 16384
SSM_GROUP = 16
SSM_STATE = 64
SSM_GROUPS_PER_BLOCK = 8
LANES = 128
COL_TILE = 512
VMEM_LIMIT = 56 * 1024 * 1024


def _params(semantics):
    return pltpu.CompilerParams(dimension_semantics=semantics, vmem_limit_bytes=VMEM_LIMIT)


def _rms_rows(x, gain):
    ms = jnp.mean(x * x, axis=-1, keepdims=True)
    return x * lax.rsqrt(ms + RMS_EPS) * gain


def _head(h):
    return slice(h * HEAD_DIM, (h + 1) * HEAD_DIM)


def _ffn_kernel(x_ref, g_ref, wg_ref, wu_ref, wd_ref, o_ref, h_ref, *, d_ff):
    def accumulate():
        h = h_ref[...]
        g = jnp.dot(h, wg_ref[...], preferred_element_type=F32)
        u = jnp.dot(h, wu_ref[...], preferred_element_type=F32)
        col = pl.program_id(1) * g.shape[1] + lax.broadcasted_iota(jnp.int32, g.shape, 1)
        a = jnp.where(col < d_ff, 0.5 * (g * jax.nn.sigmoid(g) * u), 0.0).astype(BF16)
        o_ref[...] += jnp.dot(a, wd_ref[...], preferred_element_type=F32)

    @pl.when(pl.program_id(1) == 0)
    def _():
        x = x_ref[...]
        h_ref[...] = _rms_rows(x, g_ref[...]).astype(BF16)
        o_ref[...] = x
        accumulate()

    @pl.when(pl.program_id(1) > 0)
    def _():
        accumulate()


def _ffn(x, gain, wg, wu, wd, *, tm, tf):
    m, d = x.shape
    d_ff = wg.shape[1]
    return pl.pallas_call(
        functools.partial(_ffn_kernel, d_ff=d_ff),
        grid=(m // tm, wd.shape[0] // tf),
        in_specs=[
            pl.BlockSpec((tm, d), lambda i, f: (i, 0)),
            pl.BlockSpec((1, d), lambda i, f: (0, 0)),
            pl.BlockSpec((d, tf), lambda i, f: (0, f)),
            pl.BlockSpec((d, tf), lambda i, f: (0, f)),
            pl.BlockSpec((tf, d), lambda i, f: (f, 0)),
        ],
        out_specs=pl.BlockSpec((tm, d), lambda i, f: (i, 0)),
        out_shape=jax.ShapeDtypeStruct((m, d), F32),
        scratch_shapes=[pltpu.VMEM((tm, d), BF16)],
        compiler_params=_params(("parallel", "arbitrary")),
        name="ffn",
    )(x, gain.reshape(1, d), wg, wu, wd)


def _ffn_weights(wg, wu, wd, tf):
    pad = (-wg.shape[1]) % tf
    return wg.astype(BF16), wu.astype(BF16), jnp.pad(wd.astype(BF16), ((0, pad), (0, 0)))


_J_K, _J_V, _J_U, _J_GATE, _J_END = 3, 6, 9, 11, 19
EPILOGUE_ROWS = 256


def _inproj_kernel(x_ref, g_ref, w_ref, qkg_ref, cos_ref, sin_ref,
                   qk_ref, v_ref, kv0_ref, kv1_ref, kv2_ref, u_ref, gate_ref, h_ref, z_ref,
                   *, head_major, cache_keep):
    j = pl.program_id(1)
    rows = h_ref.shape[0]
    nh = HEADS_PER_GROUP
    kv_refs = (kv0_ref, kv1_ref, kv2_ref)

    @pl.when(j == 0)
    def _():
        h_ref[...] = _rms_rows(x_ref[...], g_ref[...]).astype(BF16)

    def zdot():
        return jnp.dot(h_ref[...], w_ref[...], preferred_element_type=F32)

    def put_heads(ref, heads, r0, n):
        if head_major:
            for h in range(nh):
                ref[h, pl.ds(r0, n), :] = heads[h]
        else:
            ref[pl.ds(r0, n), :] = jnp.concatenate(heads, axis=1)

    def put_cache(g, heads, r0, n):
        ref, keep = kv_refs[g], cache_keep[g]
        first = 0 if keep is None else rows - keep
        lo = max(r0, first)
        if lo >= r0 + n:
            return
        part = [hd[lo - r0:] for hd in heads]
        if head_major:
            for h in range(nh):
                ref[pl.ds((lo - first) * nh + h, r0 + n - lo, stride=nh), :] = part[h]
        else:
            ref[pl.ds(lo - first, r0 + n - lo), :] = jnp.concatenate(part, axis=1)

    def norm_rope_tile(tile):
        z_src = z_ref.at[tile % 2]
        gain = qkg_ref[...]
        n = min(EPILOGUE_ROWS, rows)
        for r0 in range(0, rows, n):
            cos, sin = cos_ref[pl.ds(r0, n), :], sin_ref[pl.ds(r0, n), :]
            heads = []
            for h in range(nh):
                y = _rms_rows(z_src[pl.ds(r0, n), _head(h)], gain[:, _head(h)])
                heads.append(y * cos + pltpu.roll(y, HEAD_DIM // 2, axis=1) * sin)
            put_heads(qk_ref, heads, r0, n)
            if tile >= _J_K:
                put_cache(tile - _J_K, heads, r0, n)

    def v_tile(g):
        z = zdot()
        heads = [z[:, _head(h)] for h in range(nh)]
        put_heads(v_ref, heads, 0, rows)
        put_cache(g, heads, 0, rows)

    for t in range(_J_V + 1):
        @pl.when(j == t)
        def _(t=t):
            if t < _J_V:
                z_ref[t % 2] = zdot()
            else:
                v_tile(0)
            if t > 0:
                norm_rope_tile(t - 1)

    for g in range(1, N_GROUPS):
        @pl.when(j == _J_V + g)
        def _(g=g):
            v_tile(g)

    @pl.when((j >= _J_U) & (j < _J_GATE))
    def _():
        u_ref[...] = zdot()

    @pl.when(j >= _J_GATE)
    def _():
        gate_ref[...] = zdot()


def _kv_slot(j, g, writes=True):
    return jnp.where(writes & (j > _J_K + g + 1), 1, 0)


def _inproj_common_specs(d, tm, nt):
    return [
        pl.BlockSpec((tm, d), lambda i, j: (i, 0)),
        pl.BlockSpec((1, d), lambda i, j: (0, 0)),
        pl.BlockSpec((None, d, COL_TILE), lambda i, j: (j, 0, 0)),
        pl.BlockSpec((1, COL_TILE), lambda i, j: (0, jnp.clip(j - 1, 0, _J_V - 1))),
        pl.BlockSpec((tm, HEAD_DIM), lambda i, j: (i % nt, 0)),
        pl.BlockSpec((tm, HEAD_DIM), lambda i, j: (i % nt, 0)),
    ]


def _inproj_prompt(x, gain, w, qk_gain, cos, sin, *, b, l, tm):
    m, d = x.shape
    nt = l // tm
    ssm_w = (_J_GATE - _J_U) * COL_TILE
    u_tiles = _J_GATE - _J_U
    nh = HEADS_PER_GROUP
    out_specs = [
        pl.BlockSpec((None, nh, tm, HEAD_DIM), lambda i, j: (i // nt, jnp.clip(j - 1, 0, _J_V - 1), i % nt, 0)),
        pl.BlockSpec((None, nh, tm, HEAD_DIM), lambda i, j: (i // nt, jnp.clip(j - _J_V, 0, N_GROUPS - 1), i % nt, 0)),
    ]
    cache_keep = tuple(None if window >= l else window for window, _ in ATT_GROUPS)
    for g, keep in enumerate(cache_keep):
        if keep is None:
            out_specs.append(pl.BlockSpec((None, None, tm * nh, HEAD_DIM),
                                          lambda i, j, g=g: (i // nt, _kv_slot(j, g), i % nt, 0)))
        else:
            assert keep <= tm
            out_specs.append(pl.BlockSpec((None, None, keep * nh, HEAD_DIM),
                                          lambda i, j, g=g: (i // nt, _kv_slot(j, g, i % nt == nt - 1), 0, 0)))
    out_specs.append(pl.BlockSpec(
        (tm, COL_TILE), lambda i, j: (i % nt, (i // nt) * u_tiles + jnp.clip(j - _J_U, 0, u_tiles - 1))))
    out_specs.append(pl.BlockSpec(
        (None, tm, COL_TILE), lambda i, j: (jnp.clip(j - _J_GATE, 0, _J_END - _J_GATE - 1), i, 0)))
    out_shape = [jax.ShapeDtypeStruct((b, _J_V * nh, l, HEAD_DIM), F32),
                 jax.ShapeDtypeStruct((b, N_GROUPS * nh, l, HEAD_DIM), F32)]
    out_shape += [jax.ShapeDtypeStruct((b, 2, (l if keep is None else keep) * nh, HEAD_DIM), F32)
                  for keep in cache_keep]
    out_shape += [jax.ShapeDtypeStruct((l, b * ssm_w), F32),
                  jax.ShapeDtypeStruct((_J_END - _J_GATE, m, COL_TILE), F32)]
    return pl.pallas_call(
        functools.partial(_inproj_kernel, head_major=True, cache_keep=cache_keep),
        grid=(m // tm, _J_END),
        in_specs=_inproj_common_specs(d, tm, nt),
        out_specs=out_specs,
        out_shape=out_shape,
        scratch_shapes=[pltpu.VMEM((tm, d), BF16), pltpu.VMEM((2, tm, COL_TILE), F32)],
        compiler_params=_params(("arbitrary", "arbitrary")),
        name="inproj_prompt",
    )(x, gain.reshape(1, d), w, qk_gain, cos, sin)


def _inproj_sample(x, gain, w, qk_gain, cos, sin):
    m, d = x.shape
    u_tiles = _J_GATE - _J_U
    out_specs = [pl.BlockSpec((m, COL_TILE), lambda i, j: (0, jnp.clip(j - 1, 0, _J_V - 1))),
                 pl.BlockSpec((m, COL_TILE), lambda i, j: (0, jnp.clip(j - _J_V, 0, N_GROUPS - 1)))]
    for g in range(N_GROUPS):
        out_specs.append(pl.BlockSpec((None, m, COL_TILE), lambda i, j, g=g: (_kv_slot(j, g), 0, 0)))
    out_specs.append(pl.BlockSpec((m, COL_TILE), lambda i, j: (0, jnp.clip(j - _J_U, 0, u_tiles - 1))))
    out_specs.append(pl.BlockSpec((None, m, COL_TILE),
                                  lambda i, j: (jnp.clip(j - _J_GATE, 0, _J_END - _J_GATE - 1), 0, 0)))
    out_shape = [jax.ShapeDtypeStruct((m, _J_V * COL_TILE), F32), jax.ShapeDtypeStruct((m, N_GROUPS * COL_TILE), F32)]
    out_shape += [jax.ShapeDtypeStruct((2, m, GROUP_WIDTH), F32)] * N_GROUPS
    out_shape += [jax.ShapeDtypeStruct((m, u_tiles * COL_TILE), F32),
                  jax.ShapeDtypeStruct((_J_END - _J_GATE, m, COL_TILE), F32)]
    return pl.pallas_call(
        functools.partial(_inproj_kernel, head_major=False, cache_keep=(None,) * N_GROUPS),
        grid=(1, _J_END),
        in_specs=_inproj_common_specs(d, m, 1),
        out_specs=out_specs,
        out_shape=out_shape,
        scratch_shapes=[pltpu.VMEM((m, d), BF16), pltpu.VMEM((2, m, COL_TILE), F32)],
        compiler_params=_params(("arbitrary", "arbitrary")),
        name="inproj_sample",
    )(x, gain.reshape(1, d), w, qk_gain, cos, sin)


def _rope_tables(pos):
    half = HEAD_DIM // 2
    inv = jnp.power(ROPE_THETA, -jnp.arange(half, dtype=F32) * (2.0 / HEAD_DIM))
    ang = pos.astype(F32)[:, None] * inv[None, :]
    cos, sin = jnp.cos(ang), jnp.sin(ang)
    return jnp.concatenate([cos, cos], axis=1), jnp.concatenate([-sin, sin], axis=1)


ATTN_UNROLL = 16


def _attn_prompt_kernel(*refs):
    qkv = refs[:3 * N_GROUPS]
    o_ref = refs[3 * N_GROUPS]
    stats = (refs[3 * N_GROUPS + 1:3 * N_GROUPS + 4], refs[3 * N_GROUPS + 4:3 * N_GROUPS + 7])
    seq = o_ref.shape[0]
    row = lax.broadcasted_iota(jnp.int32, (QBLOCK, QBLOCK), 0)
    col = lax.broadcasted_iota(jnp.int32, (QBLOCK, QBLOCK), 1)
    cur_mask = col <= row
    prev_mask = col >= row
    nt_dims = (((1,), (1,)), ((), ()))
    full = (QBLOCK, HEAD_DIM)
    ones = jnp.ones(full, BF16)

    for g, (_, dil) in enumerate(ATT_GROUPS):
        q_ref, k_ref, v_ref = qkv[3 * g:3 * g + 3]
        nblk = seq // dil // QBLOCK
        span = QBLOCK * dil

        def rows_at(start, dil=dil):
            return pl.ds(start, QBLOCK) if dil == 1 else pl.ds(start, QBLOCK, stride=dil)

        def block(r, n, q_ref=q_ref, k_ref=k_ref, v_ref=v_ref, nblk=nblk, span=span, rows_at=rows_at):
            start = r + n * span
            rows = rows_at(start)
            q = q_ref[rows, :].astype(BF16)
            k = k_ref[rows, :].astype(BF16)
            v = jnp.concatenate([v_ref[rows, :].astype(BF16), ones], axis=1)
            if nblk > 1:
                prows = rows_at(jnp.maximum(start - span, r))
                k = jnp.concatenate([k, k_ref[prows, :].astype(BF16)], axis=0)
                vp = jnp.concatenate([v_ref[prows, :].astype(BF16), ones], axis=1)
                v = jnp.concatenate([v, vp], axis=0)
            s = lax.dot_general(q, k, nt_dims, preferred_element_type=F32) * ATT_SCALE
            if nblk > 1:
                prev_pen = jnp.where(n > 0, 0.0, -jnp.inf)
                s = jnp.concatenate([jnp.where(cur_mask, s[:, :QBLOCK], -jnp.inf),
                                     jnp.where(prev_mask, s[:, QBLOCK:], -jnp.inf) + prev_pen], axis=1)
                mx = jnp.max(jnp.maximum(s[:, :QBLOCK], s[:, QBLOCK:]), axis=1, keepdims=True)
            else:
                s = jnp.where(cur_mask, s, -jnp.inf)
                mx = jnp.max(s, axis=1, keepdims=True)
            p = jnp.exp(s - mx).astype(BF16)
            acc = jnp.dot(p, v, preferred_element_type=F32)
            return rows, acc[:, :HEAD_DIM], jnp.broadcast_to(mx, full), acc[:, HEAD_DIM:]

        src = stats[(g - 1) % 2] if g > 0 else None
        dst = stats[g % 2] if g < N_GROUPS - 1 else None

        def merge(rows, acc, mb, lb, src=src, dst=dst):
            if src is not None:
                m_old = src[1][rows, :]
                m_new = jnp.maximum(m_old, mb)
                a_old, a_new = jnp.exp(m_old - m_new), jnp.exp(mb - m_new)
                acc = a_old * src[0][rows, :] + a_new * acc
                lb = a_old * src[2][rows, :] + a_new * lb
                mb = m_new
            if dst is not None:
                dst[0][rows, :] = acc
                dst[1][rows, :] = mb
                dst[2][rows, :] = lb
            else:
                o_ref[rows, :] = acc / lb

        def body(i, carry, dil=dil, block=block, merge=merge):
            results = []
            for d in range(ATTN_UNROLL):
                idx = i * ATTN_UNROLL + d
                results.append(block(idx % dil, idx // dil))
            for res in results:
                merge(*res)
            return carry

        lax.fori_loop(0, dil * nblk // ATTN_UNROLL, body, 0)


def _attn_prompt(qk, v, *, b, l):
    nh = HEADS_PER_GROUP
    in_specs, args = [], []
    for g in range(N_GROUPS):
        for part in range(3):
            in_specs.append(pl.BlockSpec(
                (None, None, l, HEAD_DIM),
                lambda bi, hs, g=g, part=part: (bi, (part % 2) * N_GROUPS * nh + g * nh + hs, 0, 0)))
            args.append(v if part == 2 else qk)
    return pl.pallas_call(
        _attn_prompt_kernel,
        grid=(b, nh),
        in_specs=in_specs,
        out_specs=pl.BlockSpec((None, None, l, HEAD_DIM), lambda bi, hs: (bi, hs, 0, 0)),
        out_shape=jax.ShapeDtypeStruct((b, nh, l, HEAD_DIM), F32),
        scratch_shapes=[pltpu.VMEM((l, HEAD_DIM), F32)] * 6,
        compiler_params=_params(("parallel", "parallel")),
        name="attn_prompt",
    )(*args)


def _attn_decode_kernel(q_ref, *refs):
    o_ref = refs[-1]
    outs, lses = [], []
    for g in range(N_GROUPS):
        kc_ref, vc_ref, kn_ref, vn_ref = refs[4 * g:4 * g + 4]
        qg = q_ref[g][None]
        s = jnp.sum(kc_ref[...] * qg, axis=-1, keepdims=True) * ATT_SCALE
        s_new = jnp.sum(kn_ref[...] * qg, axis=-1, keepdims=True) * ATT_SCALE
        mx = jnp.maximum(jnp.max(s, axis=0, keepdims=True), s_new)
        p = jnp.exp(s - mx)
        p_new = jnp.exp(s_new - mx)
        den = jnp.sum(p, axis=0, keepdims=True) + p_new
        acc = jnp.sum(p * vc_ref[...], axis=0, keepdims=True) + p_new * vn_ref[...]
        outs.append(acc / den)
        lses.append(mx + jnp.log(den))
    mx = functools.reduce(jnp.maximum, lses)
    ws = [jnp.exp(lse - mx) for lse in lses]
    num = functools.reduce(lambda a, c: a + c, [w * o for w, o in zip(ws, outs)])
    o_ref[...] = (num / functools.reduce(lambda a, c: a + c, ws))[0]


def _attn_decode(q, caches, kvnews):
    b = q.shape[0]
    nh = HEADS_PER_GROUP
    in_specs = [pl.BlockSpec((None, N_GROUPS, nh, HEAD_DIM), lambda i: (i, 0, 0, 0))]
    args = [q]
    for g, (window, dil) in enumerate(ATT_GROUPS):
        cache = caches[g]
        assert cache.shape[2] == window and window == N_BACK * dil
        cv = cache.reshape(b, 2, N_BACK, dil, nh, HEAD_DIM)
        in_specs += [
            pl.BlockSpec((None, None, N_BACK, None, nh, HEAD_DIM), lambda i: (i, 0, 0, 0, 0, 0)),
            pl.BlockSpec((None, None, N_BACK, None, nh, HEAD_DIM), lambda i: (i, 1, 0, 0, 0, 0)),
            pl.BlockSpec((None, None, 1, nh, HEAD_DIM), lambda i: (0, i, 0, 0, 0)),
            pl.BlockSpec((None, None, 1, nh, HEAD_DIM), lambda i: (1, i, 0, 0, 0)),
        ]
        args += [cv, cv, kvnews[g], kvnews[g]]
    return pl.pallas_call(
        _attn_decode_kernel,
        grid=(b,),
        in_specs=in_specs,
        out_specs=pl.BlockSpec((None, nh, HEAD_DIM), lambda i: (i, 0, 0)),
        out_shape=jax.ShapeDtypeStruct((b, nh, HEAD_DIM), F32),
        compiler_params=_params(("parallel",)),
        name="attn_decode",
    )(*args)


def _cache_shift_kernel(c_ref, new_ref, o_ref):
    w = c_ref.shape[0]
    o_ref[pl.ds(0, w - 1)] = c_ref[pl.ds(1, w - 1)]
    o_ref[pl.ds(w - 1, 1)] = new_ref[...]


def _cache_shift(cache, kvnew):
    b, _, w, nh, e = cache.shape
    return pl.pallas_call(
        _cache_shift_kernel,
        grid=(b, 2),
        in_specs=[
            pl.BlockSpec((None, None, w, nh, e), lambda i, s: (i, s, 0, 0, 0)),
            pl.BlockSpec((None, None, 1, nh, e), lambda i, s: (s, i, 0, 0, 0)),
        ],
        out_specs=pl.BlockSpec((None, None, w, nh, e), lambda i, s: (i, s, 0, 0, 0)),
        out_shape=jax.ShapeDtypeStruct(cache.shape, F32),
        compiler_params=_params(("parallel", "parallel")),
        name=f"cache_shift_w{w}",
    )(cache, kvnew)


def _ssm_prep_kernel(lr_ref, li_ref, ldt_ref, br_ref, bi_ref, cim_ref,
                     ar_ref, ai_ref, bbr_ref, bbi_ref, ncim_ref):
    lr, li = lr_ref[...], li_ref[...]
    dt = jnp.exp(ldt_ref[...])
    mag = jnp.exp(lr * dt)
    ar = mag * jnp.cos(li * dt)
    ai = mag * jnp.sin(li * dt)
    den = lr * lr + li * li
    fr = ((ar - 1.0) * lr + ai * li) / den
    fi = (ai * lr - (ar - 1.0) * li) / den
    br, bi = br_ref[...], bi_ref[...]
    ar_ref[...] = ar
    ai_ref[...] = ai
    bbr_ref[...] = fr * br - fi * bi
    bbi_ref[...] = fr * bi + fi * br
    ncim_ref[...] = -cim_ref[...]


def _ssm_prepare(lam_re, lam_im, log_dt, b_re, b_im, c_re, c_im):
    gn, pn = lam_re.shape
    cn = SSM_GROUP
    rows = gn * cn
    rep = lambda a: jnp.broadcast_to(a[:, None, :], (gn, cn, pn)).reshape(rows, pn)
    tr = lambda a: jnp.transpose(a, (0, 2, 1)).reshape(rows, pn)
    ldt = jnp.broadcast_to(log_dt[:, None, None], (gn, cn, pn)).reshape(rows, pn)
    shp = jax.ShapeDtypeStruct((rows, pn), F32)
    ar, ai, bbr, bbi, ncim = pl.pallas_call(
        _ssm_prep_kernel, out_shape=[shp] * 5, name="ssm_prep",
    )(rep(lam_re), rep(lam_im), ldt, tr(b_re), tr(b_im), c_im.reshape(rows, pn))
    nq, gb = gn // SSM_GROUPS_PER_BLOCK, SSM_GROUPS_PER_BLOCK
    diag = (jnp.arange(gb)[:, None, None, None] == jnp.arange(gb)[None, None, :, None])
    spread = lambda a: jnp.where(diag, a.reshape(2, nq, gb, cn, 1, pn), 0.0)
    bb = spread(jnp.stack([bbr, bbi]))
    bblk = jnp.transpose(bb, (1, 2, 3, 0, 4, 5)).reshape(nq, gb * cn, 2 * gb * pn).astype(BF16)
    cc = spread(jnp.stack([c_re.reshape(rows, pn), ncim]))
    cblk = jnp.transpose(cc, (1, 0, 2, 5, 4, 3)).reshape(nq, 2 * gb * pn, gb * cn).astype(BF16)
    a_re = ar.reshape(gn, cn, pn)[:, 0, :].reshape(gn * pn // LANES, LANES)
    a_im = ai.reshape(gn, cn, pn)[:, 0, :].reshape(gn * pn // LANES, LANES)
    return a_re, a_im, bblk, cblk


def _ssm_kernel(u_ref, x0_ref, bblk_ref, cblk_ref, are_ref, aim_ref, d_ref,
                y_ref, sfin_ref, ut_ref, yt_ref, bu_ref, xs_ref, st_ref, *, nb, steps):
    c = pl.program_id(0)
    nq = bblk_ref.shape[0]
    tiles_per_q = 2 * SSM_GROUPS_PER_BLOCK * SSM_STATE // LANES
    half = tiles_per_q // 2

    @pl.when(c == 0)
    def _():
        for j in range(nq * tiles_per_q):
            st_ref[j] = x0_ref[:, j * LANES:(j + 1) * LANES]

    width = nq * LANES
    for s in range(nb if steps > 1 else 1):
        for q in range(nq):
            if steps > 1:
                ut_ref[q, pl.ds(s, steps, stride=nb), :] = u_ref[:, s * width + q * LANES:s * width + (q + 1) * LANES]
            else:
                ut_ref[q] = u_ref[:, q * LANES:(q + 1) * LANES]
    for q in range(nq):
        res = jnp.dot(ut_ref[q].astype(BF16), bblk_ref[q], preferred_element_type=F32)
        for j in range(tiles_per_q):
            bu_ref[q * tiles_per_q + j] = res[:, j * LANES:(j + 1) * LANES]

    def scan_block(q, carry):
        for lt in range(half):
            jr = q * tiles_per_q + lt
            ji = jr + half
            ar = are_ref[q * half + lt]
            ai = aim_ref[q * half + lt]
            xr, xi = st_ref[jr], st_ref[ji]
            for t in range(steps):
                rows = pl.ds(t * nb, nb)
                xr, xi = (ar * xr - ai * xi + bu_ref[jr, rows, :],
                          ar * xi + ai * xr + bu_ref[ji, rows, :])
                xs_ref[jr, rows, :] = xr.astype(BF16)
                xs_ref[ji, rows, :] = xi.astype(BF16)
            st_ref[jr] = xr
            st_ref[ji] = xi
        return carry

    for q in range(nq):
        scan_block(q, 0)

    for q in range(nq):
        xq = jnp.concatenate([xs_ref[q * tiles_per_q + j] for j in range(tiles_per_q)], axis=1)
        yq = jnp.dot(xq, cblk_ref[q], preferred_element_type=F32)
        sl = slice(q * LANES, (q + 1) * LANES)
        yq = yq + d_ref[:, sl] * ut_ref[q]
        if steps > 1:
            yt_ref[q] = yq
        else:
            y_ref[:, sl] = yq
    if steps > 1:
        for s in range(nb):
            for q in range(nq):
                y_ref[:, s * width + q * LANES:s * width + (q + 1) * LANES] = yt_ref[q, pl.ds(s, steps, stride=nb), :]

    @pl.when(c == pl.num_programs(0) - 1)
    def _():
        for j in range(nq * tiles_per_q):
            sfin_ref[:, j * LANES:(j + 1) * LANES] = st_ref[j]


def _ssm(u, x0, a_re, a_im, bblk, cblk, d_skip, *, nb, steps):
    width = d_skip.shape[0]
    nsteps = u.shape[0] // steps if steps > 1 else 1
    blk = (steps, nb * width) if steps > 1 else (nb, width)
    rows = steps * nb
    ntile = a_re.shape[0]
    nstate = 2 * ntile * LANES
    are = jnp.broadcast_to(a_re[:, None, :], (ntile, nb, LANES))
    aim = jnp.broadcast_to(a_im[:, None, :], (ntile, nb, LANES))
    const = lambda shape: pl.BlockSpec(shape, lambda c: (0,) * len(shape))
    return pl.pallas_call(
        functools.partial(_ssm_kernel, nb=nb, steps=steps),
        grid=(nsteps,),
        in_specs=[
            pl.BlockSpec(blk, lambda c: (c, 0)),
            const((nb, nstate)),
            const(bblk.shape),
            const(cblk.shape),
            const(are.shape),
            const(aim.shape),
            const((1, width)),
        ],
        out_specs=[pl.BlockSpec(blk, lambda c: (c, 0)), const((nb, nstate))],
        out_shape=[jax.ShapeDtypeStruct(u.shape, F32), jax.ShapeDtypeStruct((nb, nstate), F32)],
        scratch_shapes=[
            pltpu.VMEM((width // LANES, rows, LANES), F32),
            pltpu.VMEM((width // LANES, rows, LANES), F32),
            pltpu.VMEM((2 * ntile, rows, LANES), F32),
            pltpu.VMEM((2 * ntile, rows, LANES), BF16),
            pltpu.VMEM((2 * ntile, nb, LANES), F32),
        ],
        compiler_params=_params(("arbitrary",)),
        name=f"ssm_nb{nb}",
    )(u, x0, bblk, cblk, are, aim, d_skip.reshape(1, width))


def _state_to_lanes(s_re, s_im):
    b, gn, pn = s_re.shape
    nq = gn // SSM_GROUPS_PER_BLOCK
    st = jnp.stack([s_re.reshape(b, nq, -1), s_im.reshape(b, nq, -1)], axis=2)
    return st.reshape(b, 2 * gn * pn)


def _lanes_to_state(s, gn, pn):
    b = s.shape[0]
    nq = gn // SSM_GROUPS_PER_BLOCK
    st = s.reshape(b, nq, 2, SSM_GROUPS_PER_BLOCK * pn)
    return st[:, :, 0].reshape(b, gn, pn), st[:, :, 1].reshape(b, gn, pn)


def _mix_kernel(x_ref, att_ref, y_ref, ga_ref, gs_ref, gw_ref, gb_ref, wa_ref, ws_ref, wo_ref, out_ref):
    att = jnp.concatenate([att_ref[h] for h in range(HEADS_PER_GROUP)], axis=1)
    a_proj = jnp.dot(att.astype(BF16), wa_ref[...], preferred_element_type=F32)
    zs = jax.nn.gelu(y_ref[...])
    glu = jnp.dot(zs.astype(BF16), gw_ref[...], preferred_element_type=F32) + gb_ref[...]
    s_out = zs * jax.nn.sigmoid(glu)
    s_proj = jnp.dot(s_out.astype(BF16), ws_ref[...], preferred_element_type=F32)
    gate_a = jnp.concatenate([ga_ref[t] for t in range(ga_ref.shape[0])], axis=1)
    gate_s = jnp.concatenate([gs_ref[t] for t in range(gs_ref.shape[0])], axis=1)
    merged = jax.nn.sigmoid(gate_a) * a_proj + jax.nn.sigmoid(gate_s) * s_proj
    out_ref[...] = x_ref[...] + jnp.dot(merged.astype(BF16), wo_ref[...], preferred_element_type=F32)


def _mix(x, att, y, y_map, gates, glu_w, glu_b, wa, ws, wo, *, tm, nt):
    m, d = x.shape
    sw = glu_w.shape[0]
    row = lambda width: pl.BlockSpec((tm, width), lambda i: (i, 0))
    const = lambda a: pl.BlockSpec(a.shape, lambda i: (0, 0), pipeline_mode=pl.Buffered(1))
    glu_b = glu_b.reshape(1, sw)
    in_specs = [row(d),
                pl.BlockSpec((None, HEADS_PER_GROUP, tm, HEAD_DIM), lambda i: (i // nt, 0, i % nt, 0)),
                pl.BlockSpec((tm, sw), y_map),
                pl.BlockSpec((d // COL_TILE, tm, COL_TILE), lambda i: (0, i, 0)),
                pl.BlockSpec((d // COL_TILE, tm, COL_TILE), lambda i: (1, i, 0)),
                const(glu_w), const(glu_b), const(wa), const(ws), const(wo)]
    return pl.pallas_call(
        _mix_kernel,
        grid=(m // tm,),
        in_specs=in_specs,
        out_specs=row(d),
        out_shape=jax.ShapeDtypeStruct((m, d), F32),
        compiler_params=_params(("parallel",)),
        name="mix",
    )(x, att, y, gates, gates, glu_w, glu_b, wa, ws, wo)


FFN_TF = 512
PROMPT_TM_FFN = 1024
PROMPT_TM_INPROJ = 1024
PROMPT_TM_MIX = 256
SSM_STEPS = 16


def _layer_weights(p):
    w = dict(p)
    for name in ("ffn1", "ffn2"):
        w[name] = _ffn_weights(p[name + "_w_gate"], p[name + "_w_up"], p[name + "_w_down"], FFN_TF)
    d_in, n_in = p["w_in"].shape
    w["w_in_b"] = jnp.transpose(p["w_in"].reshape(d_in, n_in // COL_TILE, COL_TILE), (1, 0, 2)).astype(BF16)
    w["qk_gain"] = jnp.concatenate([p["q_norm"].reshape(1, -1), p["k_norm"].reshape(1, -1)], axis=1)
    w["ssm"] = _ssm_prepare(p["ssm_lambda_re"], p["ssm_lambda_im"], p["ssm_log_dt"],
                            p["ssm_b_re"], p["ssm_b_im"], p["ssm_c_re"], p["ssm_c_im"])
    for name in ("glu_w", "w_attn_branch", "w_ssm_branch", "w_out"):
        w[name + "_b"] = p[name].astype(BF16)
    return w


def _prompt_layer(x, w):
    b, l, d = x.shape
    m = b * l
    nh = HEADS_PER_GROUP
    gn, pn = w["ssm_lambda_re"].shape
    x1 = _ffn(x.reshape(m, d), w["ffn1_norm"], *w["ffn1"], tm=PROMPT_TM_FFN, tf=FFN_TF)
    cos, sin = _rope_tables(jnp.arange(l, dtype=jnp.int32))
    tm_in = min(PROMPT_TM_INPROJ, l)
    qk, v, kv0, kv1, kv2, u_t, gates = _inproj_prompt(
        x1, w["mix_norm"], w["w_in_b"], w["qk_gain"], cos, sin, b=b, l=l, tm=tm_in)
    att = _attn_prompt(qk, v, b=b, l=l)
    a_re, a_im, bblk, cblk = w["ssm"]
    y_t, sfin = _ssm(u_t, jnp.zeros((b, 2 * gn * pn), F32),
                     a_re, a_im, bblk, cblk, w["ssm_d"], nb=b, steps=SSM_STEPS)
    tm = PROMPT_TM_MIX
    nt = l // tm
    x2 = _mix(x1, att, y_t, lambda i: (i % nt, i // nt), gates,
              w["glu_w_b"], w["glu_b"], w["w_attn_branch_b"], w["w_ssm_branch_b"], w["w_out_b"], tm=tm, nt=nt)
    y = _ffn(x2, w["ffn2_norm"], *w["ffn2"], tm=PROMPT_TM_FFN, tf=FFN_TF)
    new_kv = []
    for g, (window, _) in enumerate(ATT_GROUPS):
        new_kv.append((kv0, kv1, kv2)[g].reshape(b, 2, min(window, l), nh, HEAD_DIM))
    s_re, s_im = _lanes_to_state(sfin, gn, pn)
    return y.reshape(b, l, d), new_kv, (s_re, s_im)


def _sample_layer(x, pos0, caches, state, w):
    b, l, d = x.shape
    assert l == 1
    nh = HEADS_PER_GROUP
    gn, pn = w["ssm_lambda_re"].shape
    x1 = _ffn(x.reshape(b, d), w["ffn1_norm"], *w["ffn1"], tm=b, tf=FFN_TF)
    cos, sin = _rope_tables(jnp.full((b,), pos0, dtype=jnp.int32))
    qk, _, kn0, kn1, kn2, u, gates = _inproj_sample(x1, w["mix_norm"], w["w_in_b"], w["qk_gain"], cos, sin)
    q = qk[:, :ATT_WIDTH].reshape(b, N_GROUPS, nh, HEAD_DIM)
    kvnews = [kn.reshape(2, b, 1, nh, HEAD_DIM) for kn in (kn0, kn1, kn2)]
    att = _attn_decode(q, caches, kvnews)
    new_kv = [_cache_shift(caches[g], kvnews[g]) for g in range(N_GROUPS)]
    a_re, a_im, bblk, cblk = w["ssm"]
    y, sfin = _ssm(u, _state_to_lanes(*state), a_re, a_im, bblk, cblk, w["ssm_d"], nb=b, steps=1)
    att_h = jnp.transpose(att, (1, 0, 2))[None]
    x2 = _mix(x1, att_h, y, lambda i: (i, 0), gates,
              w["glu_w_b"], w["glu_b"], w["w_attn_branch_b"], w["w_ssm_branch_b"], w["w_out_b"], tm=b, nt=1)
    out = _ffn(x2, w["ffn2_norm"], *w["ffn2"], tm=b, tf=FFN_TF)
    s_re, s_im = _lanes_to_state(sfin, gn, pn)
    return out.reshape(b, l, d), new_kv, (s_re, s_im)


def kernel(x_prompt, x_sample, cache_kv_w128, cache_kv_w512, cache_kv_w2048, state_ssm_re, state_ssm_im,
           ffn1_norm, ffn1_w_gate, ffn1_w_up, ffn1_w_down, mix_norm, w_in, q_norm, k_norm,
           ssm_lambda_re, ssm_lambda_im, ssm_log_dt, ssm_b_re, ssm_b_im, ssm_c_re, ssm_c_im, ssm_d,
           glu_w, glu_b, w_attn_branch, w_ssm_branch, w_out,
           ffn2_norm, ffn2_w_gate, ffn2_w_up, ffn2_w_down):
    depth = w_in.shape[0]
    params = dict(ffn1_norm=ffn1_norm, ffn1_w_gate=ffn1_w_gate, ffn1_w_up=ffn1_w_up, ffn1_w_down=ffn1_w_down,
                  mix_norm=mix_norm, w_in=w_in, q_norm=q_norm, k_norm=k_norm,
                  ssm_lambda_re=ssm_lambda_re, ssm_lambda_im=ssm_lambda_im, ssm_log_dt=ssm_log_dt,
                  ssm_b_re=ssm_b_re, ssm_b_im=ssm_b_im, ssm_c_re=ssm_c_re, ssm_c_im=ssm_c_im, ssm_d=ssm_d,
                  glu_w=glu_w, glu_b=glu_b, w_attn_branch=w_attn_branch, w_ssm_branch=w_ssm_branch,
                  w_out=w_out, ffn2_norm=ffn2_norm, ffn2_w_gate=ffn2_w_gate, ffn2_w_up=ffn2_w_up,
                  ffn2_w_down=ffn2_w_down)
    yp, ys = x_prompt, x_sample
    new_p = [[] for _ in range(5)]
    new_s = [[] for _ in range(5)]
    for layer in range(depth):
        w = _layer_weights({k: v[layer] for k, v in params.items()})
        yp, kv_p, st_p = _prompt_layer(yp, w)
        ys, kv_s, st_s = _sample_layer(
            ys, PAST_LEN, (cache_kv_w128[layer], cache_kv_w512[layer], cache_kv_w2048[layer]),
            (state_ssm_re[layer], state_ssm_im[layer]), w)
        for i, a in enumerate(list(kv_p) + list(st_p)):
            new_p[i].append(a)
        for i, a in enumerate(list(kv_s) + list(st_s)):
            new_s[i].append(a)
    outs_p = [jnp.stack(a) for a in new_p]
    outs_s = [jnp.stack(a) for a in new_s]
    return (yp, ys, *outs_p, *outs_s)
```

```python
import functools

import jax
import jax.numpy as jnp
from jax import lax
from jax.experimental import pallas as pl
from jax.experimental.pallas import tpu as pltpu

F32 = jnp.float32
BF16 = jnp.bfloat16

HEAD_DIM = 128
HEADS_PER_GROUP = 4
GROUP_WIDTH = HEADS_PER_GROUP * HEAD_DIM
ATT_GROUPS = ((128, 1), (512, 4), (2048, 16))
N_GROUPS = len(ATT_GROUPS)
ATT_WIDTH = N_GROUPS * GROUP_WIDTH
N_BACK = 128
QBLOCK = 128
ATT_SCALE = HEAD_DIM ** -0.5
ROPE_THETA = 10000.0
RMS_EPS = 1e-6
PAST_LEN = 16384
SSM_GROUP = 16
SSM_STATE = 64
SSM_GROUPS_PER_BLOCK = 8
LANES = 128
COL_TILE = 512
VMEM_LIMIT = 56 * 1024 * 1024


def _params(semantics):
    return pltpu.CompilerParams(dimension_semantics=semantics, vmem_limit_bytes=VMEM_LIMIT)


def _rms_rows(x, gain):
    ms = jnp.mean(x * x, axis=-1, keepdims=True)
    return x * lax.rsqrt(ms + RMS_EPS) * gain


def _head(h):
    return slice(h * HEAD_DIM, (h + 1) * HEAD_DIM)


def _ffn_kernel(x_ref, g_ref, wg_ref, wu_ref, wd_ref, o_ref, h_ref, *, d_ff):
    def accumulate():
        h = h_ref[...]
        g = jnp.dot(h, wg_ref[...], preferred_element_type=F32)
        u = jnp.dot(h, wu_ref[...], preferred_element_type=F32)
        col = pl.program_id(1) * g.shape[1] + lax.broadcasted_iota(jnp.int32, g.shape, 1)
        a = jnp.where(col < d_ff, 0.5 * (g * jax.nn.sigmoid(g) * u), 0.0).astype(BF16)
        o_ref[...] += jnp.dot(a, wd_ref[...], preferred_element_type=F32)

    @pl.when(pl.program_id(1) == 0)
    def _():
        x = x_ref[...]
        h_ref[...] = _rms_rows(x, g_ref[...]).astype(BF16)
        o_ref[...] = x
        accumulate()

    @pl.when(pl.program_id(1) > 0)
    def _():
        accumulate()


def _ffn(x, gain, wg, wu, wd, *, tm, tf):
    m, d = x.shape
    d_ff = wg.shape[1]
    return pl.pallas_call(
        functools.partial(_ffn_kernel, d_ff=d_ff),
        grid=(m // tm, wd.shape[0] // tf),
        in_specs=[
            pl.BlockSpec((tm, d), lambda i, f: (i, 0)),
            pl.BlockSpec((1, d), lambda i, f: (0, 0)),
            pl.BlockSpec((d, tf), lambda i, f: (0, f)),
            pl.BlockSpec((d, tf), lambda i, f: (0, f)),
            pl.BlockSpec((tf, d), lambda i, f: (f, 0)),
        ],
        out_specs=pl.BlockSpec((tm, d), lambda i, f: (i, 0)),
        out_shape=jax.ShapeDtypeStruct((m, d), F32),
        scratch_shapes=[pltpu.VMEM((tm, d), BF16)],
        compiler_params=_params(("parallel", "arbitrary")),
        name="ffn",
    )(x, gain.reshape(1, d), wg, wu, wd)


def _ffn_weights(wg, wu, wd, tf):
    pad = (-wg.shape[1]) % tf
    return wg.astype(BF16), wu.astype(BF16), jnp.pad(wd.astype(BF16), ((0, pad), (0, 0)))


_J_K, _J_V, _J_U, _J_GATE, _J_END = 3, 6, 9, 11, 19
EPILOGUE_ROWS = 256


def _inproj_kernel(x_ref, g_ref, w_ref, qkg_ref, cos_ref, sin_ref,
                   qk_ref, v_ref, kv0_ref, kv1_ref, kv2_ref, u_ref, gate_ref, h_ref, z_ref,
                   *, head_major, cache_keep):
    j = pl.program_id(1)
    rows = h_ref.shape[0]
    nh = HEADS_PER_GROUP
    kv_refs = (kv0_ref, kv1_ref, kv2_ref)

    @pl.when(j == 0)
    def _():
        h_ref[...] = _rms_rows(x_ref[...], g_ref[...]).astype(BF16)

    def zdot():
        return jnp.dot(h_ref[...], w_ref[...], preferred_element_type=F32)

    def put_heads(ref, heads, r0, n):
        if head_major:
            for h in range(nh):
                ref[h, pl.ds(r0, n), :] = heads[h]
        else:
            ref[pl.ds(r0, n), :] = jnp.concatenate(heads, axis=1)

    def put_cache(g, heads, r0, n):
        ref, keep = kv_refs[g], cache_keep[g]
        first = 0 if keep is None else rows - keep
        lo = max(r0, first)
        if lo >= r0 + n:
            return
        part = [hd[lo - r0:] for hd in heads]
        if head_major:
            for h in range(nh):
                ref[pl.ds((lo - first) * nh + h, r0 + n - lo, stride=nh), :] = part[h]
        else:
            ref[pl.ds(lo - first, r0 + n - lo), :] = jnp.concatenate(part, axis=1)

    def norm_rope_tile(tile):
        z_src = z_ref.at[tile % 2]
        gain = qkg_ref[...]
        n = min(EPILOGUE_ROWS, rows)
        for r0 in range(0, rows, n):
            cos, sin = cos_ref[pl.ds(r0, n), :], sin_ref[pl.ds(r0, n), :]
            heads = []
            for h in range(nh):
                y = _rms_rows(z_src[pl.ds(r0, n), _head(h)], gain[:, _head(h)])
                heads.append(y * cos + pltpu.roll(y, HEAD_DIM // 2, axis=1) * sin)
            put_heads(qk_ref, heads, r0, n)
            if tile >= _J_K:
                put_cache(tile - _J_K, heads, r0, n)

    def v_tile(g):
        z = zdot()
        heads = [z[:, _head(h)] for h in range(nh)]
        put_heads(v_ref, heads, 0, rows)
        put_cache(g, heads, 0, rows)

    for t in range(_J_V + 1):
        @pl.when(j == t)
        def _(t=t):
            if t < _J_V:
                z_ref[t % 2] = zdot()
            else:
                v_tile(0)
            if t > 0:
                norm_rope_tile(t - 1)

    for g in range(1, N_GROUPS):
        @pl.when(j == _J_V + g)
        def _(g=g):
            v_tile(g)

    @pl.when((j >= _J_U) & (j < _J_GATE))
    def _():
        u_ref[...] = zdot()

    @pl.when(j >= _J_GATE)
    def _():
        gate_ref[...] = zdot()


def _kv_slot(j, g, writes=True):
    return jnp.where(writes & (j > _J_K + g + 1), 1, 0)


def _inproj_common_specs(d, tm, nt):
    return [
        pl.BlockSpec((tm, d), lambda i, j: (i, 0)),
        pl.BlockSpec((1, d), lambda i, j: (0, 0)),
        pl.BlockSpec((d, COL_TILE), lambda i, j: (0, j)),
        pl.BlockSpec((1, COL_TILE), lambda i, j: (0, jnp.clip(j - 1, 0, _J_V - 1))),
        pl.BlockSpec((tm, HEAD_DIM), lambda i, j: (i % nt, 0)),
        pl.BlockSpec((tm, HEAD_DIM), lambda i, j: (i % nt, 0)),
    ]


def _inproj_prompt(x, gain, w, qk_gain, cos, sin, *, b, l, tm):
    m, d = x.shape
    nt = l // tm
    ssm_w = (_J_GATE - _J_U) * COL_TILE
    gate_w = (_J_END - _J_GATE) * COL_TILE
    u_tiles = _J_GATE - _J_U
    nh = HEADS_PER_GROUP
    out_specs = [
        pl.BlockSpec((None, nh, tm, HEAD_DIM), lambda i, j: (i // nt, jnp.clip(j - 1, 0, _J_V - 1), i % nt, 0)),
        pl.BlockSpec((None, nh, tm, HEAD_DIM), lambda i, j: (i // nt, jnp.clip(j - _J_V, 0, N_GROUPS - 1), i % nt, 0)),
    ]
    cache_keep = tuple(None if window >= l else window for window, _ in ATT_GROUPS)
    for g, keep in enumerate(cache_keep):
        if keep is None:
            out_specs.append(pl.BlockSpec((None, None, tm * nh, HEAD_DIM),
                                          lambda i, j, g=g: (i // nt, _kv_slot(j, g), i % nt, 0)))
        else:
            assert keep <= tm
            out_specs.append(pl.BlockSpec((None, None, keep * nh, HEAD_DIM),
                                          lambda i, j, g=g: (i // nt, _kv_slot(j, g, i % nt == nt - 1), 0, 0)))
    out_specs.append(pl.BlockSpec(
        (tm, COL_TILE), lambda i, j: (i % nt, (i // nt) * u_tiles + jnp.clip(j - _J_U, 0, u_tiles - 1))))
    out_specs.append(pl.BlockSpec(
        (tm, COL_TILE), lambda i, j: (i, jnp.clip(j - _J_GATE, 0, _J_END - _J_GATE - 1))))
    out_shape = [jax.ShapeDtypeStruct((b, _J_V * nh, l, HEAD_DIM), F32),
                 jax.ShapeDtypeStruct((b, N_GROUPS * nh, l, HEAD_DIM), F32)]
    out_shape += [jax.ShapeDtypeStruct((b, 2, (l if keep is None else keep) * nh, HEAD_DIM), F32)
                  for keep in cache_keep]
    out_shape += [jax.ShapeDtypeStruct((l, b * ssm_w), F32), jax.ShapeDtypeStruct((m, gate_w), F32)]
    return pl.pallas_call(
        functools.partial(_inproj_kernel, head_major=True, cache_keep=cache_keep),
        grid=(m // tm, _J_END),
        in_specs=_inproj_common_specs(d, tm, nt),
        out_specs=out_specs,
        out_shape=out_shape,
        scratch_shapes=[pltpu.VMEM((tm, d), BF16), pltpu.VMEM((2, tm, COL_TILE), F32)],
        compiler_params=_params(("arbitrary", "arbitrary")),
        name="inproj_prompt",
    )(x, gain.reshape(1, d), w, qk_gain, cos, sin)


def _inproj_sample(x, gain, w, qk_gain, cos, sin):
    m, d = x.shape
    u_tiles = _J_GATE - _J_U
    out_specs = [pl.BlockSpec((m, COL_TILE), lambda i, j: (0, jnp.clip(j - 1, 0, _J_V - 1))),
                 pl.BlockSpec((m, COL_TILE), lambda i, j: (0, jnp.clip(j - _J_V, 0, N_GROUPS - 1)))]
    for g in range(N_GROUPS):
        out_specs.append(pl.BlockSpec((None, m, COL_TILE), lambda i, j, g=g: (_kv_slot(j, g), 0, 0)))
    out_specs.append(pl.BlockSpec((m, COL_TILE), lambda i, j: (0, jnp.clip(j - _J_U, 0, u_tiles - 1))))
    out_specs.append(pl.BlockSpec((m, COL_TILE), lambda i, j: (0, jnp.clip(j - _J_GATE, 0, _J_END - _J_GATE - 1))))
    out_shape = [jax.ShapeDtypeStruct((m, _J_V * COL_TILE), F32), jax.ShapeDtypeStruct((m, N_GROUPS * COL_TILE), F32)]
    out_shape += [jax.ShapeDtypeStruct((2, m, GROUP_WIDTH), F32)] * N_GROUPS
    out_shape += [jax.ShapeDtypeStruct((m, u_tiles * COL_TILE), F32),
                  jax.ShapeDtypeStruct((m, (_J_END - _J_GATE) * COL_TILE), F32)]
    return pl.pallas_call(
        functools.partial(_inproj_kernel, head_major=False, cache_keep=(None,) * N_GROUPS),
        grid=(1, _J_END),
        in_specs=_inproj_common_specs(d, m, 1),
        out_specs=out_specs,
        out_shape=out_shape,
        scratch_shapes=[pltpu.VMEM((m, d), BF16), pltpu.VMEM((2, m, COL_TILE), F32)],
        compiler_params=_params(("arbitrary", "arbitrary")),
        name="inproj_sample",
    )(x, gain.reshape(1, d), w, qk_gain, cos, sin)


def _rope_tables(pos):
    half = HEAD_DIM // 2
    inv = jnp.power(ROPE_THETA, -jnp.arange(half, dtype=F32) * (2.0 / HEAD_DIM))
    ang = pos.astype(F32)[:, None] * inv[None, :]
    cos, sin = jnp.cos(ang), jnp.sin(ang)
    return jnp.concatenate([cos, cos], axis=1), jnp.concatenate([-sin, sin], axis=1)


ATTN_UNROLL = 16


def _attn_prompt_kernel(*refs):
    qkv = refs[:3 * N_GROUPS]
    o_ref = refs[3 * N_GROUPS]
    stats = (refs[3 * N_GROUPS + 1:3 * N_GROUPS + 4], refs[3 * N_GROUPS + 4:3 * N_GROUPS + 7])
    seq = o_ref.shape[0]
    row = lax.broadcasted_iota(jnp.int32, (QBLOCK, QBLOCK), 0)
    col = lax.broadcasted_iota(jnp.int32, (QBLOCK, QBLOCK), 1)
    cur_mask = col <= row
    prev_mask = col >= row
    nt_dims = (((1,), (1,)), ((), ()))
    full = (QBLOCK, HEAD_DIM)
    ones = jnp.ones(full, BF16)

    for g, (_, dil) in enumerate(ATT_GROUPS):
        q_ref, k_ref, v_ref = qkv[3 * g:3 * g + 3]
        nblk = seq // dil // QBLOCK
        span = QBLOCK * dil

        def rows_at(start, dil=dil):
            return pl.ds(start, QBLOCK) if dil == 1 else pl.ds(start, QBLOCK, stride=dil)

        def block(r, n, q_ref=q_ref, k_ref=k_ref, v_ref=v_ref, nblk=nblk, span=span, rows_at=rows_at):
            start = r + n * span
            rows = rows_at(start)
            q = q_ref[rows, :].astype(BF16)
            k = k_ref[rows, :].astype(BF16)
            v = jnp.concatenate([v_ref[rows, :].astype(BF16), ones], axis=1)
            if nblk > 1:
                prows = rows_at(jnp.maximum(start - span, r))
                k = jnp.concatenate([k, k_ref[prows, :].astype(BF16)], axis=0)
                vp = jnp.concatenate([v_ref[prows, :].astype(BF16), ones], axis=1)
                v = jnp.concatenate([v, vp], axis=0)
            s = lax.dot_general(q, k, nt_dims, preferred_element_type=F32) * ATT_SCALE
            if nblk > 1:
                prev_pen = jnp.where(n > 0, 0.0, -jnp.inf)
                s = jnp.concatenate([jnp.where(cur_mask, s[:, :QBLOCK], -jnp.inf),
                                     jnp.where(prev_mask, s[:, QBLOCK:], -jnp.inf) + prev_pen], axis=1)
                mx = jnp.max(jnp.maximum(s[:, :QBLOCK], s[:, QBLOCK:]), axis=1, keepdims=True)
            else:
                s = jnp.where(cur_mask, s, -jnp.inf)
                mx = jnp.max(s, axis=1, keepdims=True)
            p = jnp.exp(s - mx).astype(BF16)
            acc = jnp.dot(p, v, preferred_element_type=F32)
            return rows, acc[:, :HEAD_DIM], jnp.broadcast_to(mx, full), acc[:, HEAD_DIM:]

        src = stats[(g - 1) % 2] if g > 0 else None
        dst = stats[g % 2] if g < N_GROUPS - 1 else None

        def merge(rows, acc, mb, lb, src=src, dst=dst):
            if src is not None:
                m_old = src[1][rows, :]
                m_new = jnp.maximum(m_old, mb)
                a_old, a_new = jnp.exp(m_old - m_new), jnp.exp(mb - m_new)
                acc = a_old * src[0][rows, :] + a_new * acc
                lb = a_old * src[2][rows, :] + a_new * lb
                mb = m_new
            if dst is not None:
                dst[0][rows, :] = acc
                dst[1][rows, :] = mb
                dst[2][rows, :] = lb
            else:
                o_ref[rows, :] = acc / lb

        def body(i, carry, dil=dil, block=block, merge=merge):
            results = []
            for d in range(ATTN_UNROLL):
                idx = i * ATTN_UNROLL + d
                results.append(block(idx % dil, idx // dil))
            for res in results:
                merge(*res)
            return carry

        lax.fori_loop(0, dil * nblk // ATTN_UNROLL, body, 0)


def _attn_prompt(qk, v, *, b, l):
    nh = HEADS_PER_GROUP
    in_specs, args = [], []
    for g in range(N_GROUPS):
        for part in range(3):
            in_specs.append(pl.BlockSpec(
                (None, None, l, HEAD_DIM),
                lambda bi, hs, g=g, part=part: (bi, (part % 2) * N_GROUPS * nh + g * nh + hs, 0, 0)))
            args.append(v if part == 2 else qk)
    return pl.pallas_call(
        _attn_prompt_kernel,
        grid=(b, nh),
        in_specs=in_specs,
        out_specs=pl.BlockSpec((None, None, l, HEAD_DIM), lambda bi, hs: (bi, hs, 0, 0)),
        out_shape=jax.ShapeDtypeStruct((b, nh, l, HEAD_DIM), F32),
        scratch_shapes=[pltpu.VMEM((l, HEAD_DIM), F32)] * 6,
        compiler_params=_params(("parallel", "parallel")),
        name="attn_prompt",
    )(*args)


def _attn_decode_kernel(q_ref, *refs):
    o_ref = refs[-1]
    outs, lses = [], []
    for g in range(N_GROUPS):
        kc_ref, vc_ref, kn_ref, vn_ref = refs[4 * g:4 * g + 4]
        qg = q_ref[g][None]
        s = jnp.sum(kc_ref[...] * qg, axis=-1, keepdims=True) * ATT_SCALE
        s_new = jnp.sum(kn_ref[...] * qg, axis=-1, keepdims=True) * ATT_SCALE
        mx = jnp.maximum(jnp.max(s, axis=0, keepdims=True), s_new)
        p = jnp.exp(s - mx)
        p_new = jnp.exp(s_new - mx)
        den = jnp.sum(p, axis=0, keepdims=True) + p_new
        acc = jnp.sum(p * vc_ref[...], axis=0, keepdims=True) + p_new * vn_ref[...]
        outs.append(acc / den)
        lses.append(mx + jnp.log(den))
    mx = functools.reduce(jnp.maximum, lses)
    ws = [jnp.exp(lse - mx) for lse in lses]
    num = functools.reduce(lambda a, c: a + c, [w * o for w, o in zip(ws, outs)])
    o_ref[...] = (num / functools.reduce(lambda a, c: a + c, ws))[0]


def _attn_decode(q, caches, kvnews):
    b = q.shape[0]
    nh = HEADS_PER_GROUP
    in_specs = [pl.BlockSpec((None, N_GROUPS, nh, HEAD_DIM), lambda i: (i, 0, 0, 0))]
    args = [q]
    for g, (window, dil) in enumerate(ATT_GROUPS):
        cache = caches[g]
        assert cache.shape[2] == window and window == N_BACK * dil
        cv = cache.reshape(b, 2, N_BACK, dil, nh, HEAD_DIM)
        in_specs += [
            pl.BlockSpec((None, None, N_BACK, None, nh, HEAD_DIM), lambda i: (i, 0, 0, 0, 0, 0)),
            pl.BlockSpec((None, None, N_BACK, None, nh, HEAD_DIM), lambda i: (i, 1, 0, 0, 0, 0)),
            pl.BlockSpec((None, None, 1, nh, HEAD_DIM), lambda i: (0, i, 0, 0, 0)),
            pl.BlockSpec((None, None, 1, nh, HEAD_DIM), lambda i: (1, i, 0, 0, 0)),
        ]
        args += [cv, cv, kvnews[g], kvnews[g]]
    return pl.pallas_call(
        _attn_decode_kernel,
        grid=(b,),
        in_specs=in_specs,
        out_specs=pl.BlockSpec((None, nh, HEAD_DIM), lambda i: (i, 0, 0)),
        out_shape=jax.ShapeDtypeStruct((b, nh, HEAD_DIM), F32),
        compiler_params=_params(("parallel",)),
        name="attn_decode",
    )(*args)


def _cache_shift_kernel(c_ref, new_ref, o_ref):
    w = c_ref.shape[0]
    o_ref[pl.ds(0, w - 1)] = c_ref[pl.ds(1, w - 1)]
    o_ref[pl.ds(w - 1, 1)] = new_ref[...]


def _cache_shift(cache, kvnew):
    b, _, w, nh, e = cache.shape
    return pl.pallas_call(
        _cache_shift_kernel,
        grid=(b, 2),
        in_specs=[
            pl.BlockSpec((None, None, w, nh, e), lambda i, s: (i, s, 0, 0, 0)),
            pl.BlockSpec((None, None, 1, nh, e), lambda i, s: (s, i, 0, 0, 0)),
        ],
        out_specs=pl.BlockSpec((None, None, w, nh, e), lambda i, s: (i, s, 0, 0, 0)),
        out_shape=jax.ShapeDtypeStruct(cache.shape, F32),
        compiler_params=_params(("parallel", "parallel")),
        name=f"cache_shift_w{w}",
    )(cache, kvnew)


def _ssm_prep_kernel(lr_ref, li_ref, ldt_ref, br_ref, bi_ref, cim_ref,
                     ar_ref, ai_ref, bbr_ref, bbi_ref, ncim_ref):
    lr, li = lr_ref[...], li_ref[...]
    dt = jnp.exp(ldt_ref[...])
    mag = jnp.exp(lr * dt)
    ar = mag * jnp.cos(li * dt)
    ai = mag * jnp.sin(li * dt)
    den = lr * lr + li * li
    fr = ((ar - 1.0) * lr + ai * li) / den
    fi = (ai * lr - (ar - 1.0) * li) / den
    br, bi = br_ref[...], bi_ref[...]
    ar_ref[...] = ar
    ai_ref[...] = ai
    bbr_ref[...] = fr * br - fi * bi
    bbi_ref[...] = fr * bi + fi * br
    ncim_ref[...] = -cim_ref[...]


def _ssm_prepare(lam_re, lam_im, log_dt, b_re, b_im, c_re, c_im):
    gn, pn = lam_re.shape
    cn = SSM_GROUP
    rows = gn * cn
    rep = lambda a: jnp.broadcast_to(a[:, None, :], (gn, cn, pn)).reshape(rows, pn)
    tr = lambda a: jnp.transpose(a, (0, 2, 1)).reshape(rows, pn)
    ldt = jnp.broadcast_to(log_dt[:, None, None], (gn, cn, pn)).reshape(rows, pn)
    shp = jax.ShapeDtypeStruct((rows, pn), F32)
    ar, ai, bbr, bbi, ncim = pl.pallas_call(
        _ssm_prep_kernel, out_shape=[shp] * 5, name="ssm_prep",
    )(rep(lam_re), rep(lam_im), ldt, tr(b_re), tr(b_im), c_im.reshape(rows, pn))
    nq, gb = gn // SSM_GROUPS_PER_BLOCK, SSM_GROUPS_PER_BLOCK
    diag = (jnp.arange(gb)[:, None, None, None] == jnp.arange(gb)[None, None, :, None])
    spread = lambda a: jnp.where(diag, a.reshape(2, nq, gb, cn, 1, pn), 0.0)
    bb = spread(jnp.stack([bbr, bbi]))
    bblk = jnp.transpose(bb, (1, 2, 3, 0, 4, 5)).reshape(nq, gb * cn, 2 * gb * pn).astype(BF16)
    cc = spread(jnp.stack([c_re.reshape(rows, pn), ncim]))
    cblk = jnp.transpose(cc, (1, 0, 2, 5, 4, 3)).reshape(nq, 2 * gb * pn, gb * cn).astype(BF16)
    a_re = ar.reshape(gn, cn, pn)[:, 0, :].reshape(gn * pn // LANES, LANES)
    a_im = ai.reshape(gn, cn, pn)[:, 0, :].reshape(gn * pn // LANES, LANES)
    return a_re, a_im, bblk, cblk


def _ssm_kernel(u_ref, x0_ref, bblk_ref, cblk_ref, are_ref, aim_ref, d_ref,
                y_ref, sfin_ref, ut_ref, yt_ref, bu_ref, xs_ref, st_ref, *, nb, steps):
    c = pl.program_id(0)
    nq = bblk_ref.shape[0]
    tiles_per_q = 2 * SSM_GROUPS_PER_BLOCK * SSM_STATE // LANES
    half = tiles_per_q // 2

    @pl.when(c == 0)
    def _():
        for j in range(nq * tiles_per_q):
            st_ref[j] = x0_ref[:, j * LANES:(j + 1) * LANES]

    width = nq * LANES
    for s in range(nb if steps > 1 else 1):
        for q in range(nq):
            if steps > 1:
                ut_ref[q, pl.ds(s, steps, stride=nb), :] = u_ref[:, s * width + q * LANES:s * width + (q + 1) * LANES]
            else:
                ut_ref[q] = u_ref[:, q * LANES:(q + 1) * LANES]
    for q in range(nq):
        res = jnp.dot(ut_ref[q].astype(BF16), bblk_ref[q], preferred_element_type=F32)
        for j in range(tiles_per_q):
            bu_ref[q * tiles_per_q + j] = res[:, j * LANES:(j + 1) * LANES]

    def scan_block(q, carry):
        for lt in range(half):
            jr = q * tiles_per_q + lt
            ji = jr + half
            ar = are_ref[q * half + lt]
            ai = aim_ref[q * half + lt]
            xr, xi = st_ref[jr], st_ref[ji]
            for t in range(steps):
                rows = pl.ds(t * nb, nb)
                xr, xi = (ar * xr - ai * xi + bu_ref[jr, rows, :],
                          ar * xi + ai * xr + bu_ref[ji, rows, :])
                xs_ref[jr, rows, :] = xr.astype(BF16)
                xs_ref[ji, rows, :] = xi.astype(BF16)
            st_ref[jr] = xr
            st_ref[ji] = xi
        return carry

    for q in range(nq):
        scan_block(q, 0)

    for q in range(nq):
        xq = jnp.concatenate([xs_ref[q * tiles_per_q + j] for j in range(tiles_per_q)], axis=1)
        yq = jnp.dot(xq, cblk_ref[q], preferred_element_type=F32)
        sl = slice(q * LANES, (q + 1) * LANES)
        yq = yq + d_ref[:, sl] * ut_ref[q]
        if steps > 1:
            yt_ref[q] = yq
        else:
            y_ref[:, sl] = yq
    if steps > 1:
        for s in range(nb):
            for q in range(nq):
                y_ref[:, s * width + q * LANES:s * width + (q + 1) * LANES] = yt_ref[q, pl.ds(s, steps, stride=nb), :]

    @pl.when(c == pl.num_programs(0) - 1)
    def _():
        for j in range(nq * tiles_per_q):
            sfin_ref[:, j * LANES:(j + 1) * LANES] = st_ref[j]


def _ssm(u, x0, a_re, a_im, bblk, cblk, d_skip, *, nb, steps):
    width = d_skip.shape[0]
    nsteps = u.shape[0] // steps if steps > 1 else 1
    blk = (steps, nb * width) if steps > 1 else (nb, width)
    rows = steps * nb
    ntile = a_re.shape[0]
    nstate = 2 * ntile * LANES
    are = jnp.broadcast_to(a_re[:, None, :], (ntile, nb, LANES))
    aim = jnp.broadcast_to(a_im[:, None, :], (ntile, nb, LANES))
    const = lambda shape: pl.BlockSpec(shape, lambda c: (0,) * len(shape))
    return pl.pallas_call(
        functools.partial(_ssm_kernel, nb=nb, steps=steps),
        grid=(nsteps,),
        in_specs=[
            pl.BlockSpec(blk, lambda c: (c, 0)),
            const((nb, nstate)),
            const(bblk.shape),
            const(cblk.shape),
            const(are.shape),
            const(aim.shape),
            const((1, width)),
        ],
        out_specs=[pl.BlockSpec(blk, lambda c: (c, 0)), const((nb, nstate))],
        out_shape=[jax.ShapeDtypeStruct(u.shape, F32), jax.ShapeDtypeStruct((nb, nstate), F32)],
        scratch_shapes=[
            pltpu.VMEM((width // LANES, rows, LANES), F32),
            pltpu.VMEM((width // LANES, rows, LANES), F32),
            pltpu.VMEM((2 * ntile, rows, LANES), F32),
            pltpu.VMEM((2 * ntile, rows, LANES), BF16),
            pltpu.VMEM((2 * ntile, nb, LANES), F32),
        ],
        compiler_params=_params(("arbitrary",)),
        name=f"ssm_nb{nb}",
    )(u, x0, bblk, cblk, are, aim, d_skip.reshape(1, width))


def _state_to_lanes(s_re, s_im):
    b, gn, pn = s_re.shape
    nq = gn // SSM_GROUPS_PER_BLOCK
    st = jnp.stack([s_re.reshape(b, nq, -1), s_im.reshape(b, nq, -1)], axis=2)
    return st.reshape(b, 2 * gn * pn)


def _lanes_to_state(s, gn, pn):
    b = s.shape[0]
    nq = gn // SSM_GROUPS_PER_BLOCK
    st = s.reshape(b, nq, 2, SSM_GROUPS_PER_BLOCK * pn)
    return st[:, :, 0].reshape(b, gn, pn), st[:, :, 1].reshape(b, gn, pn)


def _mix_kernel(x_ref, att_ref, y_ref, ga_ref, gs_ref, gw_ref, gb_ref, wa_ref, ws_ref, wo_ref, out_ref):
    att = jnp.concatenate([att_ref[h] for h in range(HEADS_PER_GROUP)], axis=1)
    a_proj = jnp.dot(att.astype(BF16), wa_ref[...], preferred_element_type=F32)
    zs = jax.nn.gelu(y_ref[...])
    glu = jnp.dot(zs.astype(BF16), gw_ref[...], preferred_element_type=F32) + gb_ref[...]
    s_out = zs * jax.nn.sigmoid(glu)
    s_proj = jnp.dot(s_out.astype(BF16), ws_ref[...], preferred_element_type=F32)
    merged = jax.nn.sigmoid(ga_ref[...]) * a_proj + jax.nn.sigmoid(gs_ref[...]) * s_proj
    out_ref[...] = x_ref[...] + jnp.dot(merged.astype(BF16), wo_ref[...], preferred_element_type=F32)


def _mix(x, att, y, y_map, gates, glu_w, glu_b, wa, ws, wo, *, tm, nt):
    m, d = x.shape
    sw = glu_w.shape[0]
    row = lambda width: pl.BlockSpec((tm, width), lambda i: (i, 0))
    const = lambda a: pl.BlockSpec(a.shape, lambda i: (0, 0), pipeline_mode=pl.Buffered(1))
    glu_b = glu_b.reshape(1, sw)
    in_specs = [row(d),
                pl.BlockSpec((None, HEADS_PER_GROUP, tm, HEAD_DIM), lambda i: (i // nt, 0, i % nt, 0)),
                pl.BlockSpec((tm, sw), y_map),
                pl.BlockSpec((tm, d), lambda i: (i, 0)), pl.BlockSpec((tm, d), lambda i: (i, 1)),
                const(glu_w), const(glu_b), const(wa), const(ws), const(wo)]
    return pl.pallas_call(
        _mix_kernel,
        grid=(m // tm,),
        in_specs=in_specs,
        out_specs=row(d),
        out_shape=jax.ShapeDtypeStruct((m, d), F32),
        compiler_params=_params(("parallel",)),
        name="mix",
    )(x, att, y, gates, gates, glu_w, glu_b, wa, ws, wo)


FFN_TF = 512
PROMPT_TM_FFN = 1024
PROMPT_TM_INPROJ = 1024
PROMPT_TM_MIX = 256
SSM_STEPS = 16


def _layer_weights(p):
    w = dict(p)
    for name in ("ffn1", "ffn2"):
        w[name] = _ffn_weights(p[name + "_w_gate"], p[name + "_w_up"], p[name + "_w_down"], FFN_TF)
    w["w_in_b"] = p["w_in"].astype(BF16)
    w["qk_gain"] = jnp.concatenate([p["q_norm"].reshape(1, -1), p["k_norm"].reshape(1, -1)], axis=1)
    w["ssm"] = _ssm_prepare(p["ssm_lambda_re"], p["ssm_lambda_im"], p["ssm_log_dt"],
                            p["ssm_b_re"], p["ssm_b_im"], p["ssm_c_re"], p["ssm_c_im"])
    for name in ("glu_w", "w_attn_branch", "w_ssm_branch", "w_out"):
        w[name + "_b"] = p[name].astype(BF16)
    return w


def _prompt_layer(x, w):
    b, l, d = x.shape
    m = b * l
    nh = HEADS_PER_GROUP
    gn, pn = w["ssm_lambda_re"].shape
    x1 = _ffn(x.reshape(m, d), w["ffn1_norm"], *w["ffn1"], tm=PROMPT_TM_FFN, tf=FFN_TF)
    cos, sin = _rope_tables(jnp.arange(l, dtype=jnp.int32))
    tm_in = min(PROMPT_TM_INPROJ, l)
    qk, v, kv0, kv1, kv2, u_t, gates = _inproj_prompt(
        x1, w["mix_norm"], w["w_in_b"], w["qk_gain"], cos, sin, b=b, l=l, tm=tm_in)
    att = _attn_prompt(qk, v, b=b, l=l)
    a_re, a_im, bblk, cblk = w["ssm"]
    y_t, sfin = _ssm(u_t, jnp.zeros((b, 2 * gn * pn), F32),
                     a_re, a_im, bblk, cblk, w["ssm_d"], nb=b, steps=SSM_STEPS)
    tm = PROMPT_TM_MIX
    nt = l // tm
    x2 = _mix(x1, att, y_t, lambda i: (i % nt, i // nt), gates,
              w["glu_w_b"], w["glu_b"], w["w_attn_branch_b"], w["w_ssm_branch_b"], w["w_out_b"], tm=tm, nt=nt)
    y = _ffn(x2, w["ffn2_norm"], *w["ffn2"], tm=PROMPT_TM_FFN, tf=FFN_TF)
    new_kv = []
    for g, (window, _) in enumerate(ATT_GROUPS):
        new_kv.append((kv0, kv1, kv2)[g].reshape(b, 2, min(window, l), nh, HEAD_DIM))
    s_re, s_im = _lanes_to_state(sfin, gn, pn)
    return y.reshape(b, l, d), new_kv, (s_re, s_im)


def _sample_layer(x, pos0, caches, state, w):
    b, l, d = x.shape
    assert l == 1
    nh = HEADS_PER_GROUP
    gn, pn = w["ssm_lambda_re"].shape
    x1 = _ffn(x.reshape(b, d), w["ffn1_norm"], *w["ffn1"], tm=b, tf=FFN_TF)
    cos, sin = _rope_tables(jnp.full((b,), pos0, dtype=jnp.int32))
    qk, _, kn0, kn1, kn2, u, gates = _inproj_sample(x1, w["mix_norm"], w["w_in_b"], w["qk_gain"], cos, sin)
    q = qk[:, :ATT_WIDTH].reshape(b, N_GROUPS, nh, HEAD_DIM)
    kvnews = [kn.reshape(2, b, 1, nh, HEAD_DIM) for kn in (kn0, kn1, kn2)]
    att = _attn_decode(q, caches, kvnews)
    new_kv = [_cache_shift(caches[g], kvnews[g]) for g in range(N_GROUPS)]
    a_re, a_im, bblk, cblk = w["ssm"]
    y, sfin = _ssm(u, _state_to_lanes(*state), a_re, a_im, bblk, cblk, w["ssm_d"], nb=b, steps=1)
    att_h = jnp.transpose(att, (1, 0, 2))[None]
    x2 = _mix(x1, att_h, y, lambda i: (i, 0), gates,
              w["glu_w_b"], w["glu_b"], w["w_attn_branch_b"], w["w_ssm_branch_b"], w["w_out_b"], tm=b, nt=1)
    out = _ffn(x2, w["ffn2_norm"], *w["ffn2"], tm=b, tf=FFN_TF)
    s_re, s_im = _lanes_to_state(sfin, gn, pn)
    return out.reshape(b, l, d), new_kv, (s_re, s_im)


def kernel(x_prompt, x_sample, cache_kv_w128, cache_kv_w512, cache_kv_w2048, state_ssm_re, state_ssm_im,
           ffn1_norm, ffn1_w_gate, ffn1_w_up, ffn1_w_down, mix_norm, w_in, q_norm, k_norm,
           ssm_lambda_re, ssm_lambda_im, ssm_log_dt, ssm_b_re, ssm_b_im, ssm_c_re, ssm_c_im, ssm_d,
           glu_w, glu_b, w_attn_branch, w_ssm_branch, w_out,
           ffn2_norm, ffn2_w_gate, ffn2_w_up, ffn2_w_down):
    depth = w_in.shape[0]
    params = dict(ffn1_norm=ffn1_norm, ffn1_w_gate=ffn1_w_gate, ffn1_w_up=ffn1_w_up, ffn1_w_down=ffn1_w_down,
                  mix_norm=mix_norm, w_in=w_in, q_norm=q_norm, k_norm=k_norm,
                  ssm_lambda_re=ssm_lambda_re, ssm_lambda_im=ssm_lambda_im, ssm_log_dt=ssm_log_dt,
                  ssm_b_re=ssm_b_re, ssm_b_im=ssm_b_im, ssm_c_re=ssm_c_re, ssm_c_im=ssm_c_im, ssm_d=ssm_d,
                  glu_w=glu_w, glu_b=glu_b, w_attn_branch=w_attn_branch, w_ssm_branch=w_ssm_branch,
                  w_out=w_out, ffn2_norm=ffn2_norm, ffn2_w_gate=ffn2_w_gate, ffn2_w_up=ffn2_w_up,
                  ffn2_w_down=ffn2_w_down)
    yp, ys = x_prompt, x_sample
    new_p = [[] for _ in range(5)]
    new_s = [[] for _ in range(5)]
    for layer in range(depth):
        w = _layer_weights({k: v[layer] for k, v in params.items()})
        yp, kv_p, st_p = _prompt_layer(yp, w)
        ys, kv_s, st_s = _sample_layer(
            ys, PAST_LEN, (cache_kv_w128[layer], cache_kv_w512[layer], cache_kv_w2048[layer]),
            (state_ssm_re[layer], state_ssm_im[layer]), w)
        for i, a in enumerate(list(kv_p) + list(st_p)):
            new_p[i].append(a)
        for i, a in enumerate(list(kv_s) + list(st_s)):
            new_s[i].append(a)
    outs_p = [jnp.stack(a) for a in new_p]
    outs_s = [jnp.stack(a) for a in new_s]
    return (yp, ys, *outs_p, *outs_s)
```

```python
import functools

import jax
import jax.numpy as jnp
from jax import lax
from jax.experimental import pallas as pl
from jax.experimental.pallas import tpu as pltpu

F32 = jnp.float32
BF16 = jnp.bfloat16

HEAD_DIM = 128
HEADS_PER_GROUP = 4
GROUP_WIDTH = HEADS_PER_GROUP * HEAD_DIM
ATT_GROUPS = ((128, 1), (512, 4), (2048, 16))
N_GROUPS = len(ATT_GROUPS)
ATT_WIDTH = N_GROUPS * GROUP_WIDTH
N_BACK = 128
QBLOCK = 128
ATT_SCALE = HEAD_DIM ** -0.5
ROPE_THETA = 10000.0
RMS_EPS = 1e-6
PAST_LEN = 16384
SSM_GROUP = 16
SSM_STATE = 64
SSM_GROUPS_PER_BLOCK = 8
LANES = 128
COL_TILE = 512
VMEM_LIMIT = 56 * 1024 * 1024


def _params(semantics):
    return pltpu.CompilerParams(dimension_semantics=semantics, vmem_limit_bytes=VMEM_LIMIT)


def _rms_rows(x, gain):
    ms = jnp.mean(x * x, axis=-1, keepdims=True)
    return x * lax.rsqrt(ms + RMS_EPS) * gain


def _head(h):
    return slice(h * HEAD_DIM, (h + 1) * HEAD_DIM)


def _ffn_kernel(x_ref, g_ref, wg_ref, wu_ref, wd_ref, o_ref, h_ref, *, d_ff):
    def accumulate():
        h = h_ref[...]
        g = jnp.dot(h, wg_ref[...], preferred_element_type=F32)
        u = jnp.dot(h, wu_ref[...], preferred_element_type=F32)
        col = pl.program_id(1) * g.shape[1] + lax.broadcasted_iota(jnp.int32, g.shape, 1)
        a = jnp.where(col < d_ff, 0.5 * (g * jax.nn.sigmoid(g) * u), 0.0).astype(BF16)
        o_ref[...] += jnp.dot(a, wd_ref[...], preferred_element_type=F32)

    @pl.when(pl.program_id(1) == 0)
    def _():
        x = x_ref[...]
        h_ref[...] = _rms_rows(x, g_ref[...]).astype(BF16)
        o_ref[...] = x
        accumulate()

    @pl.when(pl.program_id(1) > 0)
    def _():
        accumulate()


def _ffn(x, gain, wg, wu, wd, *, tm, tf, d_ff):
    m, d = x.shape
    return pl.pallas_call(
        functools.partial(_ffn_kernel, d_ff=d_ff),
        grid=(m // tm, wd.shape[0] // tf),
        in_specs=[
            pl.BlockSpec((tm, d), lambda i, f: (i, 0)),
            pl.BlockSpec((1, d), lambda i, f: (0, 0)),
            pl.BlockSpec((d, tf), lambda i, f: (0, f)),
            pl.BlockSpec((d, tf), lambda i, f: (0, f)),
            pl.BlockSpec((tf, d), lambda i, f: (f, 0)),
        ],
        out_specs=pl.BlockSpec((tm, d), lambda i, f: (i, 0)),
        out_shape=jax.ShapeDtypeStruct((m, d), F32),
        scratch_shapes=[pltpu.VMEM((tm, d), BF16)],
        compiler_params=_params(("parallel", "arbitrary")),
        name="ffn",
    )(x, gain.reshape(1, d), wg, wu, wd)


def _ffn_cast_kernel(x_ref, g_ref, wg_ref, wu_ref, wd_ref, o_ref, wgb_ref, wub_ref, wdb_ref, h_ref, *, d_ff):
    f = pl.program_id(1)
    tf = wg_ref.shape[1]
    wg = wg_ref[...].astype(BF16)
    wu = wu_ref[...].astype(BF16)
    row = f * tf + lax.broadcasted_iota(jnp.int32, wd_ref.shape, 0)
    wd = jnp.where(row < d_ff, wd_ref[...], 0.0).astype(BF16)
    wgb_ref[...] = wg
    wub_ref[...] = wu
    wdb_ref[...] = wd

    @pl.when(f == 0)
    def _():
        x = x_ref[...]
        h_ref[...] = _rms_rows(x, g_ref[...]).astype(BF16)
        o_ref[...] = x

    h = h_ref[...]
    g = jnp.dot(h, wg, preferred_element_type=F32)
    u = jnp.dot(h, wu, preferred_element_type=F32)
    col = f * tf + lax.broadcasted_iota(jnp.int32, g.shape, 1)
    a = jnp.where(col < d_ff, 0.5 * (g * jax.nn.sigmoid(g) * u), 0.0).astype(BF16)
    o_ref[...] += jnp.dot(a, wd, preferred_element_type=F32)


def _ffn_cast(x, gain, wg, wu, wd, *, tf):
    m, d = x.shape
    d_ff = wg.shape[1]
    nf = pl.cdiv(d_ff, tf)
    wide = jax.ShapeDtypeStruct((d, nf * tf), BF16)
    return pl.pallas_call(
        functools.partial(_ffn_cast_kernel, d_ff=d_ff),
        grid=(1, nf),
        in_specs=[
            pl.BlockSpec((m, d), lambda i, f: (0, 0)),
            pl.BlockSpec((1, d), lambda i, f: (0, 0)),
            pl.BlockSpec((d, tf), lambda i, f: (0, f)),
            pl.BlockSpec((d, tf), lambda i, f: (0, f)),
            pl.BlockSpec((tf, d), lambda i, f: (f, 0)),
        ],
        out_specs=[
            pl.BlockSpec((m, d), lambda i, f: (0, 0)),
            pl.BlockSpec((d, tf), lambda i, f: (0, f)),
            pl.BlockSpec((d, tf), lambda i, f: (0, f)),
            pl.BlockSpec((tf, d), lambda i, f: (f, 0)),
        ],
        out_shape=[jax.ShapeDtypeStruct((m, d), F32), wide, wide, jax.ShapeDtypeStruct((nf * tf, d), BF16)],
        scratch_shapes=[pltpu.VMEM((m, d), BF16)],
        compiler_params=_params(("arbitrary", "arbitrary")),
        name="ffn_cast",
    )(x, gain.reshape(1, d), wg, wu, wd)


_J_K, _J_V, _J_U, _J_GATE, _J_END = 3, 6, 9, 11, 19
EPILOGUE_ROWS = 256


def _inproj_kernel(x_ref, g_ref, w_ref, qkg_ref, cos_ref, sin_ref,
                   qk_ref, v_ref, kv0_ref, kv1_ref, kv2_ref, u_ref, gate_ref, h_ref, z_ref,
                   *, head_major, cache_keep):
    j = pl.program_id(1)
    rows = h_ref.shape[0]
    nh = HEADS_PER_GROUP
    kv_refs = (kv0_ref, kv1_ref, kv2_ref)

    @pl.when(j == 0)
    def _():
        h_ref[...] = _rms_rows(x_ref[...], g_ref[...]).astype(BF16)

    def zdot():
        return jnp.dot(h_ref[...], w_ref[...], preferred_element_type=F32)

    def put_heads(ref, heads, r0, n):
        if head_major:
            for h in range(nh):
                ref[h, pl.ds(r0, n), :] = heads[h]
        else:
            ref[pl.ds(r0, n), :] = jnp.concatenate(heads, axis=1)

    def put_cache(g, heads, r0, n):
        ref, keep = kv_refs[g], cache_keep[g]
        first = 0 if keep is None else rows - keep
        lo = max(r0, first)
        if lo >= r0 + n:
            return
        part = [hd[lo - r0:] for hd in heads]
        if head_major:
            for h in range(nh):
                ref[pl.ds((lo - first) * nh + h, r0 + n - lo, stride=nh), :] = part[h]
        else:
            ref[pl.ds(lo - first, r0 + n - lo), :] = jnp.concatenate(part, axis=1)

    def norm_rope_tile(tile):
        z_src = z_ref.at[tile % 2]
        gain = qkg_ref[...]
        n = min(EPILOGUE_ROWS, rows)
        for r0 in range(0, rows, n):
            cos, sin = cos_ref[pl.ds(r0, n), :], sin_ref[pl.ds(r0, n), :]
            heads = []
            for h in range(nh):
                y = _rms_rows(z_src[pl.ds(r0, n), _head(h)], gain[:, _head(h)])
                heads.append(y * cos + pltpu.roll(y, HEAD_DIM // 2, axis=1) * sin)
            put_heads(qk_ref, heads, r0, n)
            if tile >= _J_K:
                put_cache(tile - _J_K, heads, r0, n)

    def v_tile(g):
        z = zdot()
        heads = [z[:, _head(h)] for h in range(nh)]
        put_heads(v_ref, heads, 0, rows)
        put_cache(g, heads, 0, rows)

    for t in range(_J_V + 1):
        @pl.when(j == t)
        def _(t=t):
            if t < _J_V:
                z_ref[t % 2] = zdot()
            else:
                v_tile(0)
            if t > 0:
                norm_rope_tile(t - 1)

    for g in range(1, N_GROUPS):
        @pl.when(j == _J_V + g)
        def _(g=g):
            v_tile(g)

    @pl.when((j >= _J_U) & (j < _J_GATE))
    def _():
        u_ref[...] = zdot()

    @pl.when(j >= _J_GATE)
    def _():
        gate_ref[...] = zdot()


def _kv_slot(j, g, writes=True):
    return jnp.where(writes & (j > _J_K + g + 1), 1, 0)


def _inproj_common_specs(d, tm, nt):
    return [
        pl.BlockSpec((tm, d), lambda i, j: (i, 0)),
        pl.BlockSpec((1, d), lambda i, j: (0, 0)),
        pl.BlockSpec((d, COL_TILE), lambda i, j: (0, j)),
        pl.BlockSpec((1, COL_TILE), lambda i, j: (0, jnp.clip(j - 1, 0, _J_V - 1))),
        pl.BlockSpec((tm, HEAD_DIM), lambda i, j: (i % nt, 0)),
        pl.BlockSpec((tm, HEAD_DIM), lambda i, j: (i % nt, 0)),
    ]


def _inproj_prompt(x, gain, w, qk_gain, cos, sin, *, b, l, tm):
    m, d = x.shape
    nt = l // tm
    ssm_w = (_J_GATE - _J_U) * COL_TILE
    gate_w = (_J_END - _J_GATE) * COL_TILE
    u_tiles = _J_GATE - _J_U
    nh = HEADS_PER_GROUP
    out_specs = [
        pl.BlockSpec((None, nh, tm, HEAD_DIM), lambda i, j: (i // nt, jnp.clip(j - 1, 0, _J_V - 1), i % nt, 0)),
        pl.BlockSpec((None, nh, tm, HEAD_DIM), lambda i, j: (i // nt, jnp.clip(j - _J_V, 0, N_GROUPS - 1), i % nt, 0)),
    ]
    cache_keep = tuple(None if window >= l else window for window, _ in ATT_GROUPS)
    for g, keep in enumerate(cache_keep):
        if keep is None:
            out_specs.append(pl.BlockSpec((None, None, tm * nh, HEAD_DIM),
                                          lambda i, j, g=g: (i // nt, _kv_slot(j, g), i % nt, 0)))
        else:
            assert keep <= tm
            out_specs.append(pl.BlockSpec((None, None, keep * nh, HEAD_DIM),
                                          lambda i, j, g=g: (i // nt, _kv_slot(j, g, i % nt == nt - 1), 0, 0)))
    out_specs.append(pl.BlockSpec(
        (tm, COL_TILE), lambda i, j: (i % nt, (i // nt) * u_tiles + jnp.clip(j - _J_U, 0, u_tiles - 1))))
    out_specs.append(pl.BlockSpec(
        (tm, COL_TILE), lambda i, j: (i, jnp.clip(j - _J_GATE, 0, _J_END - _J_GATE - 1))))
    out_shape = [jax.ShapeDtypeStruct((b, _J_V * nh, l, HEAD_DIM), F32),
                 jax.ShapeDtypeStruct((b, N_GROUPS * nh, l, HEAD_DIM), F32)]
    out_shape += [jax.ShapeDtypeStruct((b, 2, (l if keep is None else keep) * nh, HEAD_DIM), F32)
                  for keep in cache_keep]
    out_shape += [jax.ShapeDtypeStruct((l, b * ssm_w), F32), jax.ShapeDtypeStruct((m, gate_w), F32)]
    return pl.pallas_call(
        functools.partial(_inproj_kernel, head_major=True, cache_keep=cache_keep),
        grid=(m // tm, _J_END),
        in_specs=_inproj_common_specs(d, tm, nt),
        out_specs=out_specs,
        out_shape=out_shape,
        scratch_shapes=[pltpu.VMEM((tm, d), BF16), pltpu.VMEM((2, tm, COL_TILE), F32)],
        compiler_params=_params(("arbitrary", "arbitrary")),
        name="inproj_prompt",
    )(x, gain.reshape(1, d), w, qk_gain, cos, sin)


def _inproj_sample(x, gain, w, qk_gain, cos, sin):
    m, d = x.shape
    u_tiles = _J_GATE - _J_U
    out_specs = [pl.BlockSpec((m, COL_TILE), lambda i, j: (0, jnp.clip(j - 1, 0, _J_V - 1))),
                 pl.BlockSpec((m, COL_TILE), lambda i, j: (0, jnp.clip(j - _J_V, 0, N_GROUPS - 1)))]
    for g in range(N_GROUPS):
        out_specs.append(pl.BlockSpec((None, m, COL_TILE), lambda i, j, g=g: (_kv_slot(j, g), 0, 0)))
    out_specs.append(pl.BlockSpec((m, COL_TILE), lambda i, j: (0, jnp.clip(j - _J_U, 0, u_tiles - 1))))
    out_specs.append(pl.BlockSpec((m, COL_TILE), lambda i, j: (0, jnp.clip(j - _J_GATE, 0, _J_END - _J_GATE - 1))))
    out_shape = [jax.ShapeDtypeStruct((m, _J_V * COL_TILE), F32), jax.ShapeDtypeStruct((m, N_GROUPS * COL_TILE), F32)]
    out_shape += [jax.ShapeDtypeStruct((2, m, GROUP_WIDTH), F32)] * N_GROUPS
    out_shape += [jax.ShapeDtypeStruct((m, u_tiles * COL_TILE), F32),
                  jax.ShapeDtypeStruct((m, (_J_END - _J_GATE) * COL_TILE), F32)]
    return pl.pallas_call(
        functools.partial(_inproj_kernel, head_major=False, cache_keep=(None,) * N_GROUPS),
        grid=(1, _J_END),
        in_specs=_inproj_common_specs(d, m, 1),
        out_specs=out_specs,
        out_shape=out_shape,
        scratch_shapes=[pltpu.VMEM((m, d), BF16), pltpu.VMEM((2, m, COL_TILE), F32)],
        compiler_params=_params(("arbitrary", "arbitrary")),
        name="inproj_sample",
    )(x, gain.reshape(1, d), w, qk_gain, cos, sin)


def _rope_tables(pos):
    half = HEAD_DIM // 2
    inv = jnp.power(ROPE_THETA, -jnp.arange(half, dtype=F32) * (2.0 / HEAD_DIM))
    ang = pos.astype(F32)[:, None] * inv[None, :]
    cos, sin = jnp.cos(ang), jnp.sin(ang)
    return jnp.concatenate([cos, cos], axis=1), jnp.concatenate([-sin, sin], axis=1)


ATTN_UNROLL = 16


def _attn_prompt_kernel(*refs):
    qkv = refs[:3 * N_GROUPS]
    o_ref = refs[3 * N_GROUPS]
    stats = (refs[3 * N_GROUPS + 1:3 * N_GROUPS + 4], refs[3 * N_GROUPS + 4:3 * N_GROUPS + 7])
    seq = o_ref.shape[0]
    row = lax.broadcasted_iota(jnp.int32, (QBLOCK, QBLOCK), 0)
    col = lax.broadcasted_iota(jnp.int32, (QBLOCK, QBLOCK), 1)
    cur_mask = col <= row
    prev_mask = col >= row
    nt_dims = (((1,), (1,)), ((), ()))
    full = (QBLOCK, HEAD_DIM)
    ones = jnp.ones(full, BF16)

    for g, (_, dil) in enumerate(ATT_GROUPS):
        q_ref, k_ref, v_ref = qkv[3 * g:3 * g + 3]
        nblk = seq // dil // QBLOCK
        span = QBLOCK * dil

        def rows_at(start, dil=dil):
            return pl.ds(start, QBLOCK) if dil == 1 else pl.ds(start, QBLOCK, stride=dil)

        def block(r, n, q_ref=q_ref, k_ref=k_ref, v_ref=v_ref, nblk=nblk, span=span, rows_at=rows_at):
            start = r + n * span
            rows = rows_at(start)
            q = q_ref[rows, :].astype(BF16)
            k = k_ref[rows, :].astype(BF16)
            v = jnp.concatenate([v_ref[rows, :].astype(BF16), ones], axis=1)
            if nblk > 1:
                prows = rows_at(jnp.maximum(start - span, r))
                k = jnp.concatenate([k, k_ref[prows, :].astype(BF16)], axis=0)
                vp = jnp.concatenate([v_ref[prows, :].astype(BF16), ones], axis=1)
                v = jnp.concatenate([v, vp], axis=0)
            s = lax.dot_general(q, k, nt_dims, preferred_element_type=F32) * ATT_SCALE
            if nblk > 1:
                prev_pen = jnp.where(n > 0, 0.0, -jnp.inf)
                s = jnp.concatenate([jnp.where(cur_mask, s[:, :QBLOCK], -jnp.inf),
                                     jnp.where(prev_mask, s[:, QBLOCK:], -jnp.inf) + prev_pen], axis=1)
                mx = jnp.max(jnp.maximum(s[:, :QBLOCK], s[:, QBLOCK:]), axis=1, keepdims=True)
            else:
                s = jnp.where(cur_mask, s, -jnp.inf)
                mx = jnp.max(s, axis=1, keepdims=True)
            p = jnp.exp(s - mx).astype(BF16)
            acc = jnp.dot(p, v, preferred_element_type=F32)
            return rows, acc[:, :HEAD_DIM], jnp.broadcast_to(mx, full), acc[:, HEAD_DIM:]

        src = stats[(g - 1) % 2] if g > 0 else None
        dst = stats[g % 2] if g < N_GROUPS - 1 else None

        def merge(rows, acc, mb, lb, src=src, dst=dst):
            if src is not None:
                m_old = src[1][rows, :]
                m_new = jnp.maximum(m_old, mb)
                a_old, a_new = jnp.exp(m_old - m_new), jnp.exp(mb - m_new)
                acc = a_old * src[0][rows, :] + a_new * acc
                lb = a_old * src[2][rows, :] + a_new * lb
                mb = m_new
            if dst is not None:
                dst[0][rows, :] = acc
                dst[1][rows, :] = mb
                dst[2][rows, :] = lb
            else:
                o_ref[rows, :] = acc / lb

        def body(i, carry, dil=dil, block=block, merge=merge):
            results = []
            for d in range(ATTN_UNROLL):
                idx = i * ATTN_UNROLL + d
                results.append(block(idx % dil, idx // dil))
            for res in results:
                merge(*res)
            return carry

        lax.fori_loop(0, dil * nblk // ATTN_UNROLL, body, 0)


def _attn_prompt(qk, v, *, b, l):
    nh = HEADS_PER_GROUP
    in_specs, args = [], []
    for g in range(N_GROUPS):
        for part in range(3):
            in_specs.append(pl.BlockSpec(
                (None, None, l, HEAD_DIM),
                lambda bi, hs, g=g, part=part: (bi, (part % 2) * N_GROUPS * nh + g * nh + hs, 0, 0)))
            args.append(v if part == 2 else qk)
    return pl.pallas_call(
        _attn_prompt_kernel,
        grid=(b, nh),
        in_specs=in_specs,
        out_specs=pl.BlockSpec((None, None, l, HEAD_DIM), lambda bi, hs: (bi, hs, 0, 0)),
        out_shape=jax.ShapeDtypeStruct((b, nh, l, HEAD_DIM), F32),
        scratch_shapes=[pltpu.VMEM((l, HEAD_DIM), F32)] * 6,
        compiler_params=_params(("parallel", "parallel")),
        name="attn_prompt",
    )(*args)


def _attn_decode_kernel(q_ref, *refs):
    o_ref = refs[-1]
    outs, lses = [], []
    for g in range(N_GROUPS):
        kc_ref, vc_ref, kn_ref, vn_ref = refs[4 * g:4 * g + 4]
        qg = q_ref[g][None]
        s = jnp.sum(kc_ref[...] * qg, axis=-1, keepdims=True) * ATT_SCALE
        s_new = jnp.sum(kn_ref[...] * qg, axis=-1, keepdims=True) * ATT_SCALE
        mx = jnp.maximum(jnp.max(s, axis=0, keepdims=True), s_new)
        p = jnp.exp(s - mx)
        p_new = jnp.exp(s_new - mx)
        den = jnp.sum(p, axis=0, keepdims=True) + p_new
        acc = jnp.sum(p * vc_ref[...], axis=0, keepdims=True) + p_new * vn_ref[...]
        outs.append(acc / den)
        lses.append(mx + jnp.log(den))
    mx = functools.reduce(jnp.maximum, lses)
    ws = [jnp.exp(lse - mx) for lse in lses]
    num = functools.reduce(lambda a, c: a + c, [w * o for w, o in zip(ws, outs)])
    o_ref[...] = (num / functools.reduce(lambda a, c: a + c, ws))[0]


def _attn_decode(q, caches, kvnews):
    b = q.shape[0]
    nh = HEADS_PER_GROUP
    in_specs = [pl.BlockSpec((None, N_GROUPS, nh, HEAD_DIM), lambda i: (i, 0, 0, 0))]
    args = [q]
    for g, (window, dil) in enumerate(ATT_GROUPS):
        cache = caches[g]
        assert cache.shape[2] == window and window == N_BACK * dil
        cv = cache.reshape(b, 2, N_BACK, dil, nh, HEAD_DIM)
        in_specs += [
            pl.BlockSpec((None, None, N_BACK, None, nh, HEAD_DIM), lambda i: (i, 0, 0, 0, 0, 0)),
            pl.BlockSpec((None, None, N_BACK, None, nh, HEAD_DIM), lambda i: (i, 1, 0, 0, 0, 0)),
            pl.BlockSpec((None, None, 1, nh, HEAD_DIM), lambda i: (0, i, 0, 0, 0)),
            pl.BlockSpec((None, None, 1, nh, HEAD_DIM), lambda i: (1, i, 0, 0, 0)),
        ]
        args += [cv, cv, kvnews[g], kvnews[g]]
    return pl.pallas_call(
        _attn_decode_kernel,
        grid=(b,),
        in_specs=in_specs,
        out_specs=pl.BlockSpec((None, nh, HEAD_DIM), lambda i: (i, 0, 0)),
        out_shape=jax.ShapeDtypeStruct((b, nh, HEAD_DIM), F32),
        compiler_params=_params(("parallel",)),
        name="attn_decode",
    )(*args)


def _cache_shift_kernel(c_ref, new_ref, o_ref):
    w = c_ref.shape[0]
    o_ref[pl.ds(0, w - 1)] = c_ref[pl.ds(1, w - 1)]
    o_ref[pl.ds(w - 1, 1)] = new_ref[...]


def _cache_shift(cache, kvnew):
    b, _, w, nh, e = cache.shape
    return pl.pallas_call(
        _cache_shift_kernel,
        grid=(b, 2),
        in_specs=[
            pl.BlockSpec((None, None, w, nh, e), lambda i, s: (i, s, 0, 0, 0)),
            pl.BlockSpec((None, None, 1, nh, e), lambda i, s: (s, i, 0, 0, 0)),
        ],
        out_specs=pl.BlockSpec((None, None, w, nh, e), lambda i, s: (i, s, 0, 0, 0)),
        out_shape=jax.ShapeDtypeStruct(cache.shape, F32),
        compiler_params=_params(("parallel", "parallel")),
        name=f"cache_shift_w{w}",
    )(cache, kvnew)


def _ssm_prep_kernel(lr_ref, li_ref, ldt_ref, br_ref, bi_ref, cim_ref,
                     ar_ref, ai_ref, bbr_ref, bbi_ref, ncim_ref):
    lr, li = lr_ref[...], li_ref[...]
    dt = jnp.exp(ldt_ref[...])
    mag = jnp.exp(lr * dt)
    ar = mag * jnp.cos(li * dt)
    ai = mag * jnp.sin(li * dt)
    den = lr * lr + li * li
    fr = ((ar - 1.0) * lr + ai * li) / den
    fi = (ai * lr - (ar - 1.0) * li) / den
    br, bi = br_ref[...], bi_ref[...]
    ar_ref[...] = ar
    ai_ref[...] = ai
    bbr_ref[...] = fr * br - fi * bi
    bbi_ref[...] = fr * bi + fi * br
    ncim_ref[...] = -cim_ref[...]


def _ssm_prepare(lam_re, lam_im, log_dt, b_re, b_im, c_re, c_im):
    gn, pn = lam_re.shape
    cn = SSM_GROUP
    rows = gn * cn
    rep = lambda a: jnp.broadcast_to(a[:, None, :], (gn, cn, pn)).reshape(rows, pn)
    tr = lambda a: jnp.transpose(a, (0, 2, 1)).reshape(rows, pn)
    ldt = jnp.broadcast_to(log_dt[:, None, None], (gn, cn, pn)).reshape(rows, pn)
    shp = jax.ShapeDtypeStruct((rows, pn), F32)
    ar, ai, bbr, bbi, ncim = pl.pallas_call(
        _ssm_prep_kernel, out_shape=[shp] * 5, name="ssm_prep",
    )(rep(lam_re), rep(lam_im), ldt, tr(b_re), tr(b_im), c_im.reshape(rows, pn))
    nq, gb = gn // SSM_GROUPS_PER_BLOCK, SSM_GROUPS_PER_BLOCK
    diag = (jnp.arange(gb)[:, None, None, None] == jnp.arange(gb)[None, None, :, None])
    spread = lambda a: jnp.where(diag, a.reshape(2, nq, gb, cn, 1, pn), 0.0)
    bb = spread(jnp.stack([bbr, bbi]))
    bblk = jnp.transpose(bb, (1, 2, 3, 0, 4, 5)).reshape(nq, gb * cn, 2 * gb * pn).astype(BF16)
    cc = spread(jnp.stack([c_re.reshape(rows, pn), ncim]))
    cblk = jnp.transpose(cc, (1, 0, 2, 5, 4, 3)).reshape(nq, 2 * gb * pn, gb * cn).astype(BF16)
    a_re = ar.reshape(gn, cn, pn)[:, 0, :].reshape(gn * pn // LANES, LANES)
    a_im = ai.reshape(gn, cn, pn)[:, 0, :].reshape(gn * pn // LANES, LANES)
    return a_re, a_im, bblk, cblk


def _ssm_kernel(u_ref, x0_ref, bblk_ref, cblk_ref, are_ref, aim_ref, d_ref,
                y_ref, sfin_ref, ut_ref, yt_ref, bu_ref, xs_ref, st_ref, *, nb, steps):
    c = pl.program_id(0)
    nq = bblk_ref.shape[0]
    tiles_per_q = 2 * SSM_GROUPS_PER_BLOCK * SSM_STATE // LANES
    half = tiles_per_q // 2

    @pl.when(c == 0)
    def _():
        for j in range(nq * tiles_per_q):
            st_ref[j] = x0_ref[:, j * LANES:(j + 1) * LANES]

    width = nq * LANES
    for s in range(nb if steps > 1 else 1):
        for q in range(nq):
            if steps > 1:
                ut_ref[q, pl.ds(s, steps, stride=nb), :] = u_ref[:, s * width + q * LANES:s * width + (q + 1) * LANES]
            else:
                ut_ref[q] = u_ref[:, q * LANES:(q + 1) * LANES]
    for q in range(nq):
        res = jnp.dot(ut_ref[q].astype(BF16), bblk_ref[q], preferred_element_type=F32)
        for j in range(tiles_per_q):
            bu_ref[q * tiles_per_q + j] = res[:, j * LANES:(j + 1) * LANES]

    def scan_block(q, carry):
        for lt in range(half):
            jr = q * tiles_per_q + lt
            ji = jr + half
            ar = are_ref[q * half + lt]
            ai = aim_ref[q * half + lt]
            xr, xi = st_ref[jr], st_ref[ji]
            for t in range(steps):
                rows = pl.ds(t * nb, nb)
                xr, xi = (ar * xr - ai * xi + bu_ref[jr, rows, :],
                          ar * xi + ai * xr + bu_ref[ji, rows, :])
                xs_ref[jr, rows, :] = xr.astype(BF16)
                xs_ref[ji, rows, :] = xi.astype(BF16)
            st_ref[jr] = xr
            st_ref[ji] = xi
        return carry

    for q in range(nq):
        scan_block(q, 0)

    for q in range(nq):
        xq = jnp.concatenate([xs_ref[q * tiles_per_q + j] for j in range(tiles_per_q)], axis=1)
        yq = jnp.dot(xq, cblk_ref[q], preferred_element_type=F32)
        sl = slice(q * LANES, (q + 1) * LANES)
        yq = yq + d_ref[:, sl] * ut_ref[q]
        if steps > 1:
            yt_ref[q] = yq
        else:
            y_ref[:, sl] = yq
    if steps > 1:
        for s in range(nb):
            for q in range(nq):
                y_ref[:, s * width + q * LANES:s * width + (q + 1) * LANES] = yt_ref[q, pl.ds(s, steps, stride=nb), :]

    @pl.when(c == pl.num_programs(0) - 1)
    def _():
        for j in range(nq * tiles_per_q):
            sfin_ref[:, j * LANES:(j + 1) * LANES] = st_ref[j]


def _ssm(u, x0, a_re, a_im, bblk, cblk, d_skip, *, nb, steps):
    width = d_skip.shape[0]
    nsteps = u.shape[0] // steps if steps > 1 else 1
    blk = (steps, nb * width) if steps > 1 else (nb, width)
    rows = steps * nb
    ntile = a_re.shape[0]
    nstate = 2 * ntile * LANES
    are = jnp.broadcast_to(a_re[:, None, :], (ntile, nb, LANES))
    aim = jnp.broadcast_to(a_im[:, None, :], (ntile, nb, LANES))
    const = lambda shape: pl.BlockSpec(shape, lambda c: (0,) * len(shape))
    return pl.pallas_call(
        functools.partial(_ssm_kernel, nb=nb, steps=steps),
        grid=(nsteps,),
        in_specs=[
            pl.BlockSpec(blk, lambda c: (c, 0)),
            const((nb, nstate)),
            const(bblk.shape),
            const(cblk.shape),
            const(are.shape),
            const(aim.shape),
            const((1, width)),
        ],
        out_specs=[pl.BlockSpec(blk, lambda c: (c, 0)), const((nb, nstate))],
        out_shape=[jax.ShapeDtypeStruct(u.shape, F32), jax.ShapeDtypeStruct((nb, nstate), F32)],
        scratch_shapes=[
            pltpu.VMEM((width // LANES, rows, LANES), F32),
            pltpu.VMEM((width // LANES, rows, LANES), F32),
            pltpu.VMEM((2 * ntile, rows, LANES), F32),
            pltpu.VMEM((2 * ntile, rows, LANES), BF16),
            pltpu.VMEM((2 * ntile, nb, LANES), F32),
        ],
        compiler_params=_params(("arbitrary",)),
        name=f"ssm_nb{nb}",
    )(u, x0, bblk, cblk, are, aim, d_skip.reshape(1, width))


def _state_to_lanes(s_re, s_im):
    b, gn, pn = s_re.shape
    nq = gn // SSM_GROUPS_PER_BLOCK
    st = jnp.stack([s_re.reshape(b, nq, -1), s_im.reshape(b, nq, -1)], axis=2)
    return st.reshape(b, 2 * gn * pn)


def _lanes_to_state(s, gn, pn):
    b = s.shape[0]
    nq = gn // SSM_GROUPS_PER_BLOCK
    st = s.reshape(b, nq, 2, SSM_GROUPS_PER_BLOCK * pn)
    return st[:, :, 0].reshape(b, gn, pn), st[:, :, 1].reshape(b, gn, pn)


def _mix_kernel(x_ref, att_ref, y_ref, ga_ref, gs_ref, gw_ref, gb_ref, wa_ref, ws_ref, wo_ref, out_ref):
    att = jnp.concatenate([att_ref[h] for h in range(HEADS_PER_GROUP)], axis=1)
    a_proj = jnp.dot(att.astype(BF16), wa_ref[...], preferred_element_type=F32)
    zs = jax.nn.gelu(y_ref[...])
    glu = jnp.dot(zs.astype(BF16), gw_ref[...], preferred_element_type=F32) + gb_ref[...]
    s_out = zs * jax.nn.sigmoid(glu)
    s_proj = jnp.dot(s_out.astype(BF16), ws_ref[...], preferred_element_type=F32)
    merged = jax.nn.sigmoid(ga_ref[...]) * a_proj + jax.nn.sigmoid(gs_ref[...]) * s_proj
    out_ref[...] = x_ref[...] + jnp.dot(merged.astype(BF16), wo_ref[...], preferred_element_type=F32)


def _mix(x, att, y, y_map, gates, glu_w, glu_b, wa, ws, wo, *, tm, nt):
    m, d = x.shape
    sw = glu_w.shape[0]
    row = lambda width: pl.BlockSpec((tm, width), lambda i: (i, 0))
    const = lambda a: pl.BlockSpec(a.shape, lambda i: (0, 0), pipeline_mode=pl.Buffered(1))
    glu_b = glu_b.reshape(1, sw)
    in_specs = [row(d),
                pl.BlockSpec((None, HEADS_PER_GROUP, tm, HEAD_DIM), lambda i: (i // nt, 0, i % nt, 0)),
                pl.BlockSpec((tm, sw), y_map),
                pl.BlockSpec((tm, d), lambda i: (i, 0)), pl.BlockSpec((tm, d), lambda i: (i, 1)),
                const(glu_w), const(glu_b), const(wa), const(ws), const(wo)]
    return pl.pallas_call(
        _mix_kernel,
        grid=(m // tm,),
        in_specs=in_specs,
        out_specs=row(d),
        out_shape=jax.ShapeDtypeStruct((m, d), F32),
        compiler_params=_params(("parallel",)),
        name="mix",
    )(x, att, y, gates, gates, glu_w, glu_b, wa, ws, wo)


FFN_TF = 512
PROMPT_TM_FFN = 1024
PROMPT_TM_INPROJ = 1024
PROMPT_TM_MIX = 256
SSM_STEPS = 16


def _layer_weights(p):
    w = dict(p)
    w["w_in_b"] = p["w_in"].astype(BF16)
    w["qk_gain"] = jnp.concatenate([p["q_norm"].reshape(1, -1), p["k_norm"].reshape(1, -1)], axis=1)
    w["ssm"] = _ssm_prepare(p["ssm_lambda_re"], p["ssm_lambda_im"], p["ssm_log_dt"],
                            p["ssm_b_re"], p["ssm_b_im"], p["ssm_c_re"], p["ssm_c_im"])
    for name in ("glu_w", "w_attn_branch", "w_ssm_branch", "w_out"):
        w[name + "_b"] = p[name].astype(BF16)
    return w


def _prompt_layer(x, w, ffn_w):
    b, l, d = x.shape
    m = b * l
    nh = HEADS_PER_GROUP
    gn, pn = w["ssm_lambda_re"].shape
    d_ff = w["ffn1_w_gate"].shape[1]
    x1 = _ffn(x.reshape(m, d), w["ffn1_norm"], *ffn_w[0], tm=PROMPT_TM_FFN, tf=FFN_TF, d_ff=d_ff)
    cos, sin = _rope_tables(jnp.arange(l, dtype=jnp.int32))
    tm_in = min(PROMPT_TM_INPROJ, l)
    qk, v, kv0, kv1, kv2, u_t, gates = _inproj_prompt(
        x1, w["mix_norm"], w["w_in_b"], w["qk_gain"], cos, sin, b=b, l=l, tm=tm_in)
    att = _attn_prompt(qk, v, b=b, l=l)
    a_re, a_im, bblk, cblk = w["ssm"]
    y_t, sfin = _ssm(u_t, jnp.zeros((b, 2 * gn * pn), F32),
                     a_re, a_im, bblk, cblk, w["ssm_d"], nb=b, steps=SSM_STEPS)
    tm = PROMPT_TM_MIX
    nt = l // tm
    x2 = _mix(x1, att, y_t, lambda i: (i % nt, i // nt), gates,
              w["glu_w_b"], w["glu_b"], w["w_attn_branch_b"], w["w_ssm_branch_b"], w["w_out_b"], tm=tm, nt=nt)
    y = _ffn(x2, w["ffn2_norm"], *ffn_w[1], tm=PROMPT_TM_FFN, tf=FFN_TF, d_ff=d_ff)
    new_kv = []
    for g, (window, _) in enumerate(ATT_GROUPS):
        new_kv.append((kv0, kv1, kv2)[g].reshape(b, 2, min(window, l), nh, HEAD_DIM))
    s_re, s_im = _lanes_to_state(sfin, gn, pn)
    return y.reshape(b, l, d), new_kv, (s_re, s_im)


def _sample_layer(x, pos0, caches, state, w):
    b, l, d = x.shape
    assert l == 1
    nh = HEADS_PER_GROUP
    gn, pn = w["ssm_lambda_re"].shape
    x1, *ffn1_w = _ffn_cast(x.reshape(b, d), w["ffn1_norm"], w["ffn1_w_gate"], w["ffn1_w_up"], w["ffn1_w_down"],
                            tf=FFN_TF)
    cos, sin = _rope_tables(jnp.full((b,), pos0, dtype=jnp.int32))
    qk, _, kn0, kn1, kn2, u, gates = _inproj_sample(x1, w["mix_norm"], w["w_in_b"], w["qk_gain"], cos, sin)
    q = qk[:, :ATT_WIDTH].reshape(b, N_GROUPS, nh, HEAD_DIM)
    kvnews = [kn.reshape(2, b, 1, nh, HEAD_DIM) for kn in (kn0, kn1, kn2)]
    att = _attn_decode(q, caches, kvnews)
    new_kv = [_cache_shift(caches[g], kvnews[g]) for g in range(N_GROUPS)]
    a_re, a_im, bblk, cblk = w["ssm"]
    y, sfin = _ssm(u, _state_to_lanes(*state), a_re, a_im, bblk, cblk, w["ssm_d"], nb=b, steps=1)
    att_h = jnp.transpose(att, (1, 0, 2))[None]
    x2 = _mix(x1, att_h, y, lambda i: (i, 0), gates,
              w["glu_w_b"], w["glu_b"], w["w_attn_branch_b"], w["w_ssm_branch_b"], w["w_out_b"], tm=b, nt=1)
    out, *ffn2_w = _ffn_cast(x2, w["ffn2_norm"], w["ffn2_w_gate"], w["ffn2_w_up"], w["ffn2_w_down"], tf=FFN_TF)
    s_re, s_im = _lanes_to_state(sfin, gn, pn)
    return out.reshape(b, l, d), new_kv, (s_re, s_im), (ffn1_w, ffn2_w)


def kernel(x_prompt, x_sample, cache_kv_w128, cache_kv_w512, cache_kv_w2048, state_ssm_re, state_ssm_im,
           ffn1_norm, ffn1_w_gate, ffn1_w_up, ffn1_w_down, mix_norm, w_in, q_norm, k_norm,
           ssm_lambda_re, ssm_lambda_im, ssm_log_dt, ssm_b_re, ssm_b_im, ssm_c_re, ssm_c_im, ssm_d,
           glu_w, glu_b, w_attn_branch, w_ssm_branch, w_out,
           ffn2_norm, ffn2_w_gate, ffn2_w_up, ffn2_w_down):
    depth = w_in.shape[0]
    params = dict(ffn1_norm=ffn1_norm, ffn1_w_gate=ffn1_w_gate, ffn1_w_up=ffn1_w_up, ffn1_w_down=ffn1_w_down,
                  mix_norm=mix_norm, w_in=w_in, q_norm=q_norm, k_norm=k_norm,
                  ssm_lambda_re=ssm_lambda_re, ssm_lambda_im=ssm_lambda_im, ssm_log_dt=ssm_log_dt,
                  ssm_b_re=ssm_b_re, ssm_b_im=ssm_b_im, ssm_c_re=ssm_c_re, ssm_c_im=ssm_c_im, ssm_d=ssm_d,
                  glu_w=glu_w, glu_b=glu_b, w_attn_branch=w_attn_branch, w_ssm_branch=w_ssm_branch,
                  w_out=w_out, ffn2_norm=ffn2_norm, ffn2_w_gate=ffn2_w_gate, ffn2_w_up=ffn2_w_up,
                  ffn2_w_down=ffn2_w_down)
    yp, ys = x_prompt, x_sample
    new_p = [[] for _ in range(5)]
    new_s = [[] for _ in range(5)]
    for layer in range(depth):
        w = _layer_weights({k: v[layer] for k, v in params.items()})
        ys, kv_s, st_s, ffn_w = _sample_layer(
            ys, PAST_LEN, (cache_kv_w128[layer], cache_kv_w512[layer], cache_kv_w2048[layer]),
            (state_ssm_re[layer], state_ssm_im[layer]), w)
        yp, kv_p, st_p = _prompt_layer(yp, w, ffn_w)
        for i, a in enumerate(list(kv_p) + list(st_p)):
            new_p[i].append(a)
        for i, a in enumerate(list(kv_s) + list(st_s)):
            new_s[i].append(a)
    outs_p = [jnp.stack(a) for a in new_p]
    outs_s = [jnp.stack(a) for a in new_s]
    return (yp, ys, *outs_p, *outs_s)
```

```python
import functools

import jax
import jax.numpy as jnp
from jax import lax
from jax.experimental import pallas as pl
from jax.experimental.pallas import tpu as pltpu

F32 = jnp.float32
BF16 = jnp.bfloat16

HEAD_DIM = 128
HEADS_PER_GROUP = 4
GROUP_WIDTH = HEADS_PER_GROUP * HEAD_DIM
ATT_GROUPS = ((128, 1), (512, 4), (2048, 16))
N_GROUPS = len(ATT_GROUPS)
ATT_WIDTH = N_GROUPS * GROUP_WIDTH
N_BACK = 128
QBLOCK = 128
ATT_SCALE = HEAD_DIM ** -0.5
ROPE_THETA = 10000.0
RMS_EPS = 1e-6
PAST_LEN = 16384
SSM_GROUP = 16
SSM_STATE = 64
SSM_GROUPS_PER_BLOCK = 8
LANES = 128
COL_TILE = 512
VMEM_LIMIT = 56 * 1024 * 1024


def _params(semantics):
    return pltpu.CompilerParams(dimension_semantics=semantics, vmem_limit_bytes=VMEM_LIMIT)


def _rms_rows(x, gain):
    ms = jnp.mean(x * x, axis=-1, keepdims=True)
    return x * lax.rsqrt(ms + RMS_EPS) * gain


def _head(h):
    return slice(h * HEAD_DIM, (h + 1) * HEAD_DIM)


def _ffn_kernel(x_ref, g_ref, wg_ref, wu_ref, wd_ref, o_ref, h_ref, *, d_ff):
    def accumulate():
        h = h_ref[...]
        g = jnp.dot(h, wg_ref[...], preferred_element_type=F32)
        u = jnp.dot(h, wu_ref[...], preferred_element_type=F32)
        col = pl.program_id(1) * g.shape[1] + lax.broadcasted_iota(jnp.int32, g.shape, 1)
        a = jnp.where(col < d_ff, 0.5 * (g * jax.nn.sigmoid(g) * u), 0.0).astype(BF16)
        o_ref[...] += jnp.dot(a, wd_ref[...], preferred_element_type=F32)

    @pl.when(pl.program_id(1) == 0)
    def _():
        x = x_ref[...]
        h_ref[...] = _rms_rows(x, g_ref[...]).astype(BF16)
        o_ref[...] = x
        accumulate()

    @pl.when(pl.program_id(1) > 0)
    def _():
        accumulate()


def _ffn(x, gain, wg, wu, wd, *, tm, tf, d_ff):
    m, d = x.shape
    return pl.pallas_call(
        functools.partial(_ffn_kernel, d_ff=d_ff),
        grid=(m // tm, wd.shape[0] // tf),
        in_specs=[
            pl.BlockSpec((tm, d), lambda i, f: (i, 0)),
            pl.BlockSpec((1, d), lambda i, f: (0, 0)),
            pl.BlockSpec((d, tf), lambda i, f: (0, f)),
            pl.BlockSpec((d, tf), lambda i, f: (0, f)),
            pl.BlockSpec((tf, d), lambda i, f: (f, 0)),
        ],
        out_specs=pl.BlockSpec((tm, d), lambda i, f: (i, 0)),
        out_shape=jax.ShapeDtypeStruct((m, d), F32),
        scratch_shapes=[pltpu.VMEM((tm, d), BF16)],
        compiler_params=_params(("parallel", "arbitrary")),
        name="ffn",
    )(x, gain.reshape(1, d), wg, wu, wd)


def _ffn_cast_kernel(x_ref, g_ref, wg_ref, wu_ref, wd_ref, o_ref, wgb_ref, wub_ref, wdb_ref, h_ref, *, d_ff):
    f = pl.program_id(1)
    tf = wg_ref.shape[1]
    wg = wg_ref[...].astype(BF16)
    wu = wu_ref[...].astype(BF16)
    row = f * tf + lax.broadcasted_iota(jnp.int32, wd_ref.shape, 0)
    wd = jnp.where(row < d_ff, wd_ref[...], 0.0).astype(BF16)
    wgb_ref[...] = wg
    wub_ref[...] = wu
    wdb_ref[...] = wd

    @pl.when(f == 0)
    def _():
        x = x_ref[...]
        h_ref[...] = _rms_rows(x, g_ref[...]).astype(BF16)
        o_ref[...] = x

    h = h_ref[...]
    g = jnp.dot(h, wg, preferred_element_type=F32)
    u = jnp.dot(h, wu, preferred_element_type=F32)
    col = f * tf + lax.broadcasted_iota(jnp.int32, g.shape, 1)
    a = jnp.where(col < d_ff, 0.5 * (g * jax.nn.sigmoid(g) * u), 0.0).astype(BF16)
    o_ref[...] += jnp.dot(a, wd, preferred_element_type=F32)


def _ffn_cast(x, gain, wg, wu, wd, *, tf):
    m, d = x.shape
    d_ff = wg.shape[1]
    nf = pl.cdiv(d_ff, tf)
    wide = jax.ShapeDtypeStruct((d, nf * tf), BF16)
    return pl.pallas_call(
        functools.partial(_ffn_cast_kernel, d_ff=d_ff),
        grid=(1, nf),
        in_specs=[
            pl.BlockSpec((m, d), lambda i, f: (0, 0)),
            pl.BlockSpec((1, d), lambda i, f: (0, 0)),
            pl.BlockSpec((d, tf), lambda i, f: (0, f)),
            pl.BlockSpec((d, tf), lambda i, f: (0, f)),
            pl.BlockSpec((tf, d), lambda i, f: (f, 0)),
        ],
        out_specs=[
            pl.BlockSpec((m, d), lambda i, f: (0, 0)),
            pl.BlockSpec((d, tf), lambda i, f: (0, f)),
            pl.BlockSpec((d, tf), lambda i, f: (0, f)),
            pl.BlockSpec((tf, d), lambda i, f: (f, 0)),
        ],
        out_shape=[jax.ShapeDtypeStruct((m, d), F32), wide, wide, jax.ShapeDtypeStruct((nf * tf, d), BF16)],
        scratch_shapes=[pltpu.VMEM((m, d), BF16)],
        compiler_params=_params(("arbitrary", "arbitrary")),
        name="ffn_cast",
    )(x, gain.reshape(1, d), wg, wu, wd)


_J_K, _J_V, _J_U, _J_GATE, _J_END = 3, 6, 9, 11, 19
EPILOGUE_ROWS = 256


def _inproj_kernel(x_ref, g_ref, w_ref, qkg_ref, cos_ref, sin_ref,
                   qk_ref, v_ref, kv0_ref, kv1_ref, kv2_ref, u_ref, gate_ref, *rest,
                   head_major, cache_keep, cast_w):
    if cast_w:
        wb_ref, h_ref, z_ref = rest
        wb_ref[...] = w_ref[...].astype(BF16)
    else:
        (h_ref, z_ref), wb_ref = rest, w_ref
    j = pl.program_id(1)
    rows = h_ref.shape[0]
    nh = HEADS_PER_GROUP
    kv_refs = (kv0_ref, kv1_ref, kv2_ref)

    @pl.when(j == 0)
    def _():
        h_ref[...] = _rms_rows(x_ref[...], g_ref[...]).astype(BF16)

    def zdot():
        return jnp.dot(h_ref[...], wb_ref[...], preferred_element_type=F32)

    def put_heads(ref, heads, r0, n):
        if head_major:
            for h in range(nh):
                ref[h, pl.ds(r0, n), :] = heads[h]
        else:
            ref[pl.ds(r0, n), :] = jnp.concatenate(heads, axis=1)

    def put_cache(g, heads, r0, n):
        ref, keep = kv_refs[g], cache_keep[g]
        first = 0 if keep is None else rows - keep
        lo = max(r0, first)
        if lo >= r0 + n:
            return
        part = [hd[lo - r0:] for hd in heads]
        if head_major:
            for h in range(nh):
                ref[pl.ds((lo - first) * nh + h, r0 + n - lo, stride=nh), :] = part[h]
        else:
            ref[pl.ds(lo - first, r0 + n - lo), :] = jnp.concatenate(part, axis=1)

    def norm_rope_tile(tile):
        z_src = z_ref.at[tile % 2]
        gain = qkg_ref[...]
        n = min(EPILOGUE_ROWS, rows)
        for r0 in range(0, rows, n):
            cos, sin = cos_ref[pl.ds(r0, n), :], sin_ref[pl.ds(r0, n), :]
            heads = []
            for h in range(nh):
                y = _rms_rows(z_src[pl.ds(r0, n), _head(h)], gain[:, _head(h)])
                heads.append(y * cos + pltpu.roll(y, HEAD_DIM // 2, axis=1) * sin)
            put_heads(qk_ref, heads, r0, n)
            if tile >= _J_K:
                put_cache(tile - _J_K, heads, r0, n)

    def v_tile(g):
        z = zdot()
        heads = [z[:, _head(h)] for h in range(nh)]
        put_heads(v_ref, heads, 0, rows)
        put_cache(g, heads, 0, rows)

    for t in range(_J_V + 1):
        @pl.when(j == t)
        def _(t=t):
            if t < _J_V:
                z_ref[t % 2] = zdot()
            else:
                v_tile(0)
            if t > 0:
                norm_rope_tile(t - 1)

    for g in range(1, N_GROUPS):
        @pl.when(j == _J_V + g)
        def _(g=g):
            v_tile(g)

    @pl.when((j >= _J_U) & (j < _J_GATE))
    def _():
        u_ref[...] = zdot()

    @pl.when(j >= _J_GATE)
    def _():
        gate_ref[...] = zdot()


def _kv_slot(j, g, writes=True):
    return jnp.where(writes & (j > _J_K + g + 1), 1, 0)


def _inproj_common_specs(d, tm, nt):
    return [
        pl.BlockSpec((tm, d), lambda i, j: (i, 0)),
        pl.BlockSpec((1, d), lambda i, j: (0, 0)),
        pl.BlockSpec((d, COL_TILE), lambda i, j: (0, j)),
        pl.BlockSpec((1, COL_TILE), lambda i, j: (0, jnp.clip(j - 1, 0, _J_V - 1))),
        pl.BlockSpec((tm, HEAD_DIM), lambda i, j: (i % nt, 0)),
        pl.BlockSpec((tm, HEAD_DIM), lambda i, j: (i % nt, 0)),
    ]


def _inproj_prompt(x, gain, w, qk_gain, cos, sin, *, b, l, tm):
    m, d = x.shape
    nt = l // tm
    ssm_w = (_J_GATE - _J_U) * COL_TILE
    gate_w = (_J_END - _J_GATE) * COL_TILE
    u_tiles = _J_GATE - _J_U
    nh = HEADS_PER_GROUP
    out_specs = [
        pl.BlockSpec((None, nh, tm, HEAD_DIM), lambda i, j: (i // nt, jnp.clip(j - 1, 0, _J_V - 1), i % nt, 0)),
        pl.BlockSpec((None, nh, tm, HEAD_DIM), lambda i, j: (i // nt, jnp.clip(j - _J_V, 0, N_GROUPS - 1), i % nt, 0)),
    ]
    cache_keep = tuple(None if window >= l else window for window, _ in ATT_GROUPS)
    for g, keep in enumerate(cache_keep):
        if keep is None:
            out_specs.append(pl.BlockSpec((None, None, tm * nh, HEAD_DIM),
                                          lambda i, j, g=g: (i // nt, _kv_slot(j, g), i % nt, 0)))
        else:
            assert keep <= tm
            out_specs.append(pl.BlockSpec((None, None, keep * nh, HEAD_DIM),
                                          lambda i, j, g=g: (i // nt, _kv_slot(j, g, i % nt == nt - 1), 0, 0)))
    out_specs.append(pl.BlockSpec(
        (tm, COL_TILE), lambda i, j: (i % nt, (i // nt) * u_tiles + jnp.clip(j - _J_U, 0, u_tiles - 1))))
    out_specs.append(pl.BlockSpec(
        (tm, COL_TILE), lambda i, j: (i, jnp.clip(j - _J_GATE, 0, _J_END - _J_GATE - 1))))
    out_shape = [jax.ShapeDtypeStruct((b, _J_V * nh, l, HEAD_DIM), F32),
                 jax.ShapeDtypeStruct((b, N_GROUPS * nh, l, HEAD_DIM), F32)]
    out_shape += [jax.ShapeDtypeStruct((b, 2, (l if keep is None else keep) * nh, HEAD_DIM), F32)
                  for keep in cache_keep]
    out_shape += [jax.ShapeDtypeStruct((l, b * ssm_w), F32), jax.ShapeDtypeStruct((m, gate_w), F32)]
    return pl.pallas_call(
        functools.partial(_inproj_kernel, head_major=True, cache_keep=cache_keep, cast_w=False),
        grid=(m // tm, _J_END),
        in_specs=_inproj_common_specs(d, tm, nt),
        out_specs=out_specs,
        out_shape=out_shape,
        scratch_shapes=[pltpu.VMEM((tm, d), BF16), pltpu.VMEM((2, tm, COL_TILE), F32)],
        compiler_params=_params(("arbitrary", "arbitrary")),
        name="inproj_prompt",
    )(x, gain.reshape(1, d), w, qk_gain, cos, sin)


def _inproj_sample(x, gain, w, qk_gain, cos, sin):
    m, d = x.shape
    u_tiles = _J_GATE - _J_U
    out_specs = [pl.BlockSpec((m, COL_TILE), lambda i, j: (0, jnp.clip(j - 1, 0, _J_V - 1))),
                 pl.BlockSpec((m, COL_TILE), lambda i, j: (0, jnp.clip(j - _J_V, 0, N_GROUPS - 1)))]
    for g in range(N_GROUPS):
        out_specs.append(pl.BlockSpec((None, m, COL_TILE), lambda i, j, g=g: (_kv_slot(j, g), 0, 0)))
    out_specs.append(pl.BlockSpec((m, COL_TILE), lambda i, j: (0, jnp.clip(j - _J_U, 0, u_tiles - 1))))
    out_specs.append(pl.BlockSpec((m, COL_TILE), lambda i, j: (0, jnp.clip(j - _J_GATE, 0, _J_END - _J_GATE - 1))))
    out_shape = [jax.ShapeDtypeStruct((m, _J_V * COL_TILE), F32), jax.ShapeDtypeStruct((m, N_GROUPS * COL_TILE), F32)]
    out_shape += [jax.ShapeDtypeStruct((2, m, GROUP_WIDTH), F32)] * N_GROUPS
    out_shape += [jax.ShapeDtypeStruct((m, u_tiles * COL_TILE), F32),
                  jax.ShapeDtypeStruct((m, (_J_END - _J_GATE) * COL_TILE), F32),
                  jax.ShapeDtypeStruct(w.shape, BF16)]
    out_specs.append(pl.BlockSpec((d, COL_TILE), lambda i, j: (0, j)))
    return pl.pallas_call(
        functools.partial(_inproj_kernel, head_major=False, cache_keep=(None,) * N_GROUPS, cast_w=True),
        grid=(1, _J_END),
        in_specs=_inproj_common_specs(d, m, 1),
        out_specs=out_specs,
        out_shape=out_shape,
        scratch_shapes=[pltpu.VMEM((m, d), BF16), pltpu.VMEM((2, m, COL_TILE), F32)],
        compiler_params=_params(("arbitrary", "arbitrary")),
        name="inproj_sample",
    )(x, gain.reshape(1, d), w, qk_gain, cos, sin)


def _rope_tables(pos):
    half = HEAD_DIM // 2
    inv = jnp.power(ROPE_THETA, -jnp.arange(half, dtype=F32) * (2.0 / HEAD_DIM))
    ang = pos.astype(F32)[:, None] * inv[None, :]
    cos, sin = jnp.cos(ang), jnp.sin(ang)
    return jnp.concatenate([cos, cos], axis=1), jnp.concatenate([-sin, sin], axis=1)


ATTN_UNROLL = 16


def _attn_prompt_kernel(*refs):
    qkv = refs[:3 * N_GROUPS]
    o_ref = refs[3 * N_GROUPS]
    stats = (refs[3 * N_GROUPS + 1:3 * N_GROUPS + 4], refs[3 * N_GROUPS + 4:3 * N_GROUPS + 7])
    seq = o_ref.shape[0]
    row = lax.broadcasted_iota(jnp.int32, (QBLOCK, QBLOCK), 0)
    col = lax.broadcasted_iota(jnp.int32, (QBLOCK, QBLOCK), 1)
    cur_mask = col <= row
    prev_mask = col >= row
    nt_dims = (((1,), (1,)), ((), ()))
    full = (QBLOCK, HEAD_DIM)
    ones = jnp.ones(full, BF16)

    for g, (_, dil) in enumerate(ATT_GROUPS):
        q_ref, k_ref, v_ref = qkv[3 * g:3 * g + 3]
        nblk = seq // dil // QBLOCK
        span = QBLOCK * dil

        def rows_at(start, dil=dil):
            return pl.ds(start, QBLOCK) if dil == 1 else pl.ds(start, QBLOCK, stride=dil)

        def block(r, n, q_ref=q_ref, k_ref=k_ref, v_ref=v_ref, nblk=nblk, span=span, rows_at=rows_at):
            start = r + n * span
            rows = rows_at(start)
            q = q_ref[rows, :].astype(BF16)
            k = k_ref[rows, :].astype(BF16)
            v = jnp.concatenate([v_ref[rows, :].astype(BF16), ones], axis=1)
            if nblk > 1:
                prows = rows_at(jnp.maximum(start - span, r))
                k = jnp.concatenate([k, k_ref[prows, :].astype(BF16)], axis=0)
                vp = jnp.concatenate([v_ref[prows, :].astype(BF16), ones], axis=1)
                v = jnp.concatenate([v, vp], axis=0)
            s = lax.dot_general(q, k, nt_dims, preferred_element_type=F32) * ATT_SCALE
            if nblk > 1:
                prev_pen = jnp.where(n > 0, 0.0, -jnp.inf)
                s = jnp.concatenate([jnp.where(cur_mask, s[:, :QBLOCK], -jnp.inf),
                                     jnp.where(prev_mask, s[:, QBLOCK:], -jnp.inf) + prev_pen], axis=1)
                mx = jnp.max(jnp.maximum(s[:, :QBLOCK], s[:, QBLOCK:]), axis=1, keepdims=True)
            else:
                s = jnp.where(cur_mask, s, -jnp.inf)
                mx = jnp.max(s, axis=1, keepdims=True)
            p = jnp.exp(s - mx).astype(BF16)
            acc = jnp.dot(p, v, preferred_element_type=F32)
            return rows, acc[:, :HEAD_DIM], jnp.broadcast_to(mx, full), acc[:, HEAD_DIM:]

        src = stats[(g - 1) % 2] if g > 0 else None
        dst = stats[g % 2] if g < N_GROUPS - 1 else None

        def merge(rows, acc, mb, lb, src=src, dst=dst):
            if src is not None:
                m_old = src[1][rows, :]
                m_new = jnp.maximum(m_old, mb)
                a_old, a_new = jnp.exp(m_old - m_new), jnp.exp(mb - m_new)
                acc = a_old * src[0][rows, :] + a_new * acc
                lb = a_old * src[2][rows, :] + a_new * lb
                mb = m_new
            if dst is not None:
                dst[0][rows, :] = acc
                dst[1][rows, :] = mb
                dst[2][rows, :] = lb
            else:
                o_ref[rows, :] = acc / lb

        def body(i, carry, dil=dil, block=block, merge=merge):
            results = []
            for d in range(ATTN_UNROLL):
                idx = i * ATTN_UNROLL + d
                results.append(block(idx % dil, idx // dil))
            for res in results:
                merge(*res)
            return carry

        lax.fori_loop(0, dil * nblk // ATTN_UNROLL, body, 0)


def _attn_prompt(qk, v, *, b, l):
    nh = HEADS_PER_GROUP
    in_specs, args = [], []
    for g in range(N_GROUPS):
        for part in range(3):
            in_specs.append(pl.BlockSpec(
                (None, None, l, HEAD_DIM),
                lambda bi, hs, g=g, part=part: (bi, (part % 2) * N_GROUPS * nh + g * nh + hs, 0, 0)))
            args.append(v if part == 2 else qk)
    return pl.pallas_call(
        _attn_prompt_kernel,
        grid=(b, nh),
        in_specs=in_specs,
        out_specs=pl.BlockSpec((None, None, l, HEAD_DIM), lambda bi, hs: (bi, hs, 0, 0)),
        out_shape=jax.ShapeDtypeStruct((b, nh, l, HEAD_DIM), F32),
        scratch_shapes=[pltpu.VMEM((l, HEAD_DIM), F32)] * 6,
        compiler_params=_params(("parallel", "parallel")),
        name="attn_prompt",
    )(*args)


DECODE_ROWS = 4


def _attn_decode_kernel(q_ref, *refs):
    o_ref = refs[-1]
    for r in range(q_ref.shape[0]):
        outs, lses = [], []
        for g in range(N_GROUPS):
            kc_ref, vc_ref, kn_ref, vn_ref = refs[4 * g:4 * g + 4]
            qg = q_ref[r, g][None]
            s = jnp.sum(kc_ref[r] * qg, axis=-1, keepdims=True) * ATT_SCALE
            s_new = jnp.sum(kn_ref[r] * qg, axis=-1, keepdims=True) * ATT_SCALE
            mx = jnp.maximum(jnp.max(s, axis=0, keepdims=True), s_new)
            p = jnp.exp(s - mx)
            p_new = jnp.exp(s_new - mx)
            den = jnp.sum(p, axis=0, keepdims=True) + p_new
            acc = jnp.sum(p * vc_ref[r], axis=0, keepdims=True) + p_new * vn_ref[r]
            outs.append(acc / den)
            lses.append(mx + jnp.log(den))
        mx = functools.reduce(jnp.maximum, lses)
        ws = [jnp.exp(lse - mx) for lse in lses]
        num = functools.reduce(lambda a, c: a + c, [w * o for w, o in zip(ws, outs)])
        o_ref[r] = (num / functools.reduce(lambda a, c: a + c, ws))[0]


def _attn_decode(q, caches, kvnews):
    b = q.shape[0]
    nh = HEADS_PER_GROUP
    rows = DECODE_ROWS
    in_specs = [pl.BlockSpec((rows, N_GROUPS, nh, HEAD_DIM), lambda i: (i, 0, 0, 0))]
    args = [q]
    for g, (window, dil) in enumerate(ATT_GROUPS):
        cache = caches[g]
        assert cache.shape[2] == window and window == N_BACK * dil
        cv = cache.reshape(b, 2, N_BACK, dil, nh, HEAD_DIM)
        in_specs += [
            pl.BlockSpec((rows, None, N_BACK, None, nh, HEAD_DIM), lambda i: (i, 0, 0, 0, 0, 0)),
            pl.BlockSpec((rows, None, N_BACK, None, nh, HEAD_DIM), lambda i: (i, 1, 0, 0, 0, 0)),
            pl.BlockSpec((None, rows, 1, nh, HEAD_DIM), lambda i: (0, i, 0, 0, 0)),
            pl.BlockSpec((None, rows, 1, nh, HEAD_DIM), lambda i: (1, i, 0, 0, 0)),
        ]
        args += [cv, cv, kvnews[g], kvnews[g]]
    return pl.pallas_call(
        _attn_decode_kernel,
        grid=(b // rows,),
        in_specs=in_specs,
        out_specs=pl.BlockSpec((rows, nh, HEAD_DIM), lambda i: (i, 0, 0)),
        out_shape=jax.ShapeDtypeStruct((b, nh, HEAD_DIM), F32),
        compiler_params=_params(("parallel",)),
        name="attn_decode",
    )(*args)


def _cache_shift_kernel(c_ref, new_ref, o_ref):
    w = c_ref.shape[0]
    o_ref[pl.ds(0, w - 1)] = c_ref[pl.ds(1, w - 1)]
    o_ref[pl.ds(w - 1, 1)] = new_ref[...]


def _cache_shift(cache, kvnew):
    b, _, w, nh, e = cache.shape
    return pl.pallas_call(
        _cache_shift_kernel,
        grid=(b, 2),
        in_specs=[
            pl.BlockSpec((None, None, w, nh, e), lambda i, s: (i, s, 0, 0, 0)),
            pl.BlockSpec((None, None, 1, nh, e), lambda i, s: (s, i, 0, 0, 0)),
        ],
        out_specs=pl.BlockSpec((None, None, w, nh, e), lambda i, s: (i, s, 0, 0, 0)),
        out_shape=jax.ShapeDtypeStruct(cache.shape, F32),
        compiler_params=_params(("parallel", "parallel")),
        name=f"cache_shift_w{w}",
    )(cache, kvnew)


def _ssm_prep_kernel(lr_ref, li_ref, ldt_ref, br_ref, bi_ref, cim_ref,
                     ar_ref, ai_ref, bbr_ref, bbi_ref, ncim_ref):
    lr, li = lr_ref[...], li_ref[...]
    dt = jnp.exp(ldt_ref[...])
    mag = jnp.exp(lr * dt)
    ar = mag * jnp.cos(li * dt)
    ai = mag * jnp.sin(li * dt)
    den = lr * lr + li * li
    fr = ((ar - 1.0) * lr + ai * li) / den
    fi = (ai * lr - (ar - 1.0) * li) / den
    br, bi = br_ref[...], bi_ref[...]
    ar_ref[...] = ar
    ai_ref[...] = ai
    bbr_ref[...] = fr * br - fi * bi
    bbi_ref[...] = fr * bi + fi * br
    ncim_ref[...] = -cim_ref[...]


def _ssm_prepare(lam_re, lam_im, log_dt, b_re, b_im, c_re, c_im):
    gn, pn = lam_re.shape
    cn = SSM_GROUP
    rows = gn * cn
    rep = lambda a: jnp.broadcast_to(a[:, None, :], (gn, cn, pn)).reshape(rows, pn)
    tr = lambda a: jnp.transpose(a, (0, 2, 1)).reshape(rows, pn)
    ldt = jnp.broadcast_to(log_dt[:, None, None], (gn, cn, pn)).reshape(rows, pn)
    shp = jax.ShapeDtypeStruct((rows, pn), F32)
    ar, ai, bbr, bbi, ncim = pl.pallas_call(
        _ssm_prep_kernel, out_shape=[shp] * 5, name="ssm_prep",
    )(rep(lam_re), rep(lam_im), ldt, tr(b_re), tr(b_im), c_im.reshape(rows, pn))
    nq, gb = gn // SSM_GROUPS_PER_BLOCK, SSM_GROUPS_PER_BLOCK
    diag = (jnp.arange(gb)[:, None, None, None] == jnp.arange(gb)[None, None, :, None])
    spread = lambda a: jnp.where(diag, a.reshape(2, nq, gb, cn, 1, pn), 0.0)
    bb = spread(jnp.stack([bbr, bbi]))
    bblk = jnp.transpose(bb, (1, 2, 3, 0, 4, 5)).reshape(nq, gb * cn, 2 * gb * pn).astype(BF16)
    cc = spread(jnp.stack([c_re.reshape(rows, pn), ncim]))
    cblk = jnp.transpose(cc, (1, 0, 2, 5, 4, 3)).reshape(nq, 2 * gb * pn, gb * cn).astype(BF16)
    a_re = ar.reshape(gn, cn, pn)[:, 0, :].reshape(gn * pn // LANES, LANES)
    a_im = ai.reshape(gn, cn, pn)[:, 0, :].reshape(gn * pn // LANES, LANES)
    return a_re, a_im, bblk, cblk


def _ssm_kernel(u_ref, x0_ref, bblk_ref, cblk_ref, are_ref, aim_ref, d_ref,
                y_ref, sfin_ref, ut_ref, yt_ref, bu_ref, xs_ref, st_ref, *, nb, steps):
    c = pl.program_id(0)
    nq = bblk_ref.shape[0]
    tiles_per_q = 2 * SSM_GROUPS_PER_BLOCK * SSM_STATE // LANES
    half = tiles_per_q // 2

    @pl.when(c == 0)
    def _():
        for j in range(nq * tiles_per_q):
            st_ref[j] = x0_ref[:, j * LANES:(j + 1) * LANES]

    width = nq * LANES
    for s in range(nb if steps > 1 else 1):
        for q in range(nq):
            if steps > 1:
                ut_ref[q, pl.ds(s, steps, stride=nb), :] = u_ref[:, s * width + q * LANES:s * width + (q + 1) * LANES]
            else:
                ut_ref[q] = u_ref[:, q * LANES:(q + 1) * LANES]
    for q in range(nq):
        res = jnp.dot(ut_ref[q].astype(BF16), bblk_ref[q], preferred_element_type=F32)
        for j in range(tiles_per_q):
            bu_ref[q * tiles_per_q + j] = res[:, j * LANES:(j + 1) * LANES]

    def scan_block(q, carry):
        for lt in range(half):
            jr = q * tiles_per_q + lt
            ji = jr + half
            ar = are_ref[q * half + lt]
            ai = aim_ref[q * half + lt]
            xr, xi = st_ref[jr], st_ref[ji]
            for t in range(steps):
                rows = pl.ds(t * nb, nb)
                xr, xi = (ar * xr - ai * xi + bu_ref[jr, rows, :],
                          ar * xi + ai * xr + bu_ref[ji, rows, :])
                xs_ref[jr, rows, :] = xr.astype(BF16)
                xs_ref[ji, rows, :] = xi.astype(BF16)
            st_ref[jr] = xr
            st_ref[ji] = xi
        return carry

    for q in range(nq):
        scan_block(q, 0)

    for q in range(nq):
        xq = jnp.concatenate([xs_ref[q * tiles_per_q + j] for j in range(tiles_per_q)], axis=1)
        yq = jnp.dot(xq, cblk_ref[q], preferred_element_type=F32)
        sl = slice(q * LANES, (q + 1) * LANES)
        yq = yq + d_ref[:, sl] * ut_ref[q]
        if steps > 1:
            yt_ref[q] = yq
        else:
            y_ref[:, sl] = yq
    if steps > 1:
        for s in range(nb):
            for q in range(nq):
                y_ref[:, s * width + q * LANES:s * width + (q + 1) * LANES] = yt_ref[q, pl.ds(s, steps, stride=nb), :]

    @pl.when(c == pl.num_programs(0) - 1)
    def _():
        for j in range(nq * tiles_per_q):
            sfin_ref[:, j * LANES:(j + 1) * LANES] = st_ref[j]


def _ssm(u, x0, a_re, a_im, bblk, cblk, d_skip, *, nb, steps):
    width = d_skip.shape[0]
    nsteps = u.shape[0] // steps if steps > 1 else 1
    blk = (steps, nb * width) if steps > 1 else (nb, width)
    rows = steps * nb
    ntile = a_re.shape[0]
    nstate = 2 * ntile * LANES
    are = jnp.broadcast_to(a_re[:, None, :], (ntile, nb, LANES))
    aim = jnp.broadcast_to(a_im[:, None, :], (ntile, nb, LANES))
    const = lambda shape: pl.BlockSpec(shape, lambda c: (0,) * len(shape))
    return pl.pallas_call(
        functools.partial(_ssm_kernel, nb=nb, steps=steps),
        grid=(nsteps,),
        in_specs=[
            pl.BlockSpec(blk, lambda c: (c, 0)),
            const((nb, nstate)),
            const(bblk.shape),
            const(cblk.shape),
            const(are.shape),
            const(aim.shape),
            const((1, width)),
        ],
        out_specs=[pl.BlockSpec(blk, lambda c: (c, 0)), const((nb, nstate))],
        out_shape=[jax.ShapeDtypeStruct(u.shape, F32), jax.ShapeDtypeStruct((nb, nstate), F32)],
        scratch_shapes=[
            pltpu.VMEM((width // LANES, rows, LANES), F32),
            pltpu.VMEM((width // LANES, rows, LANES), F32),
            pltpu.VMEM((2 * ntile, rows, LANES), F32),
            pltpu.VMEM((2 * ntile, rows, LANES), BF16),
            pltpu.VMEM((2 * ntile, nb, LANES), F32),
        ],
        compiler_params=_params(("arbitrary",)),
        name=f"ssm_nb{nb}",
    )(u, x0, bblk, cblk, are, aim, d_skip.reshape(1, width))


def _state_to_lanes(s_re, s_im):
    b, gn, pn = s_re.shape
    nq = gn // SSM_GROUPS_PER_BLOCK
    st = jnp.stack([s_re.reshape(b, nq, -1), s_im.reshape(b, nq, -1)], axis=2)
    return st.reshape(b, 2 * gn * pn)


def _lanes_to_state(s, gn, pn):
    b = s.shape[0]
    nq = gn // SSM_GROUPS_PER_BLOCK
    st = s.reshape(b, nq, 2, SSM_GROUPS_PER_BLOCK * pn)
    return st[:, :, 0].reshape(b, gn, pn), st[:, :, 1].reshape(b, gn, pn)


def _mix_kernel(x_ref, att_ref, y_ref, ga_ref, gs_ref, gw_ref, gb_ref, wa_ref, ws_ref, wo_ref, out_ref):
    att = jnp.concatenate([att_ref[h] for h in range(HEADS_PER_GROUP)], axis=1)
    a_proj = jnp.dot(att.astype(BF16), wa_ref[...], preferred_element_type=F32)
    zs = jax.nn.gelu(y_ref[...])
    glu = jnp.dot(zs.astype(BF16), gw_ref[...], preferred_element_type=F32) + gb_ref[...]
    s_out = zs * jax.nn.sigmoid(glu)
    s_proj = jnp.dot(s_out.astype(BF16), ws_ref[...], preferred_element_type=F32)
    merged = jax.nn.sigmoid(ga_ref[...]) * a_proj + jax.nn.sigmoid(gs_ref[...]) * s_proj
    out_ref[...] = x_ref[...] + jnp.dot(merged.astype(BF16), wo_ref[...], preferred_element_type=F32)


def _mix(x, att, y, y_map, gates, glu_w, glu_b, wa, ws, wo, *, tm, nt):
    m, d = x.shape
    sw = glu_w.shape[0]
    row = lambda width: pl.BlockSpec((tm, width), lambda i: (i, 0))
    const = lambda a: pl.BlockSpec(a.shape, lambda i: (0, 0), pipeline_mode=pl.Buffered(1))
    glu_b = glu_b.reshape(1, sw)
    in_specs = [row(d),
                pl.BlockSpec((None, HEADS_PER_GROUP, tm, HEAD_DIM), lambda i: (i // nt, 0, i % nt, 0)),
                pl.BlockSpec((tm, sw), y_map),
                pl.BlockSpec((tm, d), lambda i: (i, 0)), pl.BlockSpec((tm, d), lambda i: (i, 1)),
                const(glu_w), const(glu_b), const(wa), const(ws), const(wo)]
    return pl.pallas_call(
        _mix_kernel,
        grid=(m // tm,),
        in_specs=in_specs,
        out_specs=row(d),
        out_shape=jax.ShapeDtypeStruct((m, d), F32),
        compiler_params=_params(("parallel",)),
        name="mix",
    )(x, att, y, gates, gates, glu_w, glu_b, wa, ws, wo)


FFN_TF = 512
PROMPT_TM_FFN = 1024
PROMPT_TM_INPROJ = 1024
PROMPT_TM_MIX = 256
SSM_STEPS = 16


def _layer_weights(p):
    w = dict(p)
    w["qk_gain"] = jnp.concatenate([p["q_norm"].reshape(1, -1), p["k_norm"].reshape(1, -1)], axis=1)
    w["ssm"] = _ssm_prepare(p["ssm_lambda_re"], p["ssm_lambda_im"], p["ssm_log_dt"],
                            p["ssm_b_re"], p["ssm_b_im"], p["ssm_c_re"], p["ssm_c_im"])
    for name in ("glu_w", "w_attn_branch", "w_ssm_branch", "w_out"):
        w[name + "_b"] = p[name].astype(BF16)
    return w


def _prompt_layer(x, w, ffn_w, w_in_b):
    b, l, d = x.shape
    m = b * l
    nh = HEADS_PER_GROUP
    gn, pn = w["ssm_lambda_re"].shape
    d_ff = w["ffn1_w_gate"].shape[1]
    x1 = _ffn(x.reshape(m, d), w["ffn1_norm"], *ffn_w[0], tm=PROMPT_TM_FFN, tf=FFN_TF, d_ff=d_ff)
    cos, sin = _rope_tables(jnp.arange(l, dtype=jnp.int32))
    tm_in = min(PROMPT_TM_INPROJ, l)
    qk, v, kv0, kv1, kv2, u_t, gates = _inproj_prompt(
        x1, w["mix_norm"], w_in_b, w["qk_gain"], cos, sin, b=b, l=l, tm=tm_in)
    att = _attn_prompt(qk, v, b=b, l=l)
    a_re, a_im, bblk, cblk = w["ssm"]
    y_t, sfin = _ssm(u_t, jnp.zeros((b, 2 * gn * pn), F32),
                     a_re, a_im, bblk, cblk, w["ssm_d"], nb=b, steps=SSM_STEPS)
    tm = PROMPT_TM_MIX
    nt = l // tm
    x2 = _mix(x1, att, y_t, lambda i: (i % nt, i // nt), gates,
              w["glu_w_b"], w["glu_b"], w["w_attn_branch_b"], w["w_ssm_branch_b"], w["w_out_b"], tm=tm, nt=nt)
    y = _ffn(x2, w["ffn2_norm"], *ffn_w[1], tm=PROMPT_TM_FFN, tf=FFN_TF, d_ff=d_ff)
    new_kv = []
    for g, (window, _) in enumerate(ATT_GROUPS):
        new_kv.append((kv0, kv1, kv2)[g].reshape(b, 2, min(window, l), nh, HEAD_DIM))
    s_re, s_im = _lanes_to_state(sfin, gn, pn)
    return y.reshape(b, l, d), new_kv, (s_re, s_im)


def _sample_layer(x, pos0, caches, state, w):
    b, l, d = x.shape
    assert l == 1
    nh = HEADS_PER_GROUP
    gn, pn = w["ssm_lambda_re"].shape
    x1, *ffn1_w = _ffn_cast(x.reshape(b, d), w["ffn1_norm"], w["ffn1_w_gate"], w["ffn1_w_up"], w["ffn1_w_down"],
                            tf=FFN_TF)
    cos, sin = _rope_tables(jnp.full((b,), pos0, dtype=jnp.int32))
    qk, _, kn0, kn1, kn2, u, gates, w_in_b = _inproj_sample(
        x1, w["mix_norm"], w["w_in"], w["qk_gain"], cos, sin)
    q = qk[:, :ATT_WIDTH].reshape(b, N_GROUPS, nh, HEAD_DIM)
    kvnews = [kn.reshape(2, b, 1, nh, HEAD_DIM) for kn in (kn0, kn1, kn2)]
    att = _attn_decode(q, caches, kvnews)
    new_kv = [_cache_shift(caches[g], kvnews[g]) for g in range(N_GROUPS)]
    a_re, a_im, bblk, cblk = w["ssm"]
    y, sfin = _ssm(u, _state_to_lanes(*state), a_re, a_im, bblk, cblk, w["ssm_d"], nb=b, steps=1)
    att_h = jnp.transpose(att, (1, 0, 2))[None]
    x2 = _mix(x1, att_h, y, lambda i: (i, 0), gates,
              w["glu_w_b"], w["glu_b"], w["w_attn_branch_b"], w["w_ssm_branch_b"], w["w_out_b"], tm=b, nt=1)
    out, *ffn2_w = _ffn_cast(x2, w["ffn2_norm"], w["ffn2_w_gate"], w["ffn2_w_up"], w["ffn2_w_down"], tf=FFN_TF)
    s_re, s_im = _lanes_to_state(sfin, gn, pn)
    return out.reshape(b, l, d), new_kv, (s_re, s_im), ((ffn1_w, ffn2_w), w_in_b)


def kernel(x_prompt, x_sample, cache_kv_w128, cache_kv_w512, cache_kv_w2048, state_ssm_re, state_ssm_im,
           ffn1_norm, ffn1_w_gate, ffn1_w_up, ffn1_w_down, mix_norm, w_in, q_norm, k_norm,
           ssm_lambda_re, ssm_lambda_im, ssm_log_dt, ssm_b_re, ssm_b_im, ssm_c_re, ssm_c_im, ssm_d,
           glu_w, glu_b, w_attn_branch, w_ssm_branch, w_out,
           ffn2_norm, ffn2_w_gate, ffn2_w_up, ffn2_w_down):
    depth = w_in.shape[0]
    params = dict(ffn1_norm=ffn1_norm, ffn1_w_gate=ffn1_w_gate, ffn1_w_up=ffn1_w_up, ffn1_w_down=ffn1_w_down,
                  mix_norm=mix_norm, w_in=w_in, q_norm=q_norm, k_norm=k_norm,
                  ssm_lambda_re=ssm_lambda_re, ssm_lambda_im=ssm_lambda_im, ssm_log_dt=ssm_log_dt,
                  ssm_b_re=ssm_b_re, ssm_b_im=ssm_b_im, ssm_c_re=ssm_c_re, ssm_c_im=ssm_c_im, ssm_d=ssm_d,
                  glu_w=glu_w, glu_b=glu_b, w_attn_branch=w_attn_branch, w_ssm_branch=w_ssm_branch,
                  w_out=w_out, ffn2_norm=ffn2_norm, ffn2_w_gate=ffn2_w_gate, ffn2_w_up=ffn2_w_up,
                  ffn2_w_down=ffn2_w_down)
    yp, ys = x_prompt, x_sample
    new_p = [[] for _ in range(5)]
    new_s = [[] for _ in range(5)]
    for layer in range(depth):
        w = _layer_weights({k: v[layer] for k, v in params.items()})
        ys, kv_s, st_s, (ffn_w, w_in_b) = _sample_layer(
            ys, PAST_LEN, (cache_kv_w128[layer], cache_kv_w512[layer], cache_kv_w2048[layer]),
            (state_ssm_re[layer], state_ssm_im[layer]), w)
        yp, kv_p, st_p = _prompt_layer(yp, w, ffn_w, w_in_b)
        for i, a in enumerate(list(kv_p) + list(st_p)):
            new_p[i].append(a)
        for i, a in enumerate(list(kv_s) + list(st_s)):
            new_s[i].append(a)
    outs_p = [jnp.stack(a) for a in new_p]
    outs_s = [jnp.stack(a) for a in new_s]
    return (yp, ys, *outs_p, *outs_s)
```

```python
import functools

import jax
import jax.numpy as jnp
from jax import lax
from jax.experimental import pallas as pl
from jax.experimental.pallas import tpu as pltpu

F32 = jnp.float32
BF16 = jnp.bfloat16

HEAD_DIM = 128
HEADS_PER_GROUP = 4
GROUP_WIDTH = HEADS_PER_GROUP * HEAD_DIM
ATT_GROUPS = ((128, 1), (512, 4), (2048, 16))
N_GROUPS = len(ATT_GROUPS)
ATT_WIDTH = N_GROUPS * GROUP_WIDTH
N_BACK = 128
QBLOCK = 128
ATT_SCALE = HEAD_DIM ** -0.5
ROPE_THETA = 10000.0
RMS_EPS = 1e-6
PAST_LEN = 16384
SSM_GROUP = 16
SSM_STATE = 64
SSM_GROUPS_PER_BLOCK = 8
LANES = 128
COL_TILE = 512
VMEM_LIMIT = 56 * 1024 * 1024


def _params(semantics):
    return pltpu.CompilerParams(dimension_semantics=semantics, vmem_limit_bytes=VMEM_LIMIT)


def _rms_rows(x, gain):
    ms = jnp.mean(x * x, axis=-1, keepdims=True)
    return x * lax.rsqrt(ms + RMS_EPS) * gain


def _head(h):
    return slice(h * HEAD_DIM, (h + 1) * HEAD_DIM)


def _ffn_kernel(x_ref, g_ref, wg_ref, wu_ref, wd_ref, o_ref, h_ref, *, d_ff):
    def accumulate():
        h = h_ref[...]
        g = jnp.dot(h, wg_ref[...], preferred_element_type=F32)
        u = jnp.dot(h, wu_ref[...], preferred_element_type=F32)
        col = pl.program_id(1) * g.shape[1] + lax.broadcasted_iota(jnp.int32, g.shape, 1)
        a = jnp.where(col < d_ff, 0.5 * (g * jax.nn.sigmoid(g) * u), 0.0).astype(BF16)
        o_ref[...] += jnp.dot(a, wd_ref[...], preferred_element_type=F32)

    @pl.when(pl.program_id(1) == 0)
    def _():
        x = x_ref[...]
        h_ref[...] = _rms_rows(x, g_ref[...]).astype(BF16)
        o_ref[...] = x
        accumulate()

    @pl.when(pl.program_id(1) > 0)
    def _():
        accumulate()


def _ffn(x, gain, wg, wu, wd, *, tm, tf, d_ff):
    m, d = x.shape
    return pl.pallas_call(
        functools.partial(_ffn_kernel, d_ff=d_ff),
        grid=(m // tm, wd.shape[0] // tf),
        in_specs=[
            pl.BlockSpec((tm, d), lambda i, f: (i, 0)),
            pl.BlockSpec((1, d), lambda i, f: (0, 0)),
            pl.BlockSpec((d, tf), lambda i, f: (0, f)),
            pl.BlockSpec((d, tf), lambda i, f: (0, f)),
            pl.BlockSpec((tf, d), lambda i, f: (f, 0)),
        ],
        out_specs=pl.BlockSpec((tm, d), lambda i, f: (i, 0)),
        out_shape=jax.ShapeDtypeStruct((m, d), F32),
        scratch_shapes=[pltpu.VMEM((tm, d), BF16)],
        compiler_params=_params(("parallel", "arbitrary")),
        name="ffn",
    )(x, gain.reshape(1, d), wg, wu, wd)


def _ffn_cast_kernel(x_ref, g_ref, wg_ref, wu_ref, wd_ref, o_ref, wgb_ref, wub_ref, wdb_ref, h_ref, *, d_ff):
    f = pl.program_id(1)
    tf = wg_ref.shape[1]
    wcol = f * tf + lax.broadcasted_iota(jnp.int32, wg_ref.shape, 1)
    wg = jnp.where(wcol < d_ff, wg_ref[...], 0.0).astype(BF16)
    wu = jnp.where(wcol < d_ff, wu_ref[...], 0.0).astype(BF16)
    row = f * tf + lax.broadcasted_iota(jnp.int32, wd_ref.shape, 0)
    wd = jnp.where(row < d_ff, wd_ref[...], 0.0).astype(BF16)
    wgb_ref[...] = wg
    wub_ref[...] = wu
    wdb_ref[...] = wd

    @pl.when(f == 0)
    def _():
        x = x_ref[...]
        h_ref[...] = _rms_rows(x, g_ref[...]).astype(BF16)
        o_ref[...] = x

    h = h_ref[...]
    g = jnp.dot(h, wg, preferred_element_type=F32)
    u = jnp.dot(h, wu, preferred_element_type=F32)
    col = f * tf + lax.broadcasted_iota(jnp.int32, g.shape, 1)
    a = jnp.where(col < d_ff, 0.5 * (g * jax.nn.sigmoid(g) * u), 0.0).astype(BF16)
    o_ref[...] += jnp.dot(a, wd, preferred_element_type=F32)


def _ffn_cast(x, gain, wg, wu, wd, *, tf):
    m, d = x.shape
    d_ff = wg.shape[1]
    nf = pl.cdiv(d_ff, tf)
    wide = jax.ShapeDtypeStruct((d, nf * tf), BF16)
    return pl.pallas_call(
        functools.partial(_ffn_cast_kernel, d_ff=d_ff),
        grid=(1, nf),
        in_specs=[
            pl.BlockSpec((m, d), lambda i, f: (0, 0)),
            pl.BlockSpec((1, d), lambda i, f: (0, 0)),
            pl.BlockSpec((d, tf), lambda i, f: (0, f)),
            pl.BlockSpec((d, tf), lambda i, f: (0, f)),
            pl.BlockSpec((tf, d), lambda i, f: (f, 0)),
        ],
        out_specs=[
            pl.BlockSpec((m, d), lambda i, f: (0, 0)),
            pl.BlockSpec((d, tf), lambda i, f: (0, f)),
            pl.BlockSpec((d, tf), lambda i, f: (0, f)),
            pl.BlockSpec((tf, d), lambda i, f: (f, 0)),
        ],
        out_shape=[jax.ShapeDtypeStruct((m, d), F32), wide, wide, jax.ShapeDtypeStruct((nf * tf, d), BF16)],
        scratch_shapes=[pltpu.VMEM((m, d), BF16)],
        compiler_params=_params(("arbitrary", "arbitrary")),
        name="ffn_cast",
    )(x, gain.reshape(1, d), wg, wu, wd)


_J_K, _J_V, _J_U, _J_GATE, _J_END = 3, 6, 9, 11, 19
EPILOGUE_ROWS = 256


def _inproj_kernel(x_ref, g_ref, w_ref, qkg_ref, cos_ref, sin_ref,
                   qk_ref, v_ref, kv0_ref, kv1_ref, kv2_ref, u_ref, gate_ref, *rest,
                   head_major, cache_keep, cast_w):
    if cast_w:
        wb_ref, h_ref, z_ref = rest
        wb_ref[...] = w_ref[...].astype(BF16)
    else:
        (h_ref, z_ref), wb_ref = rest, w_ref
    j = pl.program_id(1)
    rows = h_ref.shape[0]
    nh = HEADS_PER_GROUP
    kv_refs = (kv0_ref, kv1_ref, kv2_ref)

    @pl.when(j == 0)
    def _():
        h_ref[...] = _rms_rows(x_ref[...], g_ref[...]).astype(BF16)

    def zdot():
        return jnp.dot(h_ref[...], wb_ref[...], preferred_element_type=F32)

    def put_heads(ref, heads, r0, n):
        if head_major:
            for h in range(nh):
                ref[h, pl.ds(r0, n), :] = heads[h]
        else:
            ref[pl.ds(r0, n), :] = jnp.concatenate(heads, axis=1)

    def put_cache(g, heads, r0, n):
        ref, keep = kv_refs[g], cache_keep[g]
        first = 0 if keep is None else rows - keep
        lo = max(r0, first)
        if lo >= r0 + n:
            return
        part = [hd[lo - r0:] for hd in heads]
        if head_major:
            for h in range(nh):
                ref[pl.ds((lo - first) * nh + h, r0 + n - lo, stride=nh), :] = part[h]
        else:
            ref[pl.ds(lo - first, r0 + n - lo), :] = jnp.concatenate(part, axis=1)

    def norm_rope_tile(tile):
        z_src = z_ref.at[tile % 2]
        gain = qkg_ref[...]
        n = min(EPILOGUE_ROWS, rows)
        for r0 in range(0, rows, n):
            cos, sin = cos_ref[pl.ds(r0, n), :], sin_ref[pl.ds(r0, n), :]
            heads = []
            for h in range(nh):
                y = _rms_rows(z_src[pl.ds(r0, n), _head(h)], gain[:, _head(h)])
                heads.append(y * cos + pltpu.roll(y, HEAD_DIM // 2, axis=1) * sin)
            put_heads(qk_ref, heads, r0, n)
            if tile >= _J_K:
                put_cache(tile - _J_K, heads, r0, n)

    def v_tile(g):
        z = zdot()
        heads = [z[:, _head(h)] for h in range(nh)]
        put_heads(v_ref, heads, 0, rows)
        put_cache(g, heads, 0, rows)

    for t in range(_J_V + 1):
        @pl.when(j == t)
        def _(t=t):
            if t < _J_V:
                z_ref[t % 2] = zdot()
            else:
                v_tile(0)
            if t > 0:
                norm_rope_tile(t - 1)

    for g in range(1, N_GROUPS):
        @pl.when(j == _J_V + g)
        def _(g=g):
            v_tile(g)

    @pl.when((j >= _J_U) & (j < _J_GATE))
    def _():
        u_ref[...] = zdot()

    @pl.when(j >= _J_GATE)
    def _():
        gate_ref[...] = zdot()


def _kv_slot(j, g, writes=True):
    return jnp.where(writes & (j > _J_K + g + 1), 1, 0)


def _inproj_common_specs(d, tm, nt):
    return [
        pl.BlockSpec((tm, d), lambda i, j: (i, 0)),
        pl.BlockSpec((1, d), lambda i, j: (0, 0)),
        pl.BlockSpec((d, COL_TILE), lambda i, j: (0, j)),
        pl.BlockSpec((1, COL_TILE), lambda i, j: (0, jnp.clip(j - 1, 0, _J_V - 1))),
        pl.BlockSpec((tm, HEAD_DIM), lambda i, j: (i % nt, 0)),
        pl.BlockSpec((tm, HEAD_DIM), lambda i, j: (i % nt, 0)),
    ]


def _inproj_prompt(x, gain, w, qk_gain, cos, sin, *, b, l, tm):
    m, d = x.shape
    nt = l // tm
    ssm_w = (_J_GATE - _J_U) * COL_TILE
    gate_w = (_J_END - _J_GATE) * COL_TILE
    u_tiles = _J_GATE - _J_U
    nh = HEADS_PER_GROUP
    out_specs = [
        pl.BlockSpec((None, nh, tm, HEAD_DIM), lambda i, j: (i // nt, jnp.clip(j - 1, 0, _J_V - 1), i % nt, 0)),
        pl.BlockSpec((None, nh, tm, HEAD_DIM), lambda i, j: (i // nt, jnp.clip(j - _J_V, 0, N_GROUPS - 1), i % nt, 0)),
    ]
    cache_keep = tuple(None if window >= l else window for window, _ in ATT_GROUPS)
    for g, keep in enumerate(cache_keep):
        if keep is None:
            out_specs.append(pl.BlockSpec((None, None, tm * nh, HEAD_DIM),
                                          lambda i, j, g=g: (i // nt, _kv_slot(j, g), i % nt, 0)))
        else:
            assert keep <= tm
            out_specs.append(pl.BlockSpec((None, None, keep * nh, HEAD_DIM),
                                          lambda i, j, g=g: (i // nt, _kv_slot(j, g, i % nt == nt - 1), 0, 0)))
    out_specs.append(pl.BlockSpec(
        (tm, COL_TILE), lambda i, j: (i % nt, (i // nt) * u_tiles + jnp.clip(j - _J_U, 0, u_tiles - 1))))
    out_specs.append(pl.BlockSpec(
        (tm, COL_TILE), lambda i, j: (i, jnp.clip(j - _J_GATE, 0, _J_END - _J_GATE - 1))))
    out_shape = [jax.ShapeDtypeStruct((b, _J_V * nh, l, HEAD_DIM), F32),
                 jax.ShapeDtypeStruct((b, N_GROUPS * nh, l, HEAD_DIM), F32)]
    out_shape += [jax.ShapeDtypeStruct((b, 2, (l if keep is None else keep) * nh, HEAD_DIM), F32)
                  for keep in cache_keep]
    out_shape += [jax.ShapeDtypeStruct((l, b * ssm_w), F32), jax.ShapeDtypeStruct((m, gate_w), F32)]
    return pl.pallas_call(
        functools.partial(_inproj_kernel, head_major=True, cache_keep=cache_keep, cast_w=False),
        grid=(m // tm, _J_END),
        in_specs=_inproj_common_specs(d, tm, nt),
        out_specs=out_specs,
        out_shape=out_shape,
        scratch_shapes=[pltpu.VMEM((tm, d), BF16), pltpu.VMEM((2, tm, COL_TILE), F32)],
        compiler_params=_params(("arbitrary", "arbitrary")),
        name="inproj_prompt",
    )(x, gain.reshape(1, d), w, qk_gain, cos, sin)


def _inproj_sample(x, gain, w, qk_gain, cos, sin):
    m, d = x.shape
    u_tiles = _J_GATE - _J_U
    out_specs = [pl.BlockSpec((m, COL_TILE), lambda i, j: (0, jnp.clip(j - 1, 0, _J_V - 1))),
                 pl.BlockSpec((m, COL_TILE), lambda i, j: (0, jnp.clip(j - _J_V, 0, N_GROUPS - 1)))]
    for g in range(N_GROUPS):
        out_specs.append(pl.BlockSpec((None, m, COL_TILE), lambda i, j, g=g: (_kv_slot(j, g), 0, 0)))
    out_specs.append(pl.BlockSpec((m, COL_TILE), lambda i, j: (0, jnp.clip(j - _J_U, 0, u_tiles - 1))))
    out_specs.append(pl.BlockSpec((m, COL_TILE), lambda i, j: (0, jnp.clip(j - _J_GATE, 0, _J_END - _J_GATE - 1))))
    out_shape = [jax.ShapeDtypeStruct((m, _J_V * COL_TILE), F32), jax.ShapeDtypeStruct((m, N_GROUPS * COL_TILE), F32)]
    out_shape += [jax.ShapeDtypeStruct((2, m, GROUP_WIDTH), F32)] * N_GROUPS
    out_shape += [jax.ShapeDtypeStruct((m, u_tiles * COL_TILE), F32),
                  jax.ShapeDtypeStruct((m, (_J_END - _J_GATE) * COL_TILE), F32),
                  jax.ShapeDtypeStruct(w.shape, BF16)]
    out_specs.append(pl.BlockSpec((d, COL_TILE), lambda i, j: (0, j)))
    return pl.pallas_call(
        functools.partial(_inproj_kernel, head_major=False, cache_keep=(None,) * N_GROUPS, cast_w=True),
        grid=(1, _J_END),
        in_specs=_inproj_common_specs(d, m, 1),
        out_specs=out_specs,
        out_shape=out_shape,
        scratch_shapes=[pltpu.VMEM((m, d), BF16), pltpu.VMEM((2, m, COL_TILE), F32)],
        compiler_params=_params(("arbitrary", "arbitrary")),
        name="inproj_sample",
    )(x, gain.reshape(1, d), w, qk_gain, cos, sin)


def _rope_tables(pos):
    half = HEAD_DIM // 2
    inv = jnp.power(ROPE_THETA, -jnp.arange(half, dtype=F32) * (2.0 / HEAD_DIM))
    ang = pos.astype(F32)[:, None] * inv[None, :]
    cos, sin = jnp.cos(ang), jnp.sin(ang)
    return jnp.concatenate([cos, cos], axis=1), jnp.concatenate([-sin, sin], axis=1)


ATTN_UNROLL = 16


def _attn_prompt_kernel(*refs):
    qkv = refs[:3 * N_GROUPS]
    o_ref = refs[3 * N_GROUPS]
    stats = (refs[3 * N_GROUPS + 1:3 * N_GROUPS + 4], refs[3 * N_GROUPS + 4:3 * N_GROUPS + 7])
    seq = o_ref.shape[0]
    row = lax.broadcasted_iota(jnp.int32, (QBLOCK, QBLOCK), 0)
    col = lax.broadcasted_iota(jnp.int32, (QBLOCK, QBLOCK), 1)
    cur_mask = col <= row
    prev_mask = col >= row
    nt_dims = (((1,), (1,)), ((), ()))
    full = (QBLOCK, HEAD_DIM)
    ones = jnp.ones(full, BF16)

    for g, (_, dil) in enumerate(ATT_GROUPS):
        q_ref, k_ref, v_ref = qkv[3 * g:3 * g + 3]
        nblk = seq // dil // QBLOCK
        span = QBLOCK * dil

        def rows_at(start, dil=dil):
            return pl.ds(start, QBLOCK) if dil == 1 else pl.ds(start, QBLOCK, stride=dil)

        def block(r, n, q_ref=q_ref, k_ref=k_ref, v_ref=v_ref, nblk=nblk, span=span, rows_at=rows_at):
            start = r + n * span
            rows = rows_at(start)
            q = q_ref[rows, :].astype(BF16)
            k = k_ref[rows, :].astype(BF16)
            v = jnp.concatenate([v_ref[rows, :].astype(BF16), ones], axis=1)
            if nblk > 1:
                prows = rows_at(jnp.maximum(start - span, r))
                k = jnp.concatenate([k, k_ref[prows, :].astype(BF16)], axis=0)
                vp = jnp.concatenate([v_ref[prows, :].astype(BF16), ones], axis=1)
                v = jnp.concatenate([v, vp], axis=0)
            s = lax.dot_general(q, k, nt_dims, preferred_element_type=F32) * ATT_SCALE
            if nblk > 1:
                prev_pen = jnp.where(n > 0, 0.0, -jnp.inf)
                s = jnp.concatenate([jnp.where(cur_mask, s[:, :QBLOCK], -jnp.inf),
                                     jnp.where(prev_mask, s[:, QBLOCK:], -jnp.inf) + prev_pen], axis=1)
                mx = jnp.max(jnp.maximum(s[:, :QBLOCK], s[:, QBLOCK:]), axis=1, keepdims=True)
            else:
                s = jnp.where(cur_mask, s, -jnp.inf)
                mx = jnp.max(s, axis=1, keepdims=True)
            p = jnp.exp(s - mx).astype(BF16)
            acc = jnp.dot(p, v, preferred_element_type=F32)
            return rows, acc[:, :HEAD_DIM], jnp.broadcast_to(mx, full), acc[:, HEAD_DIM:]

        src = stats[(g - 1) % 2] if g > 0 else None
        dst = stats[g % 2] if g < N_GROUPS - 1 else None

        def merge(rows, acc, mb, lb, src=src, dst=dst):
            if src is not None:
                m_old = src[1][rows, :]
                m_new = jnp.maximum(m_old, mb)
                a_old, a_new = jnp.exp(m_old - m_new), jnp.exp(mb - m_new)
                acc = a_old * src[0][rows, :] + a_new * acc
                lb = a_old * src[2][rows, :] + a_new * lb
                mb = m_new
            if dst is not None:
                dst[0][rows, :] = acc
                dst[1][rows, :] = mb
                dst[2][rows, :] = lb
            else:
                o_ref[rows, :] = acc / lb

        def body(i, carry, dil=dil, block=block, merge=merge):
            results = []
            for d in range(ATTN_UNROLL):
                idx = i * ATTN_UNROLL + d
                results.append(block(idx % dil, idx // dil))
            for res in results:
                merge(*res)
            return carry

        lax.fori_loop(0, dil * nblk // ATTN_UNROLL, body, 0)


def _attn_prompt(qk, v, *, b, l):
    nh = HEADS_PER_GROUP
    in_specs, args = [], []
    for g in range(N_GROUPS):
        for part in range(3):
            in_specs.append(pl.BlockSpec(
                (None, None, l, HEAD_DIM),
                lambda bi, hs, g=g, part=part: (bi, (part % 2) * N_GROUPS * nh + g * nh + hs, 0, 0)))
            args.append(v if part == 2 else qk)
    return pl.pallas_call(
        _attn_prompt_kernel,
        grid=(b, nh),
        in_specs=in_specs,
        out_specs=pl.BlockSpec((None, None, l, HEAD_DIM), lambda bi, hs: (bi, hs, 0, 0)),
        out_shape=jax.ShapeDtypeStruct((b, nh, l, HEAD_DIM), F32),
        scratch_shapes=[pltpu.VMEM((l, HEAD_DIM), F32)] * 6,
        compiler_params=_params(("parallel", "parallel")),
        name="attn_prompt",
    )(*args)


DECODE_ROWS = 4


def _attn_decode_kernel(q_ref, *refs):
    o_ref = refs[-1]
    for r in range(q_ref.shape[0]):
        outs, lses = [], []
        for g in range(N_GROUPS):
            kc_ref, vc_ref, kn_ref, vn_ref = refs[4 * g:4 * g + 4]
            qg = q_ref[r, g][None]
            s = jnp.sum(kc_ref[r] * qg, axis=-1, keepdims=True) * ATT_SCALE
            s_new = jnp.sum(kn_ref[r] * qg, axis=-1, keepdims=True) * ATT_SCALE
            mx = jnp.maximum(jnp.max(s, axis=0, keepdims=True), s_new)
            p = jnp.exp(s - mx)
            p_new = jnp.exp(s_new - mx)
            den = jnp.sum(p, axis=0, keepdims=True) + p_new
            acc = jnp.sum(p * vc_ref[r], axis=0, keepdims=True) + p_new * vn_ref[r]
            outs.append(acc / den)
            lses.append(mx + jnp.log(den))
        mx = functools.reduce(jnp.maximum, lses)
        ws = [jnp.exp(lse - mx) for lse in lses]
        num = functools.reduce(lambda a, c: a + c, [w * o for w, o in zip(ws, outs)])
        o_ref[r] = (num / functools.reduce(lambda a, c: a + c, ws))[0]


def _attn_decode(q, caches, kvnews):
    b = q.shape[0]
    nh = HEADS_PER_GROUP
    rows = DECODE_ROWS
    in_specs = [pl.BlockSpec((rows, N_GROUPS, nh, HEAD_DIM), lambda i: (i, 0, 0, 0))]
    args = [q]
    for g, (window, dil) in enumerate(ATT_GROUPS):
        cache = caches[g]
        assert cache.shape[2] == window and window == N_BACK * dil
        cv = cache.reshape(b, 2, N_BACK, dil, nh, HEAD_DIM)
        in_specs += [
            pl.BlockSpec((rows, None, N_BACK, None, nh, HEAD_DIM), lambda i: (i, 0, 0, 0, 0, 0)),
            pl.BlockSpec((rows, None, N_BACK, None, nh, HEAD_DIM), lambda i: (i, 1, 0, 0, 0, 0)),
            pl.BlockSpec((None, rows, 1, nh, HEAD_DIM), lambda i: (0, i, 0, 0, 0)),
            pl.BlockSpec((None, rows, 1, nh, HEAD_DIM), lambda i: (1, i, 0, 0, 0)),
        ]
        args += [cv, cv, kvnews[g], kvnews[g]]
    return pl.pallas_call(
        _attn_decode_kernel,
        grid=(b // rows,),
        in_specs=in_specs,
        out_specs=pl.BlockSpec((rows, nh, HEAD_DIM), lambda i: (i, 0, 0)),
        out_shape=jax.ShapeDtypeStruct((b, nh, HEAD_DIM), F32),
        compiler_params=_params(("parallel",)),
        name="attn_decode",
    )(*args)


def _cache_shift_kernel(c_ref, new_ref, o_ref):
    w = c_ref.shape[0]
    o_ref[pl.ds(0, w - 1)] = c_ref[pl.ds(1, w - 1)]
    o_ref[pl.ds(w - 1, 1)] = new_ref[...]


def _cache_shift(cache, kvnew):
    b, _, w, nh, e = cache.shape
    return pl.pallas_call(
        _cache_shift_kernel,
        grid=(b, 2),
        in_specs=[
            pl.BlockSpec((None, None, w, nh, e), lambda i, s: (i, s, 0, 0, 0)),
            pl.BlockSpec((None, None, 1, nh, e), lambda i, s: (s, i, 0, 0, 0)),
        ],
        out_specs=pl.BlockSpec((None, None, w, nh, e), lambda i, s: (i, s, 0, 0, 0)),
        out_shape=jax.ShapeDtypeStruct(cache.shape, F32),
        compiler_params=_params(("parallel", "parallel")),
        name=f"cache_shift_w{w}",
    )(cache, kvnew)


def _ssm_prep_kernel(lr_ref, li_ref, ldt_ref, br_ref, bi_ref, cim_ref,
                     ar_ref, ai_ref, bbr_ref, bbi_ref, ncim_ref):
    lr, li = lr_ref[...], li_ref[...]
    dt = jnp.exp(ldt_ref[...])
    mag = jnp.exp(lr * dt)
    ar = mag * jnp.cos(li * dt)
    ai = mag * jnp.sin(li * dt)
    den = lr * lr + li * li
    fr = ((ar - 1.0) * lr + ai * li) / den
    fi = (ai * lr - (ar - 1.0) * li) / den
    br, bi = br_ref[...], bi_ref[...]
    ar_ref[...] = ar
    ai_ref[...] = ai
    bbr_ref[...] = fr * br - fi * bi
    bbi_ref[...] = fr * bi + fi * br
    ncim_ref[...] = -cim_ref[...]


def _ssm_prepare(lam_re, lam_im, log_dt, b_re, b_im, c_re, c_im):
    gn, pn = lam_re.shape
    cn = SSM_GROUP
    rows = gn * cn
    rep = lambda a: jnp.broadcast_to(a[:, None, :], (gn, cn, pn)).reshape(rows, pn)
    tr = lambda a: jnp.transpose(a, (0, 2, 1)).reshape(rows, pn)
    ldt = jnp.broadcast_to(log_dt[:, None, None], (gn, cn, pn)).reshape(rows, pn)
    shp = jax.ShapeDtypeStruct((rows, pn), F32)
    ar, ai, bbr, bbi, ncim = pl.pallas_call(
        _ssm_prep_kernel, out_shape=[shp] * 5, name="ssm_prep",
    )(rep(lam_re), rep(lam_im), ldt, tr(b_re), tr(b_im), c_im.reshape(rows, pn))
    nq, gb = gn // SSM_GROUPS_PER_BLOCK, SSM_GROUPS_PER_BLOCK
    diag = (jnp.arange(gb)[:, None, None, None] == jnp.arange(gb)[None, None, :, None])
    spread = lambda a: jnp.where(diag, a.reshape(2, nq, gb, cn, 1, pn), 0.0)
    bb = spread(jnp.stack([bbr, bbi]))
    bblk = jnp.transpose(bb, (1, 2, 3, 0, 4, 5)).reshape(nq, gb * cn, 2 * gb * pn).astype(BF16)
    cc = spread(jnp.stack([c_re.reshape(rows, pn), ncim]))
    cblk = jnp.transpose(cc, (1, 0, 2, 5, 4, 3)).reshape(nq, 2 * gb * pn, gb * cn).astype(BF16)
    a_re = ar.reshape(gn, cn, pn)[:, 0, :].reshape(gn * pn // LANES, LANES)
    a_im = ai.reshape(gn, cn, pn)[:, 0, :].reshape(gn * pn // LANES, LANES)
    return a_re, a_im, bblk, cblk


def _ssm_kernel(u_ref, x0_ref, bblk_ref, cblk_ref, are_ref, aim_ref, d_ref,
                y_ref, sfin_ref, ut_ref, yt_ref, bu_ref, xs_ref, st_ref, *, nb, steps):
    c = pl.program_id(0)
    nq = bblk_ref.shape[0]
    tiles_per_q = 2 * SSM_GROUPS_PER_BLOCK * SSM_STATE // LANES
    half = tiles_per_q // 2

    @pl.when(c == 0)
    def _():
        for j in range(nq * tiles_per_q):
            st_ref[j] = x0_ref[:, j * LANES:(j + 1) * LANES]

    width = nq * LANES
    for s in range(nb if steps > 1 else 1):
        for q in range(nq):
            if steps > 1:
                ut_ref[q, pl.ds(s, steps, stride=nb), :] = u_ref[:, s * width + q * LANES:s * width + (q + 1) * LANES]
            else:
                ut_ref[q] = u_ref[:, q * LANES:(q + 1) * LANES]
    for q in range(nq):
        res = jnp.dot(ut_ref[q].astype(BF16), bblk_ref[q], preferred_element_type=F32)
        for j in range(tiles_per_q):
            bu_ref[q * tiles_per_q + j] = res[:, j * LANES:(j + 1) * LANES]

    def scan_block(q, carry):
        for lt in range(half):
            jr = q * tiles_per_q + lt
            ji = jr + half
            ar = are_ref[q * half + lt]
            ai = aim_ref[q * half + lt]
            xr, xi = st_ref[jr], st_ref[ji]
            for t in range(steps):
                rows = pl.ds(t * nb, nb)
                xr, xi = (ar * xr - ai * xi + bu_ref[jr, rows, :],
                          ar * xi + ai * xr + bu_ref[ji, rows, :])
                xs_ref[jr, rows, :] = xr.astype(BF16)
                xs_ref[ji, rows, :] = xi.astype(BF16)
            st_ref[jr] = xr
            st_ref[ji] = xi
        return carry

    for q in range(nq):
        scan_block(q, 0)

    for q in range(nq):
        xq = jnp.concatenate([xs_ref[q * tiles_per_q + j] for j in range(tiles_per_q)], axis=1)
        yq = jnp.dot(xq, cblk_ref[q], preferred_element_type=F32)
        sl = slice(q * LANES, (q + 1) * LANES)
        yq = yq + d_ref[:, sl] * ut_ref[q]
        if steps > 1:
            yt_ref[q] = yq
        else:
            y_ref[:, sl] = yq
    if steps > 1:
        for s in range(nb):
            for q in range(nq):
                y_ref[:, s * width + q * LANES:s * width + (q + 1) * LANES] = yt_ref[q, pl.ds(s, steps, stride=nb), :]

    @pl.when(c == pl.num_programs(0) - 1)
    def _():
        for j in range(nq * tiles_per_q):
            sfin_ref[:, j * LANES:(j + 1) * LANES] = st_ref[j]


def _ssm(u, x0, a_re, a_im, bblk, cblk, d_skip, *, nb, steps):
    width = d_skip.shape[0]
    nsteps = u.shape[0] // steps if steps > 1 else 1
    blk = (steps, nb * width) if steps > 1 else (nb, width)
    rows = steps * nb
    ntile = a_re.shape[0]
    nstate = 2 * ntile * LANES
    are = jnp.broadcast_to(a_re[:, None, :], (ntile, nb, LANES))
    aim = jnp.broadcast_to(a_im[:, None, :], (ntile, nb, LANES))
    const = lambda shape: pl.BlockSpec(shape, lambda c: (0,) * len(shape))
    return pl.pallas_call(
        functools.partial(_ssm_kernel, nb=nb, steps=steps),
        grid=(nsteps,),
        in_specs=[
            pl.BlockSpec(blk, lambda c: (c, 0)),
            const((nb, nstate)),
            const(bblk.shape),
            const(cblk.shape),
            const(are.shape),
            const(aim.shape),
            const((1, width)),
        ],
        out_specs=[pl.BlockSpec(blk, lambda c: (c, 0)), const((nb, nstate))],
        out_shape=[jax.ShapeDtypeStruct(u.shape, F32), jax.ShapeDtypeStruct((nb, nstate), F32)],
        scratch_shapes=[
            pltpu.VMEM((width // LANES, rows, LANES), F32),
            pltpu.VMEM((width // LANES, rows, LANES), F32),
            pltpu.VMEM((2 * ntile, rows, LANES), F32),
            pltpu.VMEM((2 * ntile, rows, LANES), BF16),
            pltpu.VMEM((2 * ntile, nb, LANES), F32),
        ],
        compiler_params=_params(("arbitrary",)),
        name=f"ssm_nb{nb}",
    )(u, x0, bblk, cblk, are, aim, d_skip.reshape(1, width))


def _state_to_lanes(s_re, s_im):
    b, gn, pn = s_re.shape
    nq = gn // SSM_GROUPS_PER_BLOCK
    st = jnp.stack([s_re.reshape(b, nq, -1), s_im.reshape(b, nq, -1)], axis=2)
    return st.reshape(b, 2 * gn * pn)


def _lanes_to_state(s, gn, pn):
    b = s.shape[0]
    nq = gn // SSM_GROUPS_PER_BLOCK
    st = s.reshape(b, nq, 2, SSM_GROUPS_PER_BLOCK * pn)
    return st[:, :, 0].reshape(b, gn, pn), st[:, :, 1].reshape(b, gn, pn)


def _mix_kernel(x_ref, att_ref, y_ref, ga_ref, gs_ref, gw_ref, gb_ref, wa_ref, ws_ref, wo_ref, out_ref):
    att = jnp.concatenate([att_ref[h] for h in range(HEADS_PER_GROUP)], axis=1)
    a_proj = jnp.dot(att.astype(BF16), wa_ref[...], preferred_element_type=F32)
    zs = jax.nn.gelu(y_ref[...])
    glu = jnp.dot(zs.astype(BF16), gw_ref[...], preferred_element_type=F32) + gb_ref[...]
    s_out = zs * jax.nn.sigmoid(glu)
    s_proj = jnp.dot(s_out.astype(BF16), ws_ref[...], preferred_element_type=F32)
    merged = jax.nn.sigmoid(ga_ref[...]) * a_proj + jax.nn.sigmoid(gs_ref[...]) * s_proj
    out_ref[...] = x_ref[...] + jnp.dot(merged.astype(BF16), wo_ref[...], preferred_element_type=F32)


def _mix(x, att, y, y_map, gates, glu_w, glu_b, wa, ws, wo, *, tm, nt):
    m, d = x.shape
    sw = glu_w.shape[0]
    row = lambda width: pl.BlockSpec((tm, width), lambda i: (i, 0))
    const = lambda a: pl.BlockSpec(a.shape, lambda i: (0, 0), pipeline_mode=pl.Buffered(1))
    glu_b = glu_b.reshape(1, sw)
    in_specs = [row(d),
                pl.BlockSpec((None, HEADS_PER_GROUP, tm, HEAD_DIM), lambda i: (i // nt, 0, i % nt, 0)),
                pl.BlockSpec((tm, sw), y_map),
                pl.BlockSpec((tm, d), lambda i: (i, 0)), pl.BlockSpec((tm, d), lambda i: (i, 1)),
                const(glu_w), const(glu_b), const(wa), const(ws), const(wo)]
    return pl.pallas_call(
        _mix_kernel,
        grid=(m // tm,),
        in_specs=in_specs,
        out_specs=row(d),
        out_shape=jax.ShapeDtypeStruct((m, d), F32),
        compiler_params=_params(("parallel",)),
        name="mix",
    )(x, att, y, gates, gates, glu_w, glu_b, wa, ws, wo)


FFN_TF = 512
PROMPT_TM_FFN = 1024
PROMPT_TM_INPROJ = 1024
PROMPT_TM_MIX = 256
SSM_STEPS = 16


def _layer_weights(p):
    w = dict(p)
    w["qk_gain"] = jnp.concatenate([p["q_norm"].reshape(1, -1), p["k_norm"].reshape(1, -1)], axis=1)
    w["ssm"] = _ssm_prepare(p["ssm_lambda_re"], p["ssm_lambda_im"], p["ssm_log_dt"],
                            p["ssm_b_re"], p["ssm_b_im"], p["ssm_c_re"], p["ssm_c_im"])
    for name in ("glu_w", "w_attn_branch", "w_ssm_branch", "w_out"):
        w[name + "_b"] = p[name].astype(BF16)
    return w


def _prompt_layer(x, w, ffn_w, w_in_b):
    b, l, d = x.shape
    m = b * l
    nh = HEADS_PER_GROUP
    gn, pn = w["ssm_lambda_re"].shape
    d_ff = w["ffn1_w_gate"].shape[1]
    x1 = _ffn(x.reshape(m, d), w["ffn1_norm"], *ffn_w[0], tm=PROMPT_TM_FFN, tf=FFN_TF, d_ff=d_ff)
    cos, sin = _rope_tables(jnp.arange(l, dtype=jnp.int32))
    tm_in = min(PROMPT_TM_INPROJ, l)
    qk, v, kv0, kv1, kv2, u_t, gates = _inproj_prompt(
        x1, w["mix_norm"], w_in_b, w["qk_gain"], cos, sin, b=b, l=l, tm=tm_in)
    att = _attn_prompt(qk, v, b=b, l=l)
    a_re, a_im, bblk, cblk = w["ssm"]
    y_t, sfin = _ssm(u_t, jnp.zeros((b, 2 * gn * pn), F32),
                     a_re, a_im, bblk, cblk, w["ssm_d"], nb=b, steps=SSM_STEPS)
    tm = PROMPT_TM_MIX
    nt = l // tm
    x2 = _mix(x1, att, y_t, lambda i: (i % nt, i // nt), gates,
              w["glu_w_b"], w["glu_b"], w["w_attn_branch_b"], w["w_ssm_branch_b"], w["w_out_b"], tm=tm, nt=nt)
    y = _ffn(x2, w["ffn2_norm"], *ffn_w[1], tm=PROMPT_TM_FFN, tf=FFN_TF, d_ff=d_ff)
    new_kv = []
    for g, (window, _) in enumerate(ATT_GROUPS):
        new_kv.append((kv0, kv1, kv2)[g].reshape(b, 2, min(window, l), nh, HEAD_DIM))
    s_re, s_im = _lanes_to_state(sfin, gn, pn)
    return y.reshape(b, l, d), new_kv, (s_re, s_im)


def _sample_layer(x, pos0, caches, state, w):
    b, l, d = x.shape
    assert l == 1
    nh = HEADS_PER_GROUP
    gn, pn = w["ssm_lambda_re"].shape
    x1, *ffn1_w = _ffn_cast(x.reshape(b, d), w["ffn1_norm"], w["ffn1_w_gate"], w["ffn1_w_up"], w["ffn1_w_down"],
                            tf=FFN_TF)
    cos, sin = _rope_tables(jnp.full((b,), pos0, dtype=jnp.int32))
    qk, _, kn0, kn1, kn2, u, gates, w_in_b = _inproj_sample(
        x1, w["mix_norm"], w["w_in"], w["qk_gain"], cos, sin)
    q = qk[:, :ATT_WIDTH].reshape(b, N_GROUPS, nh, HEAD_DIM)
    kvnews = [kn.reshape(2, b, 1, nh, HEAD_DIM) for kn in (kn0, kn1, kn2)]
    att = _attn_decode(q, caches, kvnews)
    new_kv = [_cache_shift(caches[g], kvnews[g]) for g in range(N_GROUPS)]
    a_re, a_im, bblk, cblk = w["ssm"]
    y, sfin = _ssm(u, _state_to_lanes(*state), a_re, a_im, bblk, cblk, w["ssm_d"], nb=b, steps=1)
    att_h = jnp.transpose(att, (1, 0, 2))[None]
    x2 = _mix(x1, att_h, y, lambda i: (i, 0), gates,
              w["glu_w_b"], w["glu_b"], w["w_attn_branch_b"], w["w_ssm_branch_b"], w["w_out_b"], tm=b, nt=1)
    out, *ffn2_w = _ffn_cast(x2, w["ffn2_norm"], w["ffn2_w_gate"], w["ffn2_w_up"], w["ffn2_w_down"], tf=FFN_TF)
    s_re, s_im = _lanes_to_state(sfin, gn, pn)
    return out.reshape(b, l, d), new_kv, (s_re, s_im), ((ffn1_w, ffn2_w), w_in_b)


def kernel(x_prompt, x_sample, cache_kv_w128, cache_kv_w512, cache_kv_w2048, state_ssm_re, state_ssm_im,
           ffn1_norm, ffn1_w_gate, ffn1_w_up, ffn1_w_down, mix_norm, w_in, q_norm, k_norm,
           ssm_lambda_re, ssm_lambda_im, ssm_log_dt, ssm_b_re, ssm_b_im, ssm_c_re, ssm_c_im, ssm_d,
           glu_w, glu_b, w_attn_branch, w_ssm_branch, w_out,
           ffn2_norm, ffn2_w_gate, ffn2_w_up, ffn2_w_down):
    depth = w_in.shape[0]
    params = dict(ffn1_norm=ffn1_norm, ffn1_w_gate=ffn1_w_gate, ffn1_w_up=ffn1_w_up, ffn1_w_down=ffn1_w_down,
                  mix_norm=mix_norm, w_in=w_in, q_norm=q_norm, k_norm=k_norm,
                  ssm_lambda_re=ssm_lambda_re, ssm_lambda_im=ssm_lambda_im, ssm_log_dt=ssm_log_dt,
                  ssm_b_re=ssm_b_re, ssm_b_im=ssm_b_im, ssm_c_re=ssm_c_re, ssm_c_im=ssm_c_im, ssm_d=ssm_d,
                  glu_w=glu_w, glu_b=glu_b, w_attn_branch=w_attn_branch, w_ssm_branch=w_ssm_branch,
                  w_out=w_out, ffn2_norm=ffn2_norm, ffn2_w_gate=ffn2_w_gate, ffn2_w_up=ffn2_w_up,
                  ffn2_w_down=ffn2_w_down)
    yp, ys = x_prompt, x_sample
    new_p = [[] for _ in range(5)]
    new_s = [[] for _ in range(5)]
    for layer in range(depth):
        w = _layer_weights({k: v[layer] for k, v in params.items()})
        ys, kv_s, st_s, (ffn_w, w_in_b) = _sample_layer(
            ys, PAST_LEN, (cache_kv_w128[layer], cache_kv_w512[layer], cache_kv_w2048[layer]),
            (state_ssm_re[layer], state_ssm_im[layer]), w)
        yp, kv_p, st_p = _prompt_layer(yp, w, ffn_w, w_in_b)
        for i, a in enumerate(list(kv_p) + list(st_p)):
            new_p[i].append(a)
        for i, a in enumerate(list(kv_s) + list(st_s)):
            new_s[i].append(a)
    outs_p = [jnp.stack(a) for a in new_p]
    outs_s = [jnp.stack(a) for a in new_s]
    return (yp, ys, *outs_p, *outs_s)
```
